```python
import jax, jax.numpy as jnp
from jax import lax
import numpy as np

D_MODEL = 2048
BATCH = 2
SEQ = 4096
DEPTH = 2

N_META = 16
D_ATTN = D_MODEL // 2
N_HEADS_SB = 16
HEAD_DIM_SB = D_ATTN // N_HEADS_SB
D_POOL = D_MODEL // 4
POOL_WINDOWS = (2, 4, 8, 16)
N_POOL_GROUPS = len(POOL_WINDOWS)
POOL_GROUP_DIM = D_POOL // N_POOL_GROUPS
D_CONV = D_MODEL // 4
CONV_WIDTH = 31
D_MIX = D_ATTN + D_POOL + D_CONV
D_IN_PROJ = 3 * D_ATTN + D_POOL + 2 * D_CONV
D_FF = ((8 * D_MODEL + 3 * 256 - 1) // (3 * 256)) * 256
BLOCK_Q = 128
EPS = 1e-6

kernel_name = "hybrid_stickbreak_pool_conformer_trunk"


def rmsnorm(x, g):
    xf = x.astype(jnp.float32)
    y = xf * lax.rsqrt(jnp.mean(xf * xf, axis=-1, keepdims=True) + EPS)
    return (y * g.astype(jnp.float32)).astype(x.dtype)


def layernorm(x, g, b):
    xf = x.astype(jnp.float32)
    mu = jnp.mean(xf, axis=-1, keepdims=True)
    var = jnp.mean(jnp.square(xf - mu), axis=-1, keepdims=True)
    y = (xf - mu) * lax.rsqrt(var + EPS)
    return (y * g.astype(jnp.float32) + b.astype(jnp.float32)).astype(x.dtype)


def stick_breaking_attention(q, k, v):
    b, l, h, dh = q.shape
    pad = (-l) % BLOCK_Q
    padw = ((0, 0), (pad, 0), (0, 0), (0, 0))
    qf = jnp.pad(q.astype(jnp.float32) * (dh ** -0.5), padw)
    kf = jnp.pad(k.astype(jnp.float32), padw)
    vf = jnp.pad(v.astype(jnp.float32), padw)
    lp = l + pad
    n_blocks = lp // BLOCK_Q
    key_pos = jnp.arange(lp) - pad

    def one_block(i):
        q_blk = lax.dynamic_slice_in_dim(qf, i * BLOCK_Q, BLOCK_Q, axis=1)
        q_pos = i * BLOCK_Q + jnp.arange(BLOCK_Q) - pad
        z = jnp.einsum('bqhd,bkhd->bhqk', q_blk, kf)
        mask = (key_pos[None, :] < q_pos[:, None]) & (key_pos[None, :] >= 0)
        log_not_beta = jnp.where(mask, -jax.nn.softplus(z), 0.0)
        later = lax.cumsum(log_not_beta, axis=3, reverse=True) - log_not_beta
        weights = jnp.where(mask, jnp.exp(jax.nn.log_sigmoid(z) + later), 0.0)
        return jnp.einsum('bhqk,bkhd->bqhd', weights, vf)

    out = lax.map(one_block, jnp.arange(n_blocks))
    out = jnp.moveaxis(out, 0, 1).reshape(b, lp, h, dh)[:, pad:]
    return out.astype(v.dtype)


def multiscale_pool(u, w_pool, pool_scale):
    b, l, _ = u.shape
    uf = u.astype(jnp.float32)
    cs = jnp.pad(jnp.cumsum(uf, axis=1), ((0, 0), (1, 0), (0, 0)))
    t = jnp.arange(l)
    groups = []
    for g, w in enumerate(POOL_WINDOWS):
        lo, hi = g * POOL_GROUP_DIM, (g + 1) * POOL_GROUP_DIM
        start = jnp.maximum(t + 1 - w, 0)
        csg = cs[:, :, lo:hi]
        window_sum = csg[:, 1:] - csg[:, start]
        count = (t + 1 - start).astype(jnp.float32)[None, :, None]
        groups.append(window_sum / count - uf[:, :, lo:hi])
    pooled = jnp.stack(groups, axis=2)
    mixed = jnp.einsum('blgc,gcd->blgd', pooled, w_pool.astype(jnp.float32))
    mixed = mixed.reshape(b, l, D_POOL) * pool_scale.astype(jnp.float32)
    return mixed.astype(u.dtype)


def conformer_conv(u_glu, w_dw, b_dw, ln_g, ln_b, w_pw):
    a, gate = jnp.split(u_glu, 2, axis=-1)
    u = a * jax.nn.sigmoid(gate)
    y = lax.conv_general_dilated(
        u, w_dw[:, None, :].astype(u.dtype), window_strides=(1,),
        padding=[(CONV_WIDTH - 1, 0)],
        dimension_numbers=('NWC', 'WIO', 'NWC'),
        feature_group_count=D_CONV) + b_dw
    y = layernorm(y, ln_g, ln_b)
    y = jax.nn.silu(y)
    return y @ w_pw


def setup_inputs(seed: int = 0) -> dict:
    key = jax.random.key(seed)
    ks = jax.random.split(key, 20)
    f32 = jnp.float32

    def nrm(k, shape, scale):
        return jax.random.normal(k, shape, f32) * scale

    def gain(k, shape):
        return 1.0 + 0.05 * jax.random.normal(k, shape, f32)

    return {
        "x": nrm(ks[0], (BATCH, SEQ, D_MODEL), 1.0),
        "meta_tokens": nrm(ks[1], (N_META, D_MODEL), 1.0),
        "pre_mix_g": gain(ks[2], (DEPTH, D_MODEL)),
        "w_in": nrm(ks[3], (DEPTH, D_MODEL, D_IN_PROJ), D_MODEL ** -0.5),
        "w_pool": nrm(ks[4], (DEPTH, N_POOL_GROUPS, POOL_GROUP_DIM, POOL_GROUP_DIM), POOL_GROUP_DIM ** -0.5),
        "pool_scale": gain(ks[5], (DEPTH, D_POOL)),
        "w_dw": nrm(ks[6], (DEPTH, CONV_WIDTH, D_CONV), CONV_WIDTH ** -0.5),
        "b_dw": nrm(ks[7], (DEPTH, D_CONV), 0.02),
        "conv_ln_g": gain(ks[8], (DEPTH, D_CONV)),
        "conv_ln_b": nrm(ks[9], (DEPTH, D_CONV), 0.02),
        "w_pw": nrm(ks[10], (DEPTH, D_CONV, D_CONV), D_CONV ** -0.5),
        "mix_out_g": gain(ks[11], (DEPTH, D_MIX)),
        "w_out": nrm(ks[12], (DEPTH, D_MIX, D_MODEL), D_MIX ** -0.5),
        "post_mix_g": gain(ks[13], (DEPTH, D_MODEL)),
        "pre_ffn_g": gain(ks[14], (DEPTH, D_MODEL)),
        "w_gate": nrm(ks[15], (DEPTH, D_MODEL, D_FF), D_MODEL ** -0.5),
        "w_up": nrm(ks[16], (DEPTH, D_MODEL, D_FF), D_MODEL ** -0.5),
        "w_down": nrm(ks[17], (DEPTH, D_FF, D_MODEL), D_FF ** -0.5),
        "post_ffn_g": gain(ks[18], (DEPTH, D_MODEL)),
    }


def reference(x, meta_tokens, pre_mix_g, w_in, w_pool, pool_scale, w_dw, b_dw,
              conv_ln_g, conv_ln_b, w_pw, mix_out_g, w_out, post_mix_g,
              pre_ffn_g, w_gate, w_up, w_down, post_ffn_g):
    b = x.shape[0]
    meta = jnp.broadcast_to(meta_tokens[None].astype(x.dtype), (b, N_META, D_MODEL))
    h = jnp.concatenate([meta, x], axis=1)
    l = h.shape[1]
    splits = [D_ATTN, 2 * D_ATTN, 3 * D_ATTN, 3 * D_ATTN + D_POOL]

    for i in range(DEPTH):
        u = rmsnorm(h, pre_mix_g[i])
        proj = u @ w_in[i]
        q, k, v, u_pool, u_conv = jnp.split(proj, splits, axis=-1)
        heads = (b, l, N_HEADS_SB, HEAD_DIM_SB)
        o_attn = stick_breaking_attention(q.reshape(heads), k.reshape(heads), v.reshape(heads))
        o_attn = o_attn.reshape(b, l, D_ATTN)
        o_pool = multiscale_pool(u_pool, w_pool[i], pool_scale[i])
        o_conv = conformer_conv(u_conv, w_dw[i], b_dw[i], conv_ln_g[i], conv_ln_b[i], w_pw[i])
        g = mix_out_g[i]
        merged = jnp.concatenate([
            rmsnorm(o_attn, g[:D_ATTN]),
            rmsnorm(o_pool, g[D_ATTN:D_ATTN + D_POOL]),
            rmsnorm(o_conv, g[D_ATTN + D_POOL:]),
        ], axis=-1)
        h = h + rmsnorm(merged @ w_out[i], post_mix_g[i])

        u = rmsnorm(h, pre_ffn_g[i])
        ff = (jax.nn.silu(u @ w_gate[i]) * (u @ w_up[i])) @ w_down[i]
        h = h + rmsnorm(ff, post_ffn_g[i])

    return h[:, N_META:]
```

```python
import functools

import jax
import jax.numpy as jnp
from jax import lax
from jax.experimental import pallas as pl
from jax.experimental.pallas import tpu as pltpu

N_META = 16
N_HEADS_SB = 16
POOL_WINDOWS = (2, 4, 8, 16)
CONV_WIDTH = 31
EPS = 1e-6

LANES = 128
VMEM_BYTES_V7X = 64 * 1024 * 1024
VMEM_LIMIT = VMEM_BYTES_V7X - 8 * 1024 * 1024

SEQ_TILE = 128
ROW_ALIGN = 384
PROJ_ROWS = 384
MIX_ROWS = 384
FFN_ROWS = 768
FFN_COLS = 512
CONV_HALO = 32
POOL_HALO = 16
CONV_CHUNK = 32

LOG_WEIGHT_FLOOR = -88.0

F32 = jnp.float32
BF16 = jnp.bfloat16


def _rms(x, g):
    return x * lax.rsqrt(jnp.mean(x * x, axis=-1, keepdims=True) + EPS) * g


def _resident(shape):
    return pl.BlockSpec(shape, lambda *_: (0,) * len(shape), pipeline_mode=pl.Buffered(1))


def _in_proj_kernel(h_ref, g_ref, w_ref, q_ref, k_ref, v_ref, pool_ref, conv_ref, *, d_attn, d_pool, q_scale):
    u = _rms(h_ref[...], g_ref[...]).astype(BF16)
    o = 0
    q_ref[...] = (jnp.dot(u, w_ref[:, o:o + d_attn], preferred_element_type=F32) * q_scale).astype(BF16)
    o += d_attn
    k_ref[...] = jnp.dot(u, w_ref[:, o:o + d_attn], preferred_element_type=F32).astype(BF16)
    o += d_attn
    v_ref[...] = jnp.dot(u, w_ref[:, o:o + d_attn], preferred_element_type=F32).astype(BF16)
    o += d_attn
    pool_ref[...] = jnp.dot(u, w_ref[:, o:o + d_pool], preferred_element_type=F32)
    o += d_pool
    conv_ref[...] = jnp.dot(u, w_ref[:, o:], preferred_element_type=F32)


def _in_proj(h, g, w, d_attn, d_pool, d_conv2):
    rows, d = h.shape
    tm = PROJ_ROWS
    row_spec = lambda n: pl.BlockSpec((tm, n), lambda i: (i, 0))
    head_dim = d_attn // N_HEADS_SB
    return pl.pallas_call(
        functools.partial(_in_proj_kernel, d_attn=d_attn, d_pool=d_pool, q_scale=head_dim ** -0.5),
        grid=(rows // tm,),
        in_specs=[row_spec(d), _resident((1, d)), _resident(w.shape)],
        out_specs=[row_spec(d_attn), row_spec(d_attn), row_spec(d_attn), row_spec(d_pool), row_spec(d_conv2)],
        out_shape=[jax.ShapeDtypeStruct((rows, d_attn), BF16)] * 3
        + [jax.ShapeDtypeStruct((rows, d_pool), F32), jax.ShapeDtypeStruct((rows, d_conv2), F32)],
        compiler_params=pltpu.CompilerParams(dimension_semantics=("arbitrary",), vmem_limit_bytes=VMEM_LIMIT),
        name="in_proj",
    )(h, g, w)


def _attn_kernel(q_ref, k_ref, v_ref, tri_ref, g_ref, o_ref, acc_ref, *, n_pairs):
    t = SEQ_TILE
    qi = pl.program_id(1)
    lane = lax.broadcasted_iota(jnp.int32, (t, LANES), 1)
    first_head = lane < LANES // 2
    row = lax.broadcasted_iota(jnp.int32, (t, 2 * t), 0)
    col = lax.broadcasted_iota(jnp.int32, (t, 2 * t), 1)
    causal = (col & (t - 1)) < row
    left = col < t
    tri = tri_ref[...]
    zero = jnp.zeros((), BF16)

    def split_heads(x):
        return jnp.concatenate([jnp.where(first_head, x, zero), jnp.where(first_head, zero, x)], axis=0)

    for p in range(n_pairs):
        lanes = slice(p * LANES, (p + 1) * LANES)
        q = q_ref[:, lanes]

        def tile(kb, c0, c1, acc, diagonal):
            start = pl.multiple_of(kb * t, t)
            k2 = split_heads(k_ref[pl.ds(start, t), lanes])
            v2 = split_heads(v_ref[pl.ds(start, t), lanes])
            z = lax.dot_general(q, k2, (((1,), (1,)), ((), ())), preferred_element_type=F32)
            soft = jnp.log(1.0 + jnp.exp(-jnp.abs(z)))
            log_not_beta = -(jnp.maximum(z, 0.0) + soft)
            log_beta = jnp.minimum(z, 0.0) - soft
            if diagonal:
                log_not_beta = jnp.where(causal, log_not_beta, 0.0)
            hi = log_not_beta.astype(BF16)
            lo = (log_not_beta - hi.astype(F32)).astype(BF16)
            later = jnp.dot(hi, tri, preferred_element_type=F32) + jnp.dot(lo, tri, preferred_element_type=F32)
            w = jnp.exp(log_beta + later + jnp.where(left, c0, c1))
            if diagonal:
                w = jnp.where(causal, w, 0.0)
            acc = acc + jnp.dot(w.astype(BF16), v2, preferred_element_type=F32)
            c0 = c0 + jnp.sum(log_not_beta[:, :t], axis=1, keepdims=True)
            c1 = c1 + jnp.sum(log_not_beta[:, t:], axis=1, keepdims=True)
            return c0, c1, acc

        zc = jnp.zeros((t, 1), F32)
        c0, c1, acc = tile(qi, zc, zc, jnp.zeros((t, LANES), F32), True)

        def cond(s):
            kb, _, _, _, cmax = s
            return jnp.logical_and(kb >= 0, cmax > LOG_WEIGHT_FLOOR)

        def body(s):
            kb, c0, c1, acc, _ = s
            c0, c1, acc = tile(kb, c0, c1, acc, False)
            return kb - 1, c0, c1, acc, jnp.max(jnp.maximum(c0, c1))

        _, _, _, acc, _ = lax.while_loop(cond, body, (qi - 1, c0, c1, acc, jnp.max(jnp.maximum(c0, c1))))
        acc_ref[:, lanes] = acc

    o_ref[...] = _rms(acc_ref[...], g_ref[...]).astype(BF16)


def _attention(q, k, v, g, batch, rows_per_batch):
    rows, d_attn = q.shape
    t = SEQ_TILE
    nq = rows_per_batch // t
    j = jnp.arange(2 * t)
    tri = ((j[:, None] // t == j[None, :] // t) & (j[:, None] > j[None, :])).astype(BF16)
    kv_spec = pl.BlockSpec((rows_per_batch, d_attn), lambda b, i: (b, 0), pipeline_mode=pl.Buffered(1))
    tile_spec = pl.BlockSpec((t, d_attn), lambda b, i: (b * nq + i, 0))
    return pl.pallas_call(
        functools.partial(_attn_kernel, n_pairs=d_attn // LANES),
        grid=(batch, nq),
        in_specs=[tile_spec, kv_spec, kv_spec, _resident((2 * t, 2 * t)), _resident((1, d_attn))],
        out_specs=tile_spec,
        out_shape=jax.ShapeDtypeStruct((rows, d_attn), BF16),
        scratch_shapes=[pltpu.VMEM((t, d_attn), F32)],
        compiler_params=pltpu.CompilerParams(dimension_semantics=("arbitrary", "arbitrary"), vmem_limit_bytes=VMEM_LIMIT),
        name="sb_attention",
    )(q, k, v, tri, g)


def _mixers_kernel(pool_ref, pool_halo_ref, conv_ref, conv_halo_ref, w_pool_ref, pool_scale_ref, g_pool_ref,
                   w_dw_ref, b_dw_ref, ln_g_ref, ln_b_ref, w_pw_ref, g_conv_ref,
                   o_pool_ref, o_conv_ref, pool_ext, conv_ext, y_ref, *, d_conv):
    tl = pool_ref.shape[0]
    ti = pl.program_id(1)
    has_history = ti > 0

    pool_ext[:POOL_HALO, :] = jnp.where(has_history, pool_halo_ref[...], 0.0)
    pool_ext[POOL_HALO:, :] = pool_ref[...]
    pos = ti * tl + lax.broadcasted_iota(jnp.int32, (tl, 1), 0)
    mixed = []
    sq = jnp.zeros((tl, 1), F32)
    for gi, window in enumerate(POOL_WINDOWS):
        lanes = slice(gi * LANES, (gi + 1) * LANES)
        u = pool_ext[POOL_HALO:, lanes]
        acc = u
        for back in range(1, window):
            acc = acc + pool_ext[POOL_HALO - back:POOL_HALO - back + tl, lanes]
        count = jnp.minimum(pos + 1, window).astype(F32)
        pooled = acc / count - u
        m = jnp.dot(pooled.astype(BF16), w_pool_ref[gi], preferred_element_type=F32) * pool_scale_ref[:, lanes]
        sq = sq + jnp.sum(m * m, axis=-1, keepdims=True)
        mixed.append(m)
    d_pool = LANES * len(POOL_WINDOWS)
    inv = lax.rsqrt(sq / d_pool + EPS)
    for gi, m in enumerate(mixed):
        lanes = slice(gi * LANES, (gi + 1) * LANES)
        o_pool_ref[:, lanes] = (m * inv * g_pool_ref[:, lanes]).astype(BF16)

    def glu(x):
        return x[:, :d_conv] * jax.nn.sigmoid(x[:, d_conv:])

    conv_ext[:CONV_HALO, :] = jnp.where(has_history, glu(conv_halo_ref[...]), 0.0)
    conv_ext[CONV_HALO:, :] = glu(conv_ref[...])
    first_tap = CONV_HALO - (CONV_WIDTH - 1)

    for base in range(0, tl, CONV_CHUNK):
        acc = jnp.broadcast_to(b_dw_ref[...], (CONV_CHUNK, d_conv))
        for tap in range(CONV_WIDTH):
            lo = base + first_tap + tap
            acc = acc + w_dw_ref[tap:tap + 1, :] * conv_ext[lo:lo + CONV_CHUNK, :]
        y_ref[base:base + CONV_CHUNK, :] = acc
    y = y_ref[...]
    mu = jnp.mean(y, axis=-1, keepdims=True)
    yc = y - mu
    var = jnp.mean(yc * yc, axis=-1, keepdims=True)
    y = yc * lax.rsqrt(var + EPS) * ln_g_ref[...] + ln_b_ref[...]
    y = y * jax.nn.sigmoid(y)
    o = jnp.dot(y.astype(BF16), w_pw_ref[...], preferred_element_type=F32)
    o_conv_ref[...] = _rms(o, g_conv_ref[...]).astype(BF16)


def _mixers(u_pool, u_conv, w_pool, pool_scale, g_pool, w_dw, b_dw, ln_g, ln_b, w_pw, g_conv, batch, rows_per_batch):
    rows, d_pool = u_pool.shape
    d_conv = u_conv.shape[1] // 2
    tl = MIX_ROWS
    nt = rows_per_batch // tl

    def cur(n):
        return pl.BlockSpec((tl, n), lambda b, i: (b * nt + i, 0))

    def halo(n, h):
        return pl.BlockSpec((h, n), lambda b, i: (jnp.maximum((b * nt + i) * (tl // h) - 1, 0), 0))

    return pl.pallas_call(
        functools.partial(_mixers_kernel, d_conv=d_conv),
        grid=(batch, nt),
        in_specs=[cur(d_pool), halo(d_pool, POOL_HALO), cur(2 * d_conv), halo(2 * d_conv, CONV_HALO),
                  _resident(w_pool.shape), _resident((1, d_pool)), _resident((1, d_pool)),
                  _resident(w_dw.shape), _resident((1, d_conv)), _resident((1, d_conv)), _resident((1, d_conv)),
                  _resident(w_pw.shape), _resident((1, d_conv))],
        out_specs=[cur(d_pool), cur(d_conv)],
        out_shape=[jax.ShapeDtypeStruct((rows, d_pool), BF16), jax.ShapeDtypeStruct((rows, d_conv), BF16)],
        scratch_shapes=[pltpu.VMEM((POOL_HALO + tl, d_pool), F32), pltpu.VMEM((CONV_HALO + tl, d_conv), F32),
                        pltpu.VMEM((tl, d_conv), F32)],
        compiler_params=pltpu.CompilerParams(dimension_semantics=("arbitrary", "arbitrary"), vmem_limit_bytes=VMEM_LIMIT),
        name="pool_conv_mixers",
    )(u_pool, u_pool, u_conv, u_conv, w_pool, pool_scale, g_pool, w_dw, b_dw, ln_g, ln_b, w_pw, g_conv)


def _out_proj_kernel(attn_ref, pool_ref, conv_ref, h_ref, w_ref, g_ref, o_ref):
    d_attn, d_pool = attn_ref.shape[1], pool_ref.shape[1]
    y = jnp.dot(attn_ref[...], w_ref[:d_attn, :], preferred_element_type=F32)
    y = y + jnp.dot(pool_ref[...], w_ref[d_attn:d_attn + d_pool, :], preferred_element_type=F32)
    y = y + jnp.dot(conv_ref[...], w_ref[d_attn + d_pool:, :], preferred_element_type=F32)
    o_ref[...] = h_ref[...] + _rms(y, g_ref[...])


def _out_proj(attn, pool, conv, h, w, g):
    rows, d = h.shape
    tm = PROJ_ROWS
    row_spec = lambda n: pl.BlockSpec((tm, n), lambda i: (i, 0))
    return pl.pallas_call(
        _out_proj_kernel,
        grid=(rows // tm,),
        in_specs=[row_spec(attn.shape[1]), row_spec(pool.shape[1]), row_spec(conv.shape[1]), row_spec(d),
                  _resident(w.shape), _resident((1, d))],
        out_specs=row_spec(d),
        out_shape=jax.ShapeDtypeStruct((rows, d), F32),
        compiler_params=pltpu.CompilerParams(dimension_semantics=("arbitrary",), vmem_limit_bytes=VMEM_LIMIT),
        name="out_proj",
    )(attn, pool, conv, h, w, g)


def _ffn_kernel(h_ref, g_pre_ref, w_gate_ref, w_up_ref, w_down_ref, g_post_ref, o_ref, u_ref, acc_ref):
    c = pl.program_id(1)

    @pl.when(c == 0)
    def _():
        u_ref[...] = _rms(h_ref[...], g_pre_ref[...]).astype(BF16)
        acc_ref[...] = jnp.zeros_like(acc_ref)

    u = u_ref[...]
    gate = jnp.dot(u, w_gate_ref[...], preferred_element_type=F32)
    up = jnp.dot(u, w_up_ref[...], preferred_element_type=F32)
    a = (gate * jax.nn.sigmoid(gate) * up).astype(BF16)
    tf = w_down_ref.shape[0]
    for n in range(0, acc_ref.shape[1], tf):
        acc_ref[:, n:n + tf] += jnp.dot(a, w_down_ref[:, n:n + tf], preferred_element_type=F32)

    @pl.when(c == pl.num_programs(1) - 1)
    def _():
        o_ref[...] = h_ref[...] + _rms(acc_ref[...], g_post_ref[...])


def _ffn(h, g_pre, w_gate, w_up, w_down, g_post):
    rows, d = h.shape
    d_ff = w_gate.shape[1]
    tm, tf = FFN_ROWS, FFN_COLS
    return pl.pallas_call(
        _ffn_kernel,
        grid=(rows // tm, d_ff // tf),
        in_specs=[pl.BlockSpec((tm, d), lambda i, c: (i, 0)), _resident((1, d)),
                  pl.BlockSpec((d, tf), lambda i, c: (0, c)), pl.BlockSpec((d, tf), lambda i, c: (0, c)),
                  pl.BlockSpec((tf, d), lambda i, c: (c, 0)), _resident((1, d))],
        out_specs=pl.BlockSpec((tm, d), lambda i, c: (i, 0)),
        out_shape=jax.ShapeDtypeStruct((rows, d), F32),
        scratch_shapes=[pltpu.VMEM((tm, d), BF16), pltpu.VMEM((tm, d), F32)],
        compiler_params=pltpu.CompilerParams(dimension_semantics=("arbitrary", "arbitrary"), vmem_limit_bytes=VMEM_LIMIT),
        name="swiglu_ffn",
    )(h, g_pre, w_gate, w_up, w_down, g_post)


def kernel(x, meta_tokens, pre_mix_g, w_in, w_pool, pool_scale, w_dw, b_dw, conv_ln_g, conv_ln_b, w_pw, mix_out_g,
           w_out, post_mix_g, pre_ffn_g, w_gate, w_up, w_down, post_ffn_g):
    batch, seq, d = x.shape
    depth = w_in.shape[0]
    d_pool = pool_scale.shape[1]
    d_conv = w_pw.shape[1]
    d_attn = w_out.shape[1] - d_pool - d_conv
    assert d_pool == LANES * len(POOL_WINDOWS) and d_attn % (2 * LANES) == 0
    assert d_attn // N_HEADS_SB == LANES // 2

    length = N_META + seq
    rows_per_batch = -(-length // ROW_ALIGN) * ROW_ALIGN
    assert rows_per_batch % SEQ_TILE == 0 and (batch * rows_per_batch) % FFN_ROWS == 0
    meta = jnp.broadcast_to(meta_tokens[None].astype(x.dtype), (batch, N_META, d))
    tail = jnp.zeros((batch, rows_per_batch - length, d), x.dtype)
    h = jnp.concatenate([meta, x, tail], axis=1).reshape(batch * rows_per_batch, d)

    row = lambda a: a.reshape(1, -1)
    for i in range(depth):
        q, k, v, u_pool, u_conv = _in_proj(h, row(pre_mix_g[i]), w_in[i].astype(BF16), d_attn, d_pool, 2 * d_conv)
        g = mix_out_g[i]
        o_attn = _attention(q, k, v, row(g[:d_attn]), batch, rows_per_batch)
        o_pool, o_conv = _mixers(
            u_pool, u_conv, w_pool[i].astype(BF16), row(pool_scale[i]), row(g[d_attn:d_attn + d_pool]),
            w_dw[i], row(b_dw[i]), row(conv_ln_g[i]), row(conv_ln_b[i]), w_pw[i].astype(BF16),
            row(g[d_attn + d_pool:]), batch, rows_per_batch)
        h = _out_proj(o_attn, o_pool, o_conv, h, w_out[i].astype(BF16), row(post_mix_g[i]))
        h = _ffn(h, row(pre_ffn_g[i]), w_gate[i].astype(BF16), w_up[i].astype(BF16), w_down[i].astype(BF16),
                 row(post_ffn_g[i]))

    return h.reshape(batch, rows_per_batch, d)[:, N_META:length]
```

```python
import functools

import jax
import jax.numpy as jnp
from jax import lax
from jax.experimental import pallas as pl
from jax.experimental.pallas import tpu as pltpu

N_META = 16
N_HEADS_SB = 16
POOL_WINDOWS = (2, 4, 8, 16)
CONV_WIDTH = 31
EPS = 1e-6

LANES = 128
VMEM_BYTES_V7X = 64 * 1024 * 1024
VMEM_LIMIT = VMEM_BYTES_V7X - 8 * 1024 * 1024

SEQ_TILE = 128
ROW_ALIGN = 384
PROJ_ROWS = 384
MIX_ROWS = 384
FFN_ROWS = 768
FFN_COLS = 512
CONV_HALO = 32
POOL_HALO = 16
CONV_CHUNK = 32

LOG_WEIGHT_FLOOR = -88.0

F32 = jnp.float32
BF16 = jnp.bfloat16


def _rms(x, g):
    return x * lax.rsqrt(jnp.mean(x * x, axis=-1, keepdims=True) + EPS) * g


def _resident(shape):
    return pl.BlockSpec(shape, lambda *_: (0,) * len(shape), pipeline_mode=pl.Buffered(1))


def _layer_resident(layer, shape):
    return pl.BlockSpec((None,) + tuple(shape), lambda *_: (layer,) + (0,) * len(shape),
                        pipeline_mode=pl.Buffered(1))


def _params(n_grid_axes):
    return pltpu.CompilerParams(dimension_semantics=("arbitrary",) * n_grid_axes, vmem_limit_bytes=VMEM_LIMIT)


def _in_proj_kernel(h_ref, g_ref, w_ref, q_ref, k_ref, v_ref, pool_ref, conv_ref, *, d_attn, d_pool, q_scale):
    u = _rms(h_ref[...], g_ref[...]).astype(BF16)
    o = 0
    q_ref[...] = (jnp.dot(u, w_ref[:, o:o + d_attn], preferred_element_type=F32) * q_scale).astype(BF16)
    o += d_attn
    k_ref[...] = jnp.dot(u, w_ref[:, o:o + d_attn], preferred_element_type=F32).astype(BF16)
    o += d_attn
    v_ref[...] = jnp.dot(u, w_ref[:, o:o + d_attn], preferred_element_type=F32).astype(BF16)
    o += d_attn
    pool_ref[...] = jnp.dot(u, w_ref[:, o:o + d_pool], preferred_element_type=F32)
    o += d_pool
    conv_ref[...] = jnp.dot(u, w_ref[:, o:], preferred_element_type=F32)


def _in_proj(layer, h, g, w, d_attn, d_pool, d_conv2):
    rows, d = h.shape
    tm = PROJ_ROWS
    row_spec = lambda n: pl.BlockSpec((tm, n), lambda i: (i, 0))
    head_dim = d_attn // N_HEADS_SB
    return pl.pallas_call(
        functools.partial(_in_proj_kernel, d_attn=d_attn, d_pool=d_pool, q_scale=head_dim ** -0.5),
        grid=(rows // tm,),
        in_specs=[row_spec(d), _layer_resident(layer, (1, d)), _layer_resident(layer, w.shape[1:])],
        out_specs=[row_spec(d_attn), row_spec(d_attn), row_spec(d_attn), row_spec(d_pool), row_spec(d_conv2)],
        out_shape=[jax.ShapeDtypeStruct((rows, d_attn), BF16)] * 3
        + [jax.ShapeDtypeStruct((rows, d_pool), F32), jax.ShapeDtypeStruct((rows, d_conv2), F32)],
        compiler_params=_params(1),
        name="in_proj",
    )(h, g, w)


def _attn_kernel(q_ref, k_ref, v_ref, tri_ref, g_ref, o_ref, acc_ref, c_ref, *, n_pairs):
    t = SEQ_TILE
    qi = pl.program_id(1)
    lane = lax.broadcasted_iota(jnp.int32, (t, LANES), 1)
    first_head = lane < LANES // 2
    row = lax.broadcasted_iota(jnp.int32, (t, 2 * t), 0)
    col = lax.broadcasted_iota(jnp.int32, (t, 2 * t), 1)
    causal = (col & (t - 1)) < row
    zero = jnp.zeros((), BF16)

    def split_heads(x):
        return jnp.concatenate([jnp.where(first_head, x, zero), jnp.where(first_head, zero, x)], axis=0)

    def key_tile(kb, diagonal):
        start = pl.multiple_of(kb * t, t)
        pairs = range(n_pairs)
        lanes = [slice(p * LANES, (p + 1) * LANES) for p in pairs]
        z = [lax.dot_general(q_ref[:, lanes[p]], split_heads(k_ref[pl.ds(start, t), lanes[p]]),
                             (((1,), (1,)), ((), ())), preferred_element_type=F32) for p in pairs]
        v2 = [split_heads(v_ref[pl.ds(start, t), lanes[p]]) for p in pairs]
        c_old = None if diagonal else [(c_ref[2 * p], c_ref[2 * p + 1]) for p in pairs]
        soft = [jnp.maximum(z[p], 0.0) + jnp.log(1.0 + jnp.exp(-jnp.abs(z[p]))) for p in pairs]
        if diagonal:
            soft = [jnp.where(causal, s, 0.0) for s in soft]
        hi = [s.astype(BF16) for s in soft]
        lo = [(soft[p] - hi[p].astype(F32)).astype(BF16) for p in pairs]
        later = [jnp.dot(jnp.concatenate([hi[p], lo[p]], axis=1), tri_ref[...], preferred_element_type=F32)
                 for p in pairs]
        c_new, w = [], []
        for p in pairs:
            x = z[p] - soft[p] - later[p]
            c0 = jnp.sum(soft[p][:, :t], axis=1, keepdims=True)
            c1 = jnp.sum(soft[p][:, t:], axis=1, keepdims=True)
            if not diagonal:
                x = jnp.concatenate([x[:, :t] - c_old[p][0], x[:, t:] - c_old[p][1]], axis=1)
                c0, c1 = c_old[p][0] + c0, c_old[p][1] + c1
            wp = jnp.exp(x)
            if diagonal:
                wp = jnp.where(causal, wp, 0.0)
            w.append(wp.astype(BF16))
            c_new.append((c0, c1))
        o = [jnp.dot(w[p], v2[p], preferred_element_type=F32) for p in pairs]
        c_min = None
        for p in pairs:
            if diagonal:
                acc_ref[:, lanes[p]] = o[p]
            else:
                acc_ref[:, lanes[p]] += o[p]
            c_ref[2 * p], c_ref[2 * p + 1] = c_new[p]
            m = jnp.minimum(*c_new[p])
            c_min = m if c_min is None else jnp.minimum(c_min, m)
        return jnp.min(c_min)

    def cond(s):
        kb, c_min = s
        return jnp.logical_and(kb >= 0, c_min < -LOG_WEIGHT_FLOOR)

    def body(s):
        kb, _ = s
        return kb - 1, key_tile(kb, False)

    lax.while_loop(cond, body, (qi - 1, key_tile(qi, True)))
    o_ref[...] = _rms(acc_ref[...], g_ref[...]).astype(BF16)


def _attention(layer, q, k, v, g, batch, rows_per_batch):
    rows, d_attn = q.shape
    t = SEQ_TILE
    nq = rows_per_batch // t
    j = jnp.arange(2 * t)
    tri = ((j[:, None] // t == j[None, :] // t) & (j[:, None] > j[None, :])).astype(BF16)
    tri = jnp.concatenate([tri, tri], axis=0)
    kv_spec = pl.BlockSpec((rows_per_batch, d_attn), lambda b, i: (b, 0), pipeline_mode=pl.Buffered(1))
    tile_spec = pl.BlockSpec((t, d_attn), lambda b, i: (b * nq + i, 0))
    return pl.pallas_call(
        functools.partial(_attn_kernel, n_pairs=d_attn // LANES),
        grid=(batch, nq),
        in_specs=[tile_spec, kv_spec, kv_spec, _resident(tri.shape), _layer_resident(layer, (1, d_attn))],
        out_specs=tile_spec,
        out_shape=jax.ShapeDtypeStruct((rows, d_attn), BF16),
        scratch_shapes=[pltpu.VMEM((t, d_attn), F32), pltpu.VMEM((2 * (d_attn // LANES), t, 1), F32)],
        compiler_params=_params(2),
        name="sb_attention",
    )(q, k, v, tri, g)


def _mixers_kernel(pool_ref, pool_halo_ref, conv_ref, conv_halo_ref, w_pool_ref, pool_scale_ref, g_pool_ref,
                   w_dw_ref, b_dw_ref, ln_g_ref, ln_b_ref, w_pw_ref, g_conv_ref,
                   o_pool_ref, o_conv_ref, pool_ext, conv_ext, y_ref, *, d_conv):
    tl = pool_ref.shape[0]
    ti = pl.program_id(1)
    has_history = ti > 0

    pool_ext[:POOL_HALO, :] = jnp.where(has_history, pool_halo_ref[...], 0.0)
    pool_ext[POOL_HALO:, :] = pool_ref[...]
    pos = ti * tl + lax.broadcasted_iota(jnp.int32, (tl, 1), 0)
    mixed = []
    sq = jnp.zeros((tl, 1), F32)
    for gi, window in enumerate(POOL_WINDOWS):
        lanes = slice(gi * LANES, (gi + 1) * LANES)
        u = pool_ext[POOL_HALO:, lanes]
        acc = u
        for back in range(1, window):
            acc = acc + pool_ext[POOL_HALO - back:POOL_HALO - back + tl, lanes]
        count = jnp.minimum(pos + 1, window).astype(F32)
        pooled = acc / count - u
        m = jnp.dot(pooled.astype(BF16), w_pool_ref[gi], preferred_element_type=F32) * pool_scale_ref[:, lanes]
        sq = sq + jnp.sum(m * m, axis=-1, keepdims=True)
        mixed.append(m)
    d_pool = LANES * len(POOL_WINDOWS)
    inv = lax.rsqrt(sq / d_pool + EPS)
    for gi, m in enumerate(mixed):
        lanes = slice(gi * LANES, (gi + 1) * LANES)
        o_pool_ref[:, lanes] = (m * inv * g_pool_ref[:, lanes]).astype(BF16)

    def glu(x):
        return x[:, :d_conv] * jax.nn.sigmoid(x[:, d_conv:])

    conv_ext[:CONV_HALO, :] = jnp.where(has_history, glu(conv_halo_ref[...]), 0.0)
    conv_ext[CONV_HALO:, :] = glu(conv_ref[...])
    first_tap = CONV_HALO - (CONV_WIDTH - 1)

    for base in range(0, tl, CONV_CHUNK):
        acc = jnp.broadcast_to(b_dw_ref[...], (CONV_CHUNK, d_conv))
        for tap in range(CONV_WIDTH):
            lo = base + first_tap + tap
            acc = acc + w_dw_ref[tap:tap + 1, :] * conv_ext[lo:lo + CONV_CHUNK, :]
        y_ref[base:base + CONV_CHUNK, :] = acc
    y = y_ref[...]
    mu = jnp.mean(y, axis=-1, keepdims=True)
    yc = y - mu
    var = jnp.mean(yc * yc, axis=-1, keepdims=True)
    y = yc * lax.rsqrt(var + EPS) * ln_g_ref[...] + ln_b_ref[...]
    y = y * jax.nn.sigmoid(y)
    o = jnp.dot(y.astype(BF16), w_pw_ref[...], preferred_element_type=F32)
    o_conv_ref[...] = _rms(o, g_conv_ref[...]).astype(BF16)


def _mixers(layer, u_pool, u_conv, w_pool, pool_scale, g_pool, w_dw, b_dw, ln_g, ln_b, w_pw, g_conv, batch,
            rows_per_batch):
    rows, d_pool = u_pool.shape
    d_conv = u_conv.shape[1] // 2
    tl = MIX_ROWS
    nt = rows_per_batch // tl

    def cur(n):
        return pl.BlockSpec((tl, n), lambda b, i: (b * nt + i, 0))

    def halo(n, h):
        return pl.BlockSpec((h, n), lambda b, i: (jnp.maximum((b * nt + i) * (tl // h) - 1, 0), 0))

    lr = functools.partial(_layer_resident, layer)
    return pl.pallas_call(
        functools.partial(_mixers_kernel, d_conv=d_conv),
        grid=(batch, nt),
        in_specs=[cur(d_pool), halo(d_pool, POOL_HALO), cur(2 * d_conv), halo(2 * d_conv, CONV_HALO),
                  lr(w_pool.shape[1:]), lr((1, d_pool)), lr((1, d_pool)),
                  lr(w_dw.shape[1:]), lr((1, d_conv)), lr((1, d_conv)), lr((1, d_conv)),
                  lr(w_pw.shape[1:]), lr((1, d_conv))],
        out_specs=[cur(d_pool), cur(d_conv)],
        out_shape=[jax.ShapeDtypeStruct((rows, d_pool), BF16), jax.ShapeDtypeStruct((rows, d_conv), BF16)],
        scratch_shapes=[pltpu.VMEM((POOL_HALO + tl, d_pool), F32), pltpu.VMEM((CONV_HALO + tl, d_conv), F32),
                        pltpu.VMEM((tl, d_conv), F32)],
        compiler_params=_params(2),
        name="pool_conv_mixers",
    )(u_pool, u_pool, u_conv, u_conv, w_pool, pool_scale, g_pool, w_dw, b_dw, ln_g, ln_b, w_pw, g_conv)


def _out_proj_kernel(attn_ref, pool_ref, conv_ref, h_ref, w_ref, g_ref, o_ref):
    d_attn, d_pool = attn_ref.shape[1], pool_ref.shape[1]
    y = jnp.dot(attn_ref[...], w_ref[:d_attn, :], preferred_element_type=F32)
    y = y + jnp.dot(pool_ref[...], w_ref[d_attn:d_attn + d_pool, :], preferred_element_type=F32)
    y = y + jnp.dot(conv_ref[...], w_ref[d_attn + d_pool:, :], preferred_element_type=F32)
    o_ref[...] = h_ref[...] + _rms(y, g_ref[...])


def _out_proj(layer, attn, pool, conv, h, w, g):
    rows, d = h.shape
    tm = PROJ_ROWS
    row_spec = lambda n: pl.BlockSpec((tm, n), lambda i: (i, 0))
    return pl.pallas_call(
        _out_proj_kernel,
        grid=(rows // tm,),
        in_specs=[row_spec(attn.shape[1]), row_spec(pool.shape[1]), row_spec(conv.shape[1]), row_spec(d),
                  _layer_resident(layer, w.shape[1:]), _layer_resident(layer, (1, d))],
        out_specs=row_spec(d),
        out_shape=jax.ShapeDtypeStruct((rows, d), F32),
        compiler_params=_params(1),
        name="out_proj",
    )(attn, pool, conv, h, w, g)


def _ffn_kernel(h_ref, g_pre_ref, w_gate_ref, w_up_ref, w_down_ref, g_post_ref, o_ref, u_ref, acc_ref):
    c = pl.program_id(1)

    @pl.when(c == 0)
    def _():
        u_ref[...] = _rms(h_ref[...], g_pre_ref[...]).astype(BF16)
        acc_ref[...] = jnp.zeros_like(acc_ref)

    u = u_ref[...]
    gate = jnp.dot(u, w_gate_ref[...], preferred_element_type=F32)
    up = jnp.dot(u, w_up_ref[...], preferred_element_type=F32)
    a = (gate * jax.nn.sigmoid(gate) * up).astype(BF16)
    tf = w_down_ref.shape[0]
    for n in range(0, acc_ref.shape[1], tf):
        acc_ref[:, n:n + tf] += jnp.dot(a, w_down_ref[:, n:n + tf], preferred_element_type=F32)

    @pl.when(c == pl.num_programs(1) - 1)
    def _():
        o_ref[...] = h_ref[...] + _rms(acc_ref[...], g_post_ref[...])


def _ffn(layer, h, g_pre, w_gate, w_up, w_down, g_post):
    rows, d = h.shape
    d_ff = w_gate.shape[2]
    tm, tf = FFN_ROWS, FFN_COLS
    return pl.pallas_call(
        _ffn_kernel,
        grid=(rows // tm, d_ff // tf),
        in_specs=[pl.BlockSpec((tm, d), lambda i, c: (i, 0)), _layer_resident(layer, (1, d)),
                  pl.BlockSpec((None, d, tf), lambda i, c: (layer, 0, c)),
                  pl.BlockSpec((None, d, tf), lambda i, c: (layer, 0, c)),
                  pl.BlockSpec((None, tf, d), lambda i, c: (layer, c, 0)), _layer_resident(layer, (1, d))],
        out_specs=pl.BlockSpec((tm, d), lambda i, c: (i, 0)),
        out_shape=jax.ShapeDtypeStruct((rows, d), F32),
        scratch_shapes=[pltpu.VMEM((tm, d), BF16), pltpu.VMEM((tm, d), F32)],
        compiler_params=_params(2),
        name="swiglu_ffn",
    )(h, g_pre, w_gate, w_up, w_down, g_post)


def kernel(x, meta_tokens, pre_mix_g, w_in, w_pool, pool_scale, w_dw, b_dw, conv_ln_g, conv_ln_b, w_pw, mix_out_g,
           w_out, post_mix_g, pre_ffn_g, w_gate, w_up, w_down, post_ffn_g):
    batch, seq, d = x.shape
    depth = w_in.shape[0]
    d_pool = pool_scale.shape[1]
    d_conv = w_pw.shape[1]
    d_attn = w_out.shape[1] - d_pool - d_conv
    assert d_pool == LANES * len(POOL_WINDOWS) and d_attn % (2 * LANES) == 0
    assert d_attn // N_HEADS_SB == LANES // 2

    length = N_META + seq
    rows_per_batch = -(-length // ROW_ALIGN) * ROW_ALIGN
    assert rows_per_batch % SEQ_TILE == 0 and (batch * rows_per_batch) % FFN_ROWS == 0
    meta = jnp.broadcast_to(meta_tokens[None].astype(x.dtype), (batch, N_META, d))
    tail = jnp.zeros((batch, rows_per_batch - length, d), x.dtype)
    h = jnp.concatenate([meta, x, tail], axis=1).reshape(batch * rows_per_batch, d)

    w_in, w_pool, w_pw, w_out, w_gate, w_up, w_down = (
        w.astype(BF16) for w in (w_in, w_pool, w_pw, w_out, w_gate, w_up, w_down))
    vec = lambda a: a[:, None, :]
    pre_mix_g, pool_scale, b_dw, conv_ln_g, conv_ln_b, post_mix_g, pre_ffn_g, post_ffn_g = (
        vec(a) for a in (pre_mix_g, pool_scale, b_dw, conv_ln_g, conv_ln_b, post_mix_g, pre_ffn_g, post_ffn_g))
    g_attn, g_pool, g_conv = (vec(a) for a in jnp.split(mix_out_g, [d_attn, d_attn + d_pool], axis=1))

    for i in range(depth):
        q, k, v, u_pool, u_conv = _in_proj(i, h, pre_mix_g, w_in, d_attn, d_pool, 2 * d_conv)
        o_attn = _attention(i, q, k, v, g_attn, batch, rows_per_batch)
        o_pool, o_conv = _mixers(i, u_pool, u_conv, w_pool, pool_scale, g_pool, w_dw, b_dw, conv_ln_g, conv_ln_b,
                                 w_pw, g_conv, batch, rows_per_batch)
        h = _out_proj(i, o_attn, o_pool, o_conv, h, w_out, post_mix_g)
        h = _ffn(i, h, pre_ffn_g, w_gate, w_up, w_down, post_ffn_g)

    return h.reshape(batch, rows_per_batch, d)[:, N_META:length]
```

```python
import functools

import jax
import jax.numpy as jnp
from jax import lax
from jax.experimental import pallas as pl
from jax.experimental.pallas import tpu as pltpu

N_META = 16
N_HEADS_SB = 16
POOL_WINDOWS = (2, 4, 8, 16)
CONV_WIDTH = 31
EPS = 1e-6

LANES = 128
SUBLANES = 8
VMEM_BYTES_V7X = 64 * 1024 * 1024
VMEM_LIMIT = VMEM_BYTES_V7X - 8 * 1024 * 1024

SEQ_TILE = 128
ROW_ALIGN = 384
PROJ_ROWS = 384
MIX_ROWS = 384
FFN_ROWS = 768
FFN_COLS = 512
CONV_HALO = 32
POOL_HALO = 16
CONV_CHUNK = 32

LOG2_E = 1.4426950408889634
LOG2_WEIGHT_FLOOR = -127.0

F32 = jnp.float32
BF16 = jnp.bfloat16


def _rms(x, g):
    return x * lax.rsqrt(jnp.mean(x * x, axis=-1, keepdims=True) + EPS) * g


def _resident(shape):
    return pl.BlockSpec(shape, lambda *_: (0,) * len(shape), pipeline_mode=pl.Buffered(1))


def _layer_resident(layer, shape):
    return pl.BlockSpec((None,) + tuple(shape), lambda *_: (layer,) + (0,) * len(shape),
                        pipeline_mode=pl.Buffered(1))


def _params(n_grid_axes):
    return pltpu.CompilerParams(dimension_semantics=("arbitrary",) * n_grid_axes, vmem_limit_bytes=VMEM_LIMIT)


def _in_proj_kernel(h_ref, g_ref, w_ref, q_ref, k_ref, v_ref, pool_ref, conv_ref, *, d_attn, d_pool, q_scale):
    u = _rms(h_ref[...], g_ref[...]).astype(BF16)
    o = 0
    q_ref[...] = (jnp.dot(u, w_ref[:, o:o + d_attn], preferred_element_type=F32) * q_scale).astype(BF16)
    o += d_attn
    k_ref[...] = jnp.dot(u, w_ref[:, o:o + d_attn], preferred_element_type=F32).astype(BF16)
    o += d_attn
    v_ref[...] = jnp.dot(u, w_ref[:, o:o + d_attn], preferred_element_type=F32).astype(BF16)
    o += d_attn
    pool_ref[...] = jnp.dot(u, w_ref[:, o:o + d_pool], preferred_element_type=F32)
    o += d_pool
    conv_ref[...] = jnp.dot(u, w_ref[:, o:], preferred_element_type=F32)


def _in_proj(layer, h, g, w, d_attn, d_pool, d_conv2):
    rows, d = h.shape
    tm = PROJ_ROWS
    row_spec = lambda n: pl.BlockSpec((tm, n), lambda i: (i, 0))
    head_dim = d_attn // N_HEADS_SB
    return pl.pallas_call(
        functools.partial(_in_proj_kernel, d_attn=d_attn, d_pool=d_pool, q_scale=LOG2_E * head_dim ** -0.5),
        grid=(rows // tm,),
        in_specs=[row_spec(d), _layer_resident(layer, (1, d)), _layer_resident(layer, w.shape[1:])],
        out_specs=[row_spec(d_attn), row_spec(d_attn), row_spec(d_attn), row_spec(d_pool), row_spec(d_conv2)],
        out_shape=[jax.ShapeDtypeStruct((rows, d_attn), BF16)] * 3
        + [jax.ShapeDtypeStruct((rows, d_pool), F32), jax.ShapeDtypeStruct((rows, d_conv2), F32)],
        compiler_params=_params(1),
        name="in_proj",
    )(h, g, w)


def _attn_kernel(q_ref, k_ref, v_ref, tri_ref, g_ref, o_ref, q2_ref, acc_ref, c_ref, *, n_pairs):
    t = SEQ_TILE
    qi = pl.program_id(1)
    pairs = range(n_pairs)
    lanes = [slice(p * LANES, (p + 1) * LANES) for p in pairs]
    first_head = lax.broadcasted_iota(jnp.int32, (t, LANES), 1) < LANES // 2
    row = lax.broadcasted_iota(jnp.int32, (2 * t, t), 0)
    col = lax.broadcasted_iota(jnp.int32, (2 * t, t), 1)
    causal = col < (row & (t - 1))
    zero = jnp.zeros((), BF16)
    for p in pairs:
        qp = q_ref[:, lanes[p]]
        q2_ref[p] = jnp.concatenate([jnp.where(first_head, qp, zero), jnp.where(first_head, zero, qp)], axis=0)

    def key_tile(kb, diagonal):
        start = pl.multiple_of(kb * t, t)
        z = [lax.dot_general(q2_ref[p], k_ref[pl.ds(start, t), lanes[p]], (((1,), (1,)), ((), ())),
                             preferred_element_type=F32) for p in pairs]
        v = [v_ref[pl.ds(start, t), lanes[p]] for p in pairs]
        c_old = None if diagonal else [c_ref[p] for p in pairs]
        log_beta, later, c_new = [], [], []
        for p in pairs:
            z_pos = jnp.maximum(z[p], 0.0)
            z_neg = z[p] - z_pos
            log_term = jnp.log(1.0 + jnp.exp2(z_neg - z_pos)) * LOG2_E
            soft = z_pos + log_term
            if diagonal:
                soft = jnp.where(causal, soft, 0.0)
            hi = soft.astype(BF16)
            lo = (soft - hi.astype(F32)).astype(BF16)
            later.append(jnp.dot(jnp.concatenate([hi, lo], axis=1), tri_ref[...], preferred_element_type=F32))
            log_beta.append(z_neg - log_term)
            c = jnp.sum(soft, axis=1, keepdims=True)
            c_new.append(c if diagonal else c_old[p] + c)
        o = []
        for p in pairs:
            x = log_beta[p] - later[p]
            if not diagonal:
                x = x - c_old[p]
            w = jnp.exp2(x)
            if diagonal:
                w = jnp.where(causal, w, 0.0)
            o.append(jnp.dot(w.astype(BF16), v[p], preferred_element_type=F32))
        c_min = None
        for p in pairs:
            o_pair = jnp.where(first_head, o[p][:t], o[p][t:])
            if diagonal:
                acc_ref[:, lanes[p]] = o_pair
            else:
                acc_ref[:, lanes[p]] += o_pair
            c_ref[p] = c_new[p]
            c_min = c_new[p] if c_min is None else jnp.minimum(c_min, c_new[p])
        return jnp.min(c_min)

    def cond(s):
        kb, c_min = s
        return jnp.logical_and(kb >= 0, c_min < -LOG2_WEIGHT_FLOOR)

    def body(s):
        kb, _ = s
        return kb - 1, key_tile(kb, False)

    lax.while_loop(cond, body, (qi - 1, key_tile(qi, True)))
    o_ref[...] = _rms(acc_ref[...], g_ref[...]).astype(BF16)


def _attention(layer, q, k, v, g, batch, rows_per_batch):
    rows, d_attn = q.shape
    t = SEQ_TILE
    nq = rows_per_batch // t
    j = jnp.arange(t)
    tri = (j[:, None] > j[None, :]).astype(BF16)
    tri = jnp.concatenate([tri, tri], axis=0)
    n_pairs = d_attn // LANES
    kv_spec = pl.BlockSpec((rows_per_batch, d_attn), lambda b, i: (b, 0), pipeline_mode=pl.Buffered(1))
    tile_spec = pl.BlockSpec((t, d_attn), lambda b, i: (b * nq + i, 0))
    return pl.pallas_call(
        functools.partial(_attn_kernel, n_pairs=n_pairs),
        grid=(batch, nq),
        in_specs=[tile_spec, kv_spec, kv_spec, _resident(tri.shape), _layer_resident(layer, (1, d_attn))],
        out_specs=tile_spec,
        out_shape=jax.ShapeDtypeStruct((rows, d_attn), BF16),
        scratch_shapes=[pltpu.VMEM((n_pairs, 2 * t, LANES), BF16), pltpu.VMEM((t, d_attn), F32),
                        pltpu.VMEM((n_pairs, 2 * t, 1), F32)],
        compiler_params=_params(2),
        name="sb_attention",
    )(q, k, v, tri, g)


def _mixers_kernel(pool_ref, pool_halo_ref, conv_ref, conv_halo_ref, w_pool_ref, pool_scale_ref, g_pool_ref,
                   w_dw_ref, b_dw_ref, ln_g_ref, ln_b_ref, w_pw_ref, g_conv_ref,
                   o_pool_ref, o_conv_ref, pool_ext, conv_ext, y_ref, *, d_conv):
    tl = pool_ref.shape[0]
    ti = pl.program_id(1)
    has_history = ti > 0

    pool_ext[:POOL_HALO, :] = jnp.where(has_history, pool_halo_ref[...], 0.0)
    pool_ext[POOL_HALO:, :] = pool_ref[...]
    pos = ti * tl + lax.broadcasted_iota(jnp.int32, (tl, 1), 0)
    mixed = []
    sq = jnp.zeros((tl, 1), F32)
    for gi, window in enumerate(POOL_WINDOWS):
        lanes = slice(gi * LANES, (gi + 1) * LANES)
        u = pool_ext[POOL_HALO:, lanes]
        acc = u
        for back in range(1, window):
            acc = acc + pool_ext[POOL_HALO - back:POOL_HALO - back + tl, lanes]
        count = jnp.minimum(pos + 1, window).astype(F32)
        pooled = acc / count - u
        m = jnp.dot(pooled.astype(BF16), w_pool_ref[gi], preferred_element_type=F32) * pool_scale_ref[:, lanes]
        sq = sq + jnp.sum(m * m, axis=-1, keepdims=True)
        mixed.append(m)
    d_pool = LANES * len(POOL_WINDOWS)
    inv = lax.rsqrt(sq / d_pool + EPS)
    for gi, m in enumerate(mixed):
        lanes = slice(gi * LANES, (gi + 1) * LANES)
        o_pool_ref[:, lanes] = (m * inv * g_pool_ref[:, lanes]).astype(BF16)

    def glu(x):
        return x[:, :d_conv] * jax.nn.sigmoid(x[:, d_conv:])

    n_ext = CONV_HALO + tl
    conv_ext[0, :CONV_HALO, :] = jnp.where(has_history, glu(conv_halo_ref[...]), 0.0)
    conv_ext[0, CONV_HALO:, :] = glu(conv_ref[...])
    for r in range(1, SUBLANES):
        conv_ext[r, :n_ext - SUBLANES, :] = conv_ext[0, r:r + n_ext - SUBLANES, :]
    first_tap = CONV_HALO - (CONV_WIDTH - 1)

    for base in range(0, tl, CONV_CHUNK):
        acc = jnp.broadcast_to(b_dw_ref[...], (CONV_CHUNK, d_conv))
        for tap in range(CONV_WIDTH):
            shift = (first_tap + tap) % SUBLANES
            lo = base + first_tap + tap - shift
            acc = acc + w_dw_ref[tap:tap + 1, :] * conv_ext[shift, lo:lo + CONV_CHUNK, :]
        y_ref[base:base + CONV_CHUNK, :] = acc
    y = y_ref[...]
    mu = jnp.mean(y, axis=-1, keepdims=True)
    yc = y - mu
    var = jnp.mean(yc * yc, axis=-1, keepdims=True)
    y = yc * lax.rsqrt(var + EPS) * ln_g_ref[...] + ln_b_ref[...]
    y = y * jax.nn.sigmoid(y)
    o = jnp.dot(y.astype(BF16), w_pw_ref[...], preferred_element_type=F32)
    o_conv_ref[...] = _rms(o, g_conv_ref[...]).astype(BF16)


def _mixers(layer, u_pool, u_conv, w_pool, pool_scale, g_pool, w_dw, b_dw, ln_g, ln_b, w_pw, g_conv, batch,
            rows_per_batch):
    rows, d_pool = u_pool.shape
    d_conv = u_conv.shape[1] // 2
    tl = MIX_ROWS
    nt = rows_per_batch // tl

    def cur(n):
        return pl.BlockSpec((tl, n), lambda b, i: (b * nt + i, 0))

    def halo(n, h):
        return pl.BlockSpec((h, n), lambda b, i: (jnp.maximum((b * nt + i) * (tl // h) - 1, 0), 0))

    lr = functools.partial(_layer_resident, layer)
    return pl.pallas_call(
        functools.partial(_mixers_kernel, d_conv=d_conv),
        grid=(batch, nt),
        in_specs=[cur(d_pool), halo(d_pool, POOL_HALO), cur(2 * d_conv), halo(2 * d_conv, CONV_HALO),
                  lr(w_pool.shape[1:]), lr((1, d_pool)), lr((1, d_pool)),
                  lr(w_dw.shape[1:]), lr((1, d_conv)), lr((1, d_conv)), lr((1, d_conv)),
                  lr(w_pw.shape[1:]), lr((1, d_conv))],
        out_specs=[cur(d_pool), cur(d_conv)],
        out_shape=[jax.ShapeDtypeStruct((rows, d_pool), BF16), jax.ShapeDtypeStruct((rows, d_conv), BF16)],
        scratch_shapes=[pltpu.VMEM((POOL_HALO + tl, d_pool), F32), pltpu.VMEM((SUBLANES, CONV_HALO + tl, d_conv), F32),
                        pltpu.VMEM((tl, d_conv), F32)],
        compiler_params=_params(2),
        name="pool_conv_mixers",
    )(u_pool, u_pool, u_conv, u_conv, w_pool, pool_scale, g_pool, w_dw, b_dw, ln_g, ln_b, w_pw, g_conv)


def _out_proj_kernel(attn_ref, pool_ref, conv_ref, h_ref, w_ref, g_ref, o_ref):
    d_attn, d_pool = attn_ref.shape[1], pool_ref.shape[1]
    y = jnp.dot(attn_ref[...], w_ref[:d_attn, :], preferred_element_type=F32)
    y = y + jnp.dot(pool_ref[...], w_ref[d_attn:d_attn + d_pool, :], preferred_element_type=F32)
    y = y + jnp.dot(conv_ref[...], w_ref[d_attn + d_pool:, :], preferred_element_type=F32)
    o_ref[...] = h_ref[...] + _rms(y, g_ref[...])


def _out_proj(layer, attn, pool, conv, h, w, g):
    rows, d = h.shape
    tm = PROJ_ROWS
    row_spec = lambda n: pl.BlockSpec((tm, n), lambda i: (i, 0))
    return pl.pallas_call(
        _out_proj_kernel,
        grid=(rows // tm,),
        in_specs=[row_spec(attn.shape[1]), row_spec(pool.shape[1]), row_spec(conv.shape[1]), row_spec(d),
                  _layer_resident(layer, w.shape[1:]), _layer_resident(layer, (1, d))],
        out_specs=row_spec(d),
        out_shape=jax.ShapeDtypeStruct((rows, d), F32),
        compiler_params=_params(1),
        name="out_proj",
    )(attn, pool, conv, h, w, g)


def _ffn_kernel(h_ref, g_pre_ref, w_gate_ref, w_up_ref, w_down_ref, g_post_ref, o_ref, u_ref, acc_ref):
    c = pl.program_id(1)

    @pl.when(c == 0)
    def _():
        u_ref[...] = _rms(h_ref[...], g_pre_ref[...]).astype(BF16)
        acc_ref[...] = jnp.zeros_like(acc_ref)

    u = u_ref[...]
    gate = jnp.dot(u, w_gate_ref[...], preferred_element_type=F32)
    up = jnp.dot(u, w_up_ref[...], preferred_element_type=F32)
    a = (gate * jax.nn.sigmoid(gate) * up).astype(BF16)
    tf = w_down_ref.shape[0]
    for n in range(0, acc_ref.shape[1], tf):
        acc_ref[:, n:n + tf] += jnp.dot(a, w_down_ref[:, n:n + tf], preferred_element_type=F32)

    @pl.when(c == pl.num_programs(1) - 1)
    def _():
        o_ref[...] = h_ref[...] + _rms(acc_ref[...], g_post_ref[...])


def _ffn(layer, h, g_pre, w_gate, w_up, w_down, g_post):
    rows, d = h.shape
    d_ff = w_gate.shape[2]
    tm, tf = FFN_ROWS, FFN_COLS
    return pl.pallas_call(
        _ffn_kernel,
        grid=(rows // tm, d_ff // tf),
        in_specs=[pl.BlockSpec((tm, d), lambda i, c: (i, 0)), _layer_resident(layer, (1, d)),
                  pl.BlockSpec((None, d, tf), lambda i, c: (layer, 0, c)),
                  pl.BlockSpec((None, d, tf), lambda i, c: (layer, 0, c)),
                  pl.BlockSpec((None, tf, d), lambda i, c: (layer, c, 0)), _layer_resident(layer, (1, d))],
        out_specs=pl.BlockSpec((tm, d), lambda i, c: (i, 0)),
        out_shape=jax.ShapeDtypeStruct((rows, d), F32),
        scratch_shapes=[pltpu.VMEM((tm, d), BF16), pltpu.VMEM((tm, d), F32)],
        compiler_params=_params(2),
        name="swiglu_ffn",
    )(h, g_pre, w_gate, w_up, w_down, g_post)


def kernel(x, meta_tokens, pre_mix_g, w_in, w_pool, pool_scale, w_dw, b_dw, conv_ln_g, conv_ln_b, w_pw, mix_out_g,
           w_out, post_mix_g, pre_ffn_g, w_gate, w_up, w_down, post_ffn_g):
    batch, seq, d = x.shape
    depth = w_in.shape[0]
    d_pool = pool_scale.shape[1]
    d_conv = w_pw.shape[1]
    d_attn = w_out.shape[1] - d_pool - d_conv
    assert d_pool == LANES * len(POOL_WINDOWS) and d_attn % (2 * LANES) == 0
    assert d_attn // N_HEADS_SB == LANES // 2

    length = N_META + seq
    rows_per_batch = -(-length // ROW_ALIGN) * ROW_ALIGN
    assert rows_per_batch % SEQ_TILE == 0 and (batch * rows_per_batch) % FFN_ROWS == 0
    meta = jnp.broadcast_to(meta_tokens[None].astype(x.dtype), (batch, N_META, d))
    tail = jnp.zeros((batch, rows_per_batch - length, d), x.dtype)
    h = jnp.concatenate([meta, x, tail], axis=1).reshape(batch * rows_per_batch, d)

    w_in, w_pool, w_pw, w_out, w_gate, w_up, w_down = (
        w.astype(BF16) for w in (w_in, w_pool, w_pw, w_out, w_gate, w_up, w_down))
    vec = lambda a: a[:, None, :]
    pre_mix_g, pool_scale, b_dw, conv_ln_g, conv_ln_b, post_mix_g, pre_ffn_g, post_ffn_g = (
        vec(a) for a in (pre_mix_g, pool_scale, b_dw, conv_ln_g, conv_ln_b, post_mix_g, pre_ffn_g, post_ffn_g))
    g_attn, g_pool, g_conv = (vec(a) for a in jnp.split(mix_out_g, [d_attn, d_attn + d_pool], axis=1))

    for i in range(depth):
        q, k, v, u_pool, u_conv = _in_proj(i, h, pre_mix_g, w_in, d_attn, d_pool, 2 * d_conv)
        o_attn = _attention(i, q, k, v, g_attn, batch, rows_per_batch)
        o_pool, o_conv = _mixers(i, u_pool, u_conv, w_pool, pool_scale, g_pool, w_dw, b_dw, conv_ln_g, conv_ln_b,
                                 w_pw, g_conv, batch, rows_per_batch)
        h = _out_proj(i, o_attn, o_pool, o_conv, h, w_out, post_mix_g)
        h = _ffn(i, h, pre_ffn_g, w_gate, w_up, w_down, post_ffn_g)

    return h.reshape(batch, rows_per_batch, d)[:, N_META:length]
```

```python
import functools

import jax
import jax.numpy as jnp
from jax import lax
from jax.experimental import pallas as pl
from jax.experimental.pallas import tpu as pltpu

N_META = 16
N_HEADS_SB = 16
POOL_WINDOWS = (2, 4, 8, 16)
CONV_WIDTH = 31
EPS = 1e-6

LANES = 128
SUBLANES = 8
VMEM_BYTES_V7X = 64 * 1024 * 1024
VMEM_LIMIT = VMEM_BYTES_V7X - 8 * 1024 * 1024

SEQ_TILE = 128
TOP_ROWS = 32
ROW_ALIGN = 384
PROJ_ROWS = 384
MIX_ROWS = 384
FFN_ROWS = 768
FFN_COLS = 512
CONV_HALO = 32
POOL_HALO = 16
CONV_CHUNK = 32

LOG2_E = 1.4426950408889634
LOG2_WEIGHT_FLOOR = -127.0

F32 = jnp.float32
BF16 = jnp.bfloat16


def _rms(x, g):
    return x * lax.rsqrt(jnp.mean(x * x, axis=-1, keepdims=True) + EPS) * g


def _resident(shape):
    return pl.BlockSpec(shape, lambda *_: (0,) * len(shape), pipeline_mode=pl.Buffered(1))


def _layer_resident(layer, shape):
    return pl.BlockSpec((None,) + tuple(shape), lambda *_: (layer,) + (0,) * len(shape),
                        pipeline_mode=pl.Buffered(1))


def _params(n_grid_axes):
    return pltpu.CompilerParams(dimension_semantics=("arbitrary",) * n_grid_axes, vmem_limit_bytes=VMEM_LIMIT)


def _in_proj_kernel(h_ref, g_ref, w_ref, q_ref, k_ref, v_ref, pool_ref, conv_ref, *, d_attn, d_pool, q_scale):
    u = _rms(h_ref[...], g_ref[...]).astype(BF16)
    o = 0
    q_ref[...] = (jnp.dot(u, w_ref[:, o:o + d_attn], preferred_element_type=F32) * q_scale).astype(BF16)
    o += d_attn
    k_ref[...] = jnp.dot(u, w_ref[:, o:o + d_attn], preferred_element_type=F32).astype(BF16)
    o += d_attn
    v_ref[...] = jnp.dot(u, w_ref[:, o:o + d_attn], preferred_element_type=F32).astype(BF16)
    o += d_attn
    pool_ref[...] = jnp.dot(u, w_ref[:, o:o + d_pool], preferred_element_type=F32)
    o += d_pool
    conv_ref[...] = jnp.dot(u, w_ref[:, o:], preferred_element_type=F32)


def _in_proj(layer, h, g, w, d_attn, d_pool, d_conv2):
    rows, d = h.shape
    tm = PROJ_ROWS
    row_spec = lambda n: pl.BlockSpec((tm, n), lambda i: (i, 0))
    head_dim = d_attn // N_HEADS_SB
    return pl.pallas_call(
        functools.partial(_in_proj_kernel, d_attn=d_attn, d_pool=d_pool, q_scale=LOG2_E * head_dim ** -0.5),
        grid=(rows // tm,),
        in_specs=[row_spec(d), _layer_resident(layer, (1, d)), _layer_resident(layer, w.shape[1:])],
        out_specs=[row_spec(d_attn), row_spec(d_attn), row_spec(d_attn), row_spec(d_pool), row_spec(d_conv2)],
        out_shape=[jax.ShapeDtypeStruct((rows, d_attn), BF16)] * 3
        + [jax.ShapeDtypeStruct((rows, d_pool), F32), jax.ShapeDtypeStruct((rows, d_conv2), F32)],
        compiler_params=_params(1),
        name="in_proj",
    )(h, g, w)


def _attn_kernel(q_ref, k_ref, v_ref, tri_ref, g_ref, o_ref, q2_ref, acc_ref, c_ref, *, n_pairs):
    t = SEQ_TILE
    qi = pl.program_id(1)
    pairs = range(n_pairs)
    lanes = [slice(p * LANES, (p + 1) * LANES) for p in pairs]
    first_head = lax.broadcasted_iota(jnp.int32, (t, LANES), 1) < LANES // 2
    row = lax.broadcasted_iota(jnp.int32, (2 * t, t), 0)
    col = lax.broadcasted_iota(jnp.int32, (2 * t, t), 1)
    causal = col < (row & (t - 1))
    zero = jnp.zeros((), BF16)
    for p in pairs:
        qp = q_ref[:, lanes[p]]
        q2_ref[p] = jnp.concatenate([jnp.where(first_head, qp, zero), jnp.where(first_head, zero, qp)], axis=0)

    def head_rows(ref, p, r):
        return ref[p] if r == t else jnp.concatenate([ref[p, :r], ref[p, t:t + r]], axis=0)

    def key_tile(kb, r, diagonal=False):
        start = pl.multiple_of(kb * t, t)
        z = [lax.dot_general(head_rows(q2_ref, p, r), k_ref[pl.ds(start, t), lanes[p]], (((1,), (1,)), ((), ())),
                             preferred_element_type=F32) for p in pairs]
        v = [v_ref[pl.ds(start, t), lanes[p]] for p in pairs]
        c_old = None if diagonal else [head_rows(c_ref, p, r) for p in pairs]
        log_beta, later, c_new = [], [], []
        for p in pairs:
            z_pos = jnp.maximum(z[p], 0.0)
            z_neg = z[p] - z_pos
            log_term = jnp.log(1.0 + jnp.exp2(z_neg - z_pos)) * LOG2_E
            soft = z_pos + log_term
            if diagonal:
                soft = jnp.where(causal, soft, 0.0)
            hi = soft.astype(BF16)
            lo = (soft - hi.astype(F32)).astype(BF16)
            later.append(jnp.dot(jnp.concatenate([hi, lo], axis=1), tri_ref[...], preferred_element_type=F32))
            log_beta.append(z_neg - log_term)
            c = jnp.sum(soft, axis=1, keepdims=True)
            c_new.append(jnp.broadcast_to(c, soft.shape) if diagonal else c_old[p] + c)
        o = []
        for p in pairs:
            x = log_beta[p] - later[p]
            if not diagonal:
                x = x - c_old[p]
            w = jnp.exp2(x)
            if diagonal:
                w = jnp.where(causal, w, 0.0)
            o.append(jnp.dot(w.astype(BF16), v[p], preferred_element_type=F32))
        c_min = None
        for p in pairs:
            first = lax.broadcasted_iota(jnp.int32, (r, LANES), 1) < LANES // 2
            o_pair = jnp.where(first, o[p][:r], o[p][r:])
            if diagonal:
                acc_ref[:, lanes[p]] = o_pair
            else:
                acc_ref[:r, lanes[p]] += o_pair
            if r == t:
                c_ref[p] = c_new[p]
            else:
                c_ref[p, :r] = c_new[p][:r]
                c_ref[p, t:t + r] = c_new[p][r:]
            c_min = c_new[p] if c_min is None else jnp.minimum(c_min, c_new[p])
        if r < t:
            return jnp.min(c_min)
        top = jnp.minimum(jnp.min(c_min[:TOP_ROWS]), jnp.min(c_min[t:t + TOP_ROWS]))
        rest = jnp.minimum(jnp.min(c_min[TOP_ROWS:t]), jnp.min(c_min[t + TOP_ROWS:]))
        return top, rest

    def unfinished(c_min):
        return c_min < -LOG2_WEIGHT_FLOOR

    def full_body(s):
        kb, _, _ = s
        return (kb - 1,) + key_tile(kb, t)

    def top_body(s):
        kb, _ = s
        return kb - 1, key_tile(kb, TOP_ROWS)

    kb, top, _ = lax.while_loop(lambda s: jnp.logical_and(s[0] >= 0, unfinished(s[2])), full_body,
                                (qi - 1,) + key_tile(qi, t, diagonal=True))
    lax.while_loop(lambda s: jnp.logical_and(s[0] >= 0, unfinished(s[1])), top_body, (kb, top))
    o_ref[...] = _rms(acc_ref[...], g_ref[...]).astype(BF16)


def _attention(layer, q, k, v, g, batch, rows_per_batch):
    rows, d_attn = q.shape
    t = SEQ_TILE
    nq = rows_per_batch // t
    j = jnp.arange(t)
    tri = (j[:, None] > j[None, :]).astype(BF16)
    tri = jnp.concatenate([tri, tri], axis=0)
    n_pairs = d_attn // LANES
    kv_spec = pl.BlockSpec((rows_per_batch, d_attn), lambda b, i: (b, 0), pipeline_mode=pl.Buffered(1))
    tile_spec = pl.BlockSpec((t, d_attn), lambda b, i: (b * nq + i, 0))
    return pl.pallas_call(
        functools.partial(_attn_kernel, n_pairs=n_pairs),
        grid=(batch, nq),
        in_specs=[tile_spec, kv_spec, kv_spec, _resident(tri.shape), _layer_resident(layer, (1, d_attn))],
        out_specs=tile_spec,
        out_shape=jax.ShapeDtypeStruct((rows, d_attn), BF16),
        scratch_shapes=[pltpu.VMEM((n_pairs, 2 * t, LANES), BF16), pltpu.VMEM((t, d_attn), F32),
                        pltpu.VMEM((n_pairs, 2 * t, t), F32)],
        compiler_params=_params(2),
        name="sb_attention",
    )(q, k, v, tri, g)


def _mixers_kernel(pool_ref, pool_halo_ref, conv_ref, conv_halo_ref, w_pool_ref, pool_scale_ref, g_pool_ref,
                   w_dw_ref, b_dw_ref, ln_g_ref, ln_b_ref, w_pw_ref, g_conv_ref,
                   o_pool_ref, o_conv_ref, pool_ext, conv_ext, y_ref, *, d_conv):
    tl = pool_ref.shape[0]
    ti = pl.program_id(1)
    has_history = ti > 0

    pool_ext[:POOL_HALO, :] = jnp.where(has_history, pool_halo_ref[...], 0.0)
    pool_ext[POOL_HALO:, :] = pool_ref[...]
    pos = ti * tl + lax.broadcasted_iota(jnp.int32, (tl, 1), 0)
    mixed = []
    sq = jnp.zeros((tl, 1), F32)
    for gi, window in enumerate(POOL_WINDOWS):
        lanes = slice(gi * LANES, (gi + 1) * LANES)
        u = pool_ext[POOL_HALO:, lanes]
        acc = u
        for back in range(1, window):
            acc = acc + pool_ext[POOL_HALO - back:POOL_HALO - back + tl, lanes]
        count = jnp.minimum(pos + 1, window).astype(F32)
        pooled = acc / count - u
        m = jnp.dot(pooled.astype(BF16), w_pool_ref[gi], preferred_element_type=F32) * pool_scale_ref[:, lanes]
        sq = sq + jnp.sum(m * m, axis=-1, keepdims=True)
        mixed.append(m)
    d_pool = LANES * len(POOL_WINDOWS)
    inv = lax.rsqrt(sq / d_pool + EPS)
    for gi, m in enumerate(mixed):
        lanes = slice(gi * LANES, (gi + 1) * LANES)
        o_pool_ref[:, lanes] = (m * inv * g_pool_ref[:, lanes]).astype(BF16)

    def glu(x):
        return x[:, :d_conv] * jax.nn.sigmoid(x[:, d_conv:])

    n_ext = CONV_HALO + tl
    conv_ext[0, :CONV_HALO, :] = jnp.where(has_history, glu(conv_halo_ref[...]), 0.0)
    conv_ext[0, CONV_HALO:, :] = glu(conv_ref[...])
    for r in range(1, SUBLANES):
        conv_ext[r, :n_ext - SUBLANES, :] = conv_ext[0, r:r + n_ext - SUBLANES, :]
    first_tap = CONV_HALO - (CONV_WIDTH - 1)

    for base in range(0, tl, CONV_CHUNK):
        acc = jnp.broadcast_to(b_dw_ref[...], (CONV_CHUNK, d_conv))
        for tap in range(CONV_WIDTH):
            shift = (first_tap + tap) % SUBLANES
            lo = base + first_tap + tap - shift
            acc = acc + w_dw_ref[tap:tap + 1, :] * conv_ext[shift, lo:lo + CONV_CHUNK, :]
        y_ref[base:base + CONV_CHUNK, :] = acc
    y = y_ref[...]
    mu = jnp.mean(y, axis=-1, keepdims=True)
    yc = y - mu
    var = jnp.mean(yc * yc, axis=-1, keepdims=True)
    y = yc * lax.rsqrt(var + EPS) * ln_g_ref[...] + ln_b_ref[...]
    y = y * jax.nn.sigmoid(y)
    o = jnp.dot(y.astype(BF16), w_pw_ref[...], preferred_element_type=F32)
    o_conv_ref[...] = _rms(o, g_conv_ref[...]).astype(BF16)


def _mixers(layer, u_pool, u_conv, w_pool, pool_scale, g_pool, w_dw, b_dw, ln_g, ln_b, w_pw, g_conv, batch,
            rows_per_batch):
    rows, d_pool = u_pool.shape
    d_conv = u_conv.shape[1] // 2
    tl = MIX_ROWS
    nt = rows_per_batch // tl

    def cur(n):
        return pl.BlockSpec((tl, n), lambda b, i: (b * nt + i, 0))

    def halo(n, h):
        return pl.BlockSpec((h, n), lambda b, i: (jnp.maximum((b * nt + i) * (tl // h) - 1, 0), 0))

    lr = functools.partial(_layer_resident, layer)
    return pl.pallas_call(
        functools.partial(_mixers_kernel, d_conv=d_conv),
        grid=(batch, nt),
        in_specs=[cur(d_pool), halo(d_pool, POOL_HALO), cur(2 * d_conv), halo(2 * d_conv, CONV_HALO),
                  lr(w_pool.shape[1:]), lr((1, d_pool)), lr((1, d_pool)),
                  lr(w_dw.shape[1:]), lr((1, d_conv)), lr((1, d_conv)), lr((1, d_conv)),
                  lr(w_pw.shape[1:]), lr((1, d_conv))],
        out_specs=[cur(d_pool), cur(d_conv)],
        out_shape=[jax.ShapeDtypeStruct((rows, d_pool), BF16), jax.ShapeDtypeStruct((rows, d_conv), BF16)],
        scratch_shapes=[pltpu.VMEM((POOL_HALO + tl, d_pool), F32), pltpu.VMEM((SUBLANES, CONV_HALO + tl, d_conv), F32),
                        pltpu.VMEM((tl, d_conv), F32)],
        compiler_params=_params(2),
        name="pool_conv_mixers",
    )(u_pool, u_pool, u_conv, u_conv, w_pool, pool_scale, g_pool, w_dw, b_dw, ln_g, ln_b, w_pw, g_conv)


def _out_proj_kernel(attn_ref, pool_ref, conv_ref, h_ref, w_ref, g_ref, o_ref):
    d_attn, d_pool = attn_ref.shape[1], pool_ref.shape[1]
    y = jnp.dot(attn_ref[...], w_ref[:d_attn, :], preferred_element_type=F32)
    y = y + jnp.dot(pool_ref[...], w_ref[d_attn:d_attn + d_pool, :], preferred_element_type=F32)
    y = y + jnp.dot(conv_ref[...], w_ref[d_attn + d_pool:, :], preferred_element_type=F32)
    o_ref[...] = h_ref[...] + _rms(y, g_ref[...])


def _out_proj(layer, attn, pool, conv, h, w, g):
    rows, d = h.shape
    tm = PROJ_ROWS
    row_spec = lambda n: pl.BlockSpec((tm, n), lambda i: (i, 0))
    return pl.pallas_call(
        _out_proj_kernel,
        grid=(rows // tm,),
        in_specs=[row_spec(attn.shape[1]), row_spec(pool.shape[1]), row_spec(conv.shape[1]), row_spec(d),
                  _layer_resident(layer, w.shape[1:]), _layer_resident(layer, (1, d))],
        out_specs=row_spec(d),
        out_shape=jax.ShapeDtypeStruct((rows, d), F32),
        compiler_params=_params(1),
        name="out_proj",
    )(attn, pool, conv, h, w, g)


def _ffn_kernel(h_ref, g_pre_ref, w_gate_ref, w_up_ref, w_down_ref, g_post_ref, o_ref, u_ref, acc_ref):
    c = pl.program_id(1)

    @pl.when(c == 0)
    def _():
        u_ref[...] = _rms(h_ref[...], g_pre_ref[...]).astype(BF16)
        acc_ref[...] = jnp.zeros_like(acc_ref)

    u = u_ref[...]
    gate = jnp.dot(u, w_gate_ref[...], preferred_element_type=F32)
    up = jnp.dot(u, w_up_ref[...], preferred_element_type=F32)
    a = (gate * jax.nn.sigmoid(gate) * up).astype(BF16)
    tf = w_down_ref.shape[0]
    for n in range(0, acc_ref.shape[1], tf):
        acc_ref[:, n:n + tf] += jnp.dot(a, w_down_ref[:, n:n + tf], preferred_element_type=F32)

    @pl.when(c == pl.num_programs(1) - 1)
    def _():
        o_ref[...] = h_ref[...] + _rms(acc_ref[...], g_post_ref[...])


def _ffn(layer, h, g_pre, w_gate, w_up, w_down, g_post):
    rows, d = h.shape
    d_ff = w_gate.shape[2]
    tm, tf = FFN_ROWS, FFN_COLS
    return pl.pallas_call(
        _ffn_kernel,
        grid=(rows // tm, d_ff // tf),
        in_specs=[pl.BlockSpec((tm, d), lambda i, c: (i, 0)), _layer_resident(layer, (1, d)),
                  pl.BlockSpec((None, d, tf), lambda i, c: (layer, 0, c)),
                  pl.BlockSpec((None, d, tf), lambda i, c: (layer, 0, c)),
                  pl.BlockSpec((None, tf, d), lambda i, c: (layer, c, 0)), _layer_resident(layer, (1, d))],
        out_specs=pl.BlockSpec((tm, d), lambda i, c: (i, 0)),
        out_shape=jax.ShapeDtypeStruct((rows, d), F32),
        scratch_shapes=[pltpu.VMEM((tm, d), BF16), pltpu.VMEM((tm, d), F32)],
        compiler_params=_params(2),
        name="swiglu_ffn",
    )(h, g_pre, w_gate, w_up, w_down, g_post)


def kernel(x, meta_tokens, pre_mix_g, w_in, w_pool, pool_scale, w_dw, b_dw, conv_ln_g, conv_ln_b, w_pw, mix_out_g,
           w_out, post_mix_g, pre_ffn_g, w_gate, w_up, w_down, post_ffn_g):
    batch, seq, d = x.shape
    depth = w_in.shape[0]
    d_pool = pool_scale.shape[1]
    d_conv = w_pw.shape[1]
    d_attn = w_out.shape[1] - d_pool - d_conv
    assert d_pool == LANES * len(POOL_WINDOWS) and d_attn % (2 * LANES) == 0
    assert d_attn // N_HEADS_SB == LANES // 2

    length = N_META + seq
    rows_per_batch = -(-length // ROW_ALIGN) * ROW_ALIGN
    assert rows_per_batch % SEQ_TILE == 0 and (batch * rows_per_batch) % FFN_ROWS == 0
    meta = jnp.broadcast_to(meta_tokens[None].astype(x.dtype), (batch, N_META, d))
    tail = jnp.zeros((batch, rows_per_batch - length, d), x.dtype)
    h = jnp.concatenate([meta, x, tail], axis=1).reshape(batch * rows_per_batch, d)

    w_in, w_pool, w_pw, w_out, w_gate, w_up, w_down = (
        w.astype(BF16) for w in (w_in, w_pool, w_pw, w_out, w_gate, w_up, w_down))
    vec = lambda a: a[:, None, :]
    pre_mix_g, pool_scale, b_dw, conv_ln_g, conv_ln_b, post_mix_g, pre_ffn_g, post_ffn_g = (
        vec(a) for a in (pre_mix_g, pool_scale, b_dw, conv_ln_g, conv_ln_b, post_mix_g, pre_ffn_g, post_ffn_g))
    g_attn, g_pool, g_conv = (vec(a) for a in jnp.split(mix_out_g, [d_attn, d_attn + d_pool], axis=1))

    for i in range(depth):
        q, k, v, u_pool, u_conv = _in_proj(i, h, pre_mix_g, w_in, d_attn, d_pool, 2 * d_conv)
        o_attn = _attention(i, q, k, v, g_attn, batch, rows_per_batch)
        o_pool, o_conv = _mixers(i, u_pool, u_conv, w_pool, pool_scale, g_pool, w_dw, b_dw, conv_ln_g, conv_ln_b,
                                 w_pw, g_conv, batch, rows_per_batch)
        h = _out_proj(i, o_attn, o_pool, o_conv, h, w_out, post_mix_g)
        h = _ffn(i, h, pre_ffn_g, w_gate, w_up, w_down, post_ffn_g)

    return h.reshape(batch, rows_per_batch, d)[:, N_META:length]
```

```python
import functools

import jax
import jax.numpy as jnp
from jax import lax
from jax.experimental import pallas as pl
from jax.experimental.pallas import tpu as pltpu

N_META = 16
N_HEADS_SB = 16
POOL_WINDOWS = (2, 4, 8, 16)
CONV_WIDTH = 31
EPS = 1e-6

LANES = 128
SUBLANES = 8
VMEM_BYTES_V7X = 64 * 1024 * 1024
VMEM_LIMIT = VMEM_BYTES_V7X - 8 * 1024 * 1024

SEQ_TILE = 128
TOP_ROWS = 32
ROW_ALIGN = 384
PROJ_ROWS = 384
MIX_ROWS = 384
FFN_ROWS = 768
FFN_COLS = 512
CONV_HALO = 32
POOL_HALO = 16
CONV_CHUNK = 32

LOG2_E = 1.4426950408889634
LOG2_WEIGHT_FLOOR = -127.0

F32 = jnp.float32
BF16 = jnp.bfloat16


def _rms(x, g):
    return x * lax.rsqrt(jnp.mean(x * x, axis=-1, keepdims=True) + EPS) * g


def _resident(shape):
    return pl.BlockSpec(shape, lambda *_: (0,) * len(shape), pipeline_mode=pl.Buffered(1))


def _layer_resident(layer, shape):
    return pl.BlockSpec((None,) + tuple(shape), lambda *_: (layer,) + (0,) * len(shape),
                        pipeline_mode=pl.Buffered(1))


def _params(n_grid_axes):
    return pltpu.CompilerParams(dimension_semantics=("arbitrary",) * n_grid_axes, vmem_limit_bytes=VMEM_LIMIT)


def _in_proj_kernel(h_ref, g_ref, w_ref, q_ref, k_ref, v_ref, pool_ref, conv_ref, *, d_attn, d_pool, q_scale):
    u = _rms(h_ref[...], g_ref[...]).astype(BF16)
    o = 0
    q_ref[...] = (jnp.dot(u, w_ref[:, o:o + d_attn], preferred_element_type=F32) * q_scale).astype(BF16)
    o += d_attn
    k_ref[...] = jnp.dot(u, w_ref[:, o:o + d_attn], preferred_element_type=F32).astype(BF16)
    o += d_attn
    v_ref[...] = jnp.dot(u, w_ref[:, o:o + d_attn], preferred_element_type=F32).astype(BF16)
    o += d_attn
    pool_ref[...] = jnp.dot(u, w_ref[:, o:o + d_pool], preferred_element_type=F32)
    o += d_pool
    d_conv = conv_ref.shape[1]
    conv_ref[...] = (jnp.dot(u, w_ref[:, o:o + d_conv], preferred_element_type=F32)
                     * jax.nn.sigmoid(jnp.dot(u, w_ref[:, o + d_conv:], preferred_element_type=F32)))


def _in_proj(layer, h, g, w, d_attn, d_pool, d_conv):
    rows, d = h.shape
    tm = PROJ_ROWS
    row_spec = lambda n: pl.BlockSpec((tm, n), lambda i: (i, 0))
    head_dim = d_attn // N_HEADS_SB
    return pl.pallas_call(
        functools.partial(_in_proj_kernel, d_attn=d_attn, d_pool=d_pool, q_scale=LOG2_E * head_dim ** -0.5),
        grid=(rows // tm,),
        in_specs=[row_spec(d), _layer_resident(layer, (1, d)), _layer_resident(layer, w.shape[1:])],
        out_specs=[row_spec(d_attn), row_spec(d_attn), row_spec(d_attn), row_spec(d_pool), row_spec(d_conv)],
        out_shape=[jax.ShapeDtypeStruct((rows, d_attn), BF16)] * 3
        + [jax.ShapeDtypeStruct((rows, d_pool), F32), jax.ShapeDtypeStruct((rows, d_conv), F32)],
        compiler_params=_params(1),
        name="in_proj",
    )(h, g, w)


def _attn_kernel(q_ref, k_ref, v_ref, tri_ref, g_ref, o_ref, q2_ref, acc_ref, c_ref, *, n_pairs):
    t = SEQ_TILE
    qi = pl.program_id(1)
    pairs = range(n_pairs)
    lanes = [slice(p * LANES, (p + 1) * LANES) for p in pairs]
    first_head = lax.broadcasted_iota(jnp.int32, (t, LANES), 1) < LANES // 2
    row = lax.broadcasted_iota(jnp.int32, (2 * t, t), 0)
    col = lax.broadcasted_iota(jnp.int32, (2 * t, t), 1)
    causal = col < (row & (t - 1))
    zero = jnp.zeros((), BF16)
    for p in pairs:
        qp = q_ref[:, lanes[p]]
        q2_ref[p] = jnp.concatenate([jnp.where(first_head, qp, zero), jnp.where(first_head, zero, qp)], axis=0)

    def head_rows(ref, p, r):
        return ref[p] if r == t else jnp.concatenate([ref[p, :r], ref[p, t:t + r]], axis=0)

    def key_tile(kb, r, diagonal=False):
        start = pl.multiple_of(kb * t, t)
        z = [lax.dot_general(head_rows(q2_ref, p, r), k_ref[pl.ds(start, t), lanes[p]], (((1,), (1,)), ((), ())),
                             preferred_element_type=F32) for p in pairs]
        v = [v_ref[pl.ds(start, t), lanes[p]] for p in pairs]
        c_old = None if diagonal else [head_rows(c_ref, p, r) for p in pairs]
        log_beta, later, c_new = [], [], []
        for p in pairs:
            z_pos = jnp.maximum(z[p], 0.0)
            z_neg = z[p] - z_pos
            log_term = jnp.log(1.0 + jnp.exp2(z_neg - z_pos)) * LOG2_E
            soft = z_pos + log_term
            if diagonal:
                soft = jnp.where(causal, soft, 0.0)
            later.append(jnp.dot(soft.astype(BF16), tri_ref[...], preferred_element_type=F32))
            log_beta.append(z_neg - log_term)
            c = jnp.sum(soft, axis=1, keepdims=True)
            c_new.append(jnp.broadcast_to(c, soft.shape) if diagonal else c_old[p] + c)
        o = []
        for p in pairs:
            x = log_beta[p] - later[p]
            if not diagonal:
                x = x - c_old[p]
            w = jnp.exp2(x)
            if diagonal:
                w = jnp.where(causal, w, 0.0)
            o.append(jnp.dot(w.astype(BF16), v[p], preferred_element_type=F32))
        c_min = None
        for p in pairs:
            first = lax.broadcasted_iota(jnp.int32, (r, LANES), 1) < LANES // 2
            o_pair = jnp.where(first, o[p][:r], o[p][r:])
            if diagonal:
                acc_ref[:, lanes[p]] = o_pair
            else:
                acc_ref[:r, lanes[p]] += o_pair
            if r == t:
                c_ref[p] = c_new[p]
            else:
                c_ref[p, :r] = c_new[p][:r]
                c_ref[p, t:t + r] = c_new[p][r:]
            c_min = c_new[p] if c_min is None else jnp.minimum(c_min, c_new[p])
        if r < t:
            return jnp.min(c_min)
        top = jnp.minimum(jnp.min(c_min[:TOP_ROWS]), jnp.min(c_min[t:t + TOP_ROWS]))
        rest = jnp.minimum(jnp.min(c_min[TOP_ROWS:t]), jnp.min(c_min[t + TOP_ROWS:]))
        return top, rest

    def unfinished(c_min):
        return c_min < -LOG2_WEIGHT_FLOOR

    def full_body(s):
        kb, _, _ = s
        return (kb - 1,) + key_tile(kb, t)

    def top_body(s):
        kb, _ = s
        return kb - 1, key_tile(kb, TOP_ROWS)

    kb, top, _ = lax.while_loop(lambda s: jnp.logical_and(s[0] >= 0, unfinished(s[2])), full_body,
                                (qi - 1,) + key_tile(qi, t, diagonal=True))
    lax.while_loop(lambda s: jnp.logical_and(s[0] >= 0, unfinished(s[1])), top_body, (kb, top))
    o_ref[...] = _rms(acc_ref[...], g_ref[...]).astype(BF16)


def _attention(layer, q, k, v, g, batch, rows_per_batch):
    rows, d_attn = q.shape
    t = SEQ_TILE
    nq = rows_per_batch // t
    j = jnp.arange(t)
    tri = (j[:, None] > j[None, :]).astype(BF16)
    n_pairs = d_attn // LANES
    kv_spec = pl.BlockSpec((rows_per_batch, d_attn), lambda b, i: (b, 0), pipeline_mode=pl.Buffered(1))
    tile_spec = pl.BlockSpec((t, d_attn), lambda b, i: (b * nq + i, 0))
    return pl.pallas_call(
        functools.partial(_attn_kernel, n_pairs=n_pairs),
        grid=(batch, nq),
        in_specs=[tile_spec, kv_spec, kv_spec, _resident(tri.shape), _layer_resident(layer, (1, d_attn))],
        out_specs=tile_spec,
        out_shape=jax.ShapeDtypeStruct((rows, d_attn), BF16),
        scratch_shapes=[pltpu.VMEM((n_pairs, 2 * t, LANES), BF16), pltpu.VMEM((t, d_attn), F32),
                        pltpu.VMEM((n_pairs, 2 * t, t), F32)],
        compiler_params=_params(2),
        name="sb_attention",
    )(q, k, v, tri, g)


def _mixers_kernel(pool_ref, pool_halo_ref, conv_ref, conv_halo_ref, w_pool_ref, pool_scale_ref, g_pool_ref,
                   w_dw_ref, b_dw_ref, ln_g_ref, ln_b_ref, w_pw_ref, g_conv_ref,
                   o_pool_ref, o_conv_ref, pool_ext, conv_ext, y_ref, *, d_conv):
    tl = pool_ref.shape[0]
    ti = pl.program_id(1)
    has_history = ti > 0

    pool_ext[:POOL_HALO, :] = jnp.where(has_history, pool_halo_ref[...], 0.0)
    pool_ext[POOL_HALO:, :] = pool_ref[...]
    pos = ti * tl + lax.broadcasted_iota(jnp.int32, (tl, 1), 0)
    mixed = []
    sq = jnp.zeros((tl, 1), F32)
    for gi, window in enumerate(POOL_WINDOWS):
        lanes = slice(gi * LANES, (gi + 1) * LANES)
        u = pool_ext[POOL_HALO:, lanes]
        acc = u
        for back in range(1, window):
            acc = acc + pool_ext[POOL_HALO - back:POOL_HALO - back + tl, lanes]
        count = jnp.minimum(pos + 1, window).astype(F32)
        pooled = acc / count - u
        m = jnp.dot(pooled.astype(BF16), w_pool_ref[gi], preferred_element_type=F32) * pool_scale_ref[:, lanes]
        sq = sq + jnp.sum(m * m, axis=-1, keepdims=True)
        mixed.append(m)
    d_pool = LANES * len(POOL_WINDOWS)
    inv = lax.rsqrt(sq / d_pool + EPS)
    for gi, m in enumerate(mixed):
        lanes = slice(gi * LANES, (gi + 1) * LANES)
        o_pool_ref[:, lanes] = (m * inv * g_pool_ref[:, lanes]).astype(BF16)

    n_ext = CONV_HALO + tl
    conv_ext[0, :CONV_HALO, :] = jnp.where(has_history, conv_halo_ref[...], 0.0)
    conv_ext[0, CONV_HALO:, :] = conv_ref[...]
    for r in range(1, SUBLANES):
        conv_ext[r, :n_ext - SUBLANES, :] = conv_ext[0, r:r + n_ext - SUBLANES, :]
    first_tap = CONV_HALO - (CONV_WIDTH - 1)

    for base in range(0, tl, CONV_CHUNK):
        acc = jnp.broadcast_to(b_dw_ref[...], (CONV_CHUNK, d_conv))
        for tap in range(CONV_WIDTH):
            shift = (first_tap + tap) % SUBLANES
            lo = base + first_tap + tap - shift
            acc = acc + w_dw_ref[tap:tap + 1, :] * conv_ext[shift, lo:lo + CONV_CHUNK, :]
        y_ref[base:base + CONV_CHUNK, :] = acc
    y = y_ref[...]
    mu = jnp.mean(y, axis=-1, keepdims=True)
    yc = y - mu
    var = jnp.mean(yc * yc, axis=-1, keepdims=True)
    y = yc * lax.rsqrt(var + EPS) * ln_g_ref[...] + ln_b_ref[...]
    y = y * jax.nn.sigmoid(y)
    o = jnp.dot(y.astype(BF16), w_pw_ref[...], preferred_element_type=F32)
    o_conv_ref[...] = _rms(o, g_conv_ref[...]).astype(BF16)


def _mixers(layer, u_pool, u_conv, w_pool, pool_scale, g_pool, w_dw, b_dw, ln_g, ln_b, w_pw, g_conv, batch,
            rows_per_batch):
    rows, d_pool = u_pool.shape
    d_conv = u_conv.shape[1]
    tl = MIX_ROWS
    nt = rows_per_batch // tl

    def cur(n):
        return pl.BlockSpec((tl, n), lambda b, i: (b * nt + i, 0))

    def halo(n, h):
        return pl.BlockSpec((h, n), lambda b, i: (jnp.maximum((b * nt + i) * (tl // h) - 1, 0), 0))

    lr = functools.partial(_layer_resident, layer)
    return pl.pallas_call(
        functools.partial(_mixers_kernel, d_conv=d_conv),
        grid=(batch, nt),
        in_specs=[cur(d_pool), halo(d_pool, POOL_HALO), cur(d_conv), halo(d_conv, CONV_HALO),
                  lr(w_pool.shape[1:]), lr((1, d_pool)), lr((1, d_pool)),
                  lr(w_dw.shape[1:]), lr((1, d_conv)), lr((1, d_conv)), lr((1, d_conv)),
                  lr(w_pw.shape[1:]), lr((1, d_conv))],
        out_specs=[cur(d_pool), cur(d_conv)],
        out_shape=[jax.ShapeDtypeStruct((rows, d_pool), BF16), jax.ShapeDtypeStruct((rows, d_conv), BF16)],
        scratch_shapes=[pltpu.VMEM((POOL_HALO + tl, d_pool), F32), pltpu.VMEM((SUBLANES, CONV_HALO + tl, d_conv), F32),
                        pltpu.VMEM((tl, d_conv), F32)],
        compiler_params=_params(2),
        name="pool_conv_mixers",
    )(u_pool, u_pool, u_conv, u_conv, w_pool, pool_scale, g_pool, w_dw, b_dw, ln_g, ln_b, w_pw, g_conv)


def _out_proj_kernel(attn_ref, pool_ref, conv_ref, h_ref, w_ref, g_ref, o_ref):
    d_attn, d_pool = attn_ref.shape[1], pool_ref.shape[1]
    y = jnp.dot(attn_ref[...], w_ref[:d_attn, :], preferred_element_type=F32)
    y = y + jnp.dot(pool_ref[...], w_ref[d_attn:d_attn + d_pool, :], preferred_element_type=F32)
    y = y + jnp.dot(conv_ref[...], w_ref[d_attn + d_pool:, :], preferred_element_type=F32)
    o_ref[...] = h_ref[...] + _rms(y, g_ref[...])


def _out_proj(layer, attn, pool, conv, h, w, g):
    rows, d = h.shape
    tm = PROJ_ROWS
    row_spec = lambda n: pl.BlockSpec((tm, n), lambda i: (i, 0))
    return pl.pallas_call(
        _out_proj_kernel,
        grid=(rows // tm,),
        in_specs=[row_spec(attn.shape[1]), row_spec(pool.shape[1]), row_spec(conv.shape[1]), row_spec(d),
                  _layer_resident(layer, w.shape[1:]), _layer_resident(layer, (1, d))],
        out_specs=row_spec(d),
        out_shape=jax.ShapeDtypeStruct((rows, d), F32),
        compiler_params=_params(1),
        name="out_proj",
    )(attn, pool, conv, h, w, g)


def _ffn_kernel(h_ref, g_pre_ref, w_gate_ref, w_up_ref, w_down_ref, g_post_ref, o_ref, u_ref, acc_ref):
    c = pl.program_id(1)

    @pl.when(c == 0)
    def _():
        u_ref[...] = _rms(h_ref[...], g_pre_ref[...]).astype(BF16)
        acc_ref[...] = jnp.zeros_like(acc_ref)

    u = u_ref[...]
    gate = jnp.dot(u, w_gate_ref[...], preferred_element_type=F32)
    up = jnp.dot(u, w_up_ref[...], preferred_element_type=F32)
    a = (gate * jax.nn.sigmoid(gate) * up).astype(BF16)
    tf = w_down_ref.shape[0]
    for n in range(0, acc_ref.shape[1], tf):
        acc_ref[:, n:n + tf] += jnp.dot(a, w_down_ref[:, n:n + tf], preferred_element_type=F32)

    @pl.when(c == pl.num_programs(1) - 1)
    def _():
        o_ref[...] = h_ref[...] + _rms(acc_ref[...], g_post_ref[...])


def _ffn(layer, h, g_pre, w_gate, w_up, w_down, g_post):
    rows, d = h.shape
    d_ff = w_gate.shape[2]
    tm, tf = FFN_ROWS, FFN_COLS
    return pl.pallas_call(
        _ffn_kernel,
        grid=(rows // tm, d_ff // tf),
        in_specs=[pl.BlockSpec((tm, d), lambda i, c: (i, 0)), _layer_resident(layer, (1, d)),
                  pl.BlockSpec((None, d, tf), lambda i, c: (layer, 0, c)),
                  pl.BlockSpec((None, d, tf), lambda i, c: (layer, 0, c)),
                  pl.BlockSpec((None, tf, d), lambda i, c: (layer, c, 0)), _layer_resident(layer, (1, d))],
        out_specs=pl.BlockSpec((tm, d), lambda i, c: (i, 0)),
        out_shape=jax.ShapeDtypeStruct((rows, d), F32),
        scratch_shapes=[pltpu.VMEM((tm, d), BF16), pltpu.VMEM((tm, d), F32)],
        compiler_params=_params(2),
        name="swiglu_ffn",
    )(h, g_pre, w_gate, w_up, w_down, g_post)


def kernel(x, meta_tokens, pre_mix_g, w_in, w_pool, pool_scale, w_dw, b_dw, conv_ln_g, conv_ln_b, w_pw, mix_out_g,
           w_out, post_mix_g, pre_ffn_g, w_gate, w_up, w_down, post_ffn_g):
    batch, seq, d = x.shape
    depth = w_in.shape[0]
    d_pool = pool_scale.shape[1]
    d_conv = w_pw.shape[1]
    d_attn = w_out.shape[1] - d_pool - d_conv
    assert d_pool == LANES * len(POOL_WINDOWS) and d_attn % (2 * LANES) == 0
    assert d_attn // N_HEADS_SB == LANES // 2

    length = N_META + seq
    rows_per_batch = -(-length // ROW_ALIGN) * ROW_ALIGN
    assert rows_per_batch % SEQ_TILE == 0 and (batch * rows_per_batch) % FFN_ROWS == 0
    meta = jnp.broadcast_to(meta_tokens[None].astype(x.dtype), (batch, N_META, d))
    tail = jnp.zeros((batch, rows_per_batch - length, d), x.dtype)
    h = jnp.concatenate([meta, x, tail], axis=1).reshape(batch * rows_per_batch, d)

    w_in, w_pool, w_pw, w_out, w_gate, w_up, w_down = (
        w.astype(BF16) for w in (w_in, w_pool, w_pw, w_out, w_gate, w_up, w_down))
    vec = lambda a: a[:, None, :]
    pre_mix_g, pool_scale, b_dw, conv_ln_g, conv_ln_b, post_mix_g, pre_ffn_g, post_ffn_g = (
        vec(a) for a in (pre_mix_g, pool_scale, b_dw, conv_ln_g, conv_ln_b, post_mix_g, pre_ffn_g, post_ffn_g))
    g_attn, g_pool, g_conv = (vec(a) for a in jnp.split(mix_out_g, [d_attn, d_attn + d_pool], axis=1))

    for i in range(depth):
        q, k, v, u_pool, u_conv = _in_proj(i, h, pre_mix_g, w_in, d_attn, d_pool, d_conv)
        o_attn = _attention(i, q, k, v, g_attn, batch, rows_per_batch)
        o_pool, o_conv = _mixers(i, u_pool, u_conv, w_pool, pool_scale, g_pool, w_dw, b_dw, conv_ln_g, conv_ln_b,
                                 w_pw, g_conv, batch, rows_per_batch)
        h = _out_proj(i, o_attn, o_pool, o_conv, h, w_out, post_mix_g)
        h = _ffn(i, h, pre_ffn_g, w_gate, w_up, w_down, post_ffn_g)

    return h.reshape(batch, rows_per_batch, d)[:, N_META:length]
```

```python
import functools
from typing import NamedTuple

import jax
import jax.numpy as jnp
from jax import lax
from jax.experimental import pallas as pl
from jax.experimental.pallas import tpu as pltpu

N_META = 16
N_HEADS_SB = 16
POOL_WINDOWS = (2, 4, 8, 16)
CONV_WIDTH = 31
EPS = 1e-6

LANES = 128
SUBLANES = 8
VMEM_BYTES_V7X = 64 * 1024 * 1024
VMEM_LIMIT = VMEM_BYTES_V7X - 8 * 1024 * 1024

SEQ_TILE = 128
TOP_ROWS = 32
ROW_ALIGN = 384
PROJ_ROWS = 384
MIX_ROWS = 384
FFN_ROWS = 768
LAST_FFN_ROWS = 512
FFN_COLS = 512
CONV_HALO = 32
POOL_HALO = 16
CONV_CHUNK = 32

LOG2_E = 1.4426950408889634
LOG2_WEIGHT_FLOOR = -127.0

F32 = jnp.float32
BF16 = jnp.bfloat16


def _rms(x, g):
    return x * lax.rsqrt(jnp.mean(x * x, axis=-1, keepdims=True) + EPS) * g


def _resident(shape):
    return pl.BlockSpec(shape, lambda *_: (0,) * len(shape), pipeline_mode=pl.Buffered(1))


def _layer_resident(layer, shape):
    return pl.BlockSpec((None,) + tuple(shape), lambda *_: (layer,) + (0,) * len(shape),
                        pipeline_mode=pl.Buffered(1))


def _params(n_grid_axes):
    return pltpu.CompilerParams(dimension_semantics=("arbitrary",) * n_grid_axes, vmem_limit_bytes=VMEM_LIMIT)


class _Tokens(NamedTuple):
    x: jax.Array
    meta: jax.Array
    seq: int
    rows_per_batch: int

    @property
    def shape(self):
        return (self.x.shape[0] // self.seq * self.rows_per_batch, self.x.shape[1])


def _stream_inputs(h, tm):
    if not isinstance(h, _Tokens):
        return [h], [pl.BlockSpec((tm, h.shape[1]), lambda i, *_: (i, 0))], None
    seq, d = h.seq, h.x.shape[1]
    n_meta = h.meta.shape[0]
    tiles_per_batch = h.rows_per_batch // tm
    n_pad = h.rows_per_batch - n_meta - seq
    assert h.rows_per_batch % tm == 0 and n_pad <= tm and n_meta <= tm
    assert all(n % SUBLANES == 0 for n in (n_meta, n_pad, seq, tm))

    def start(i):
        b, t = i // tiles_per_batch, i % tiles_per_batch
        return pl.multiple_of(b * seq + jnp.clip(t * tm - n_meta, 0, seq - tm), SUBLANES)

    spec = pl.BlockSpec((pl.Element(tm), pl.Element(d)), lambda i, *_: (start(i), 0))
    return [h.x, h.meta], [spec, _resident(h.meta.shape)], (tiles_per_batch, n_pad)


def _stream_tile(stream_refs, layout):
    if layout is None:
        return stream_refs[0][...]
    x_ref, meta_ref = stream_refs
    tiles_per_batch, n_pad = layout
    blk = x_ref[...]
    tm, d = blk.shape
    t = lax.rem(pl.program_id(0), tiles_per_batch)
    first = jnp.concatenate([meta_ref[...], blk[:tm - meta_ref.shape[0]]], axis=0)
    last = jnp.concatenate([blk[n_pad:], jnp.zeros((n_pad, d), blk.dtype)], axis=0)
    return jnp.where(t == 0, first, jnp.where(t == tiles_per_batch - 1, last, blk))


def _in_proj_kernel(*refs, n_stream, layout, d_attn, d_pool, q_scale):
    g_ref, w_ref, q_ref, k_ref, v_ref, pool_ref, conv_ref = refs[n_stream:]
    u = _rms(_stream_tile(refs[:n_stream], layout), g_ref[...]).astype(BF16)
    o = 0
    q_ref[...] = (jnp.dot(u, w_ref[:, o:o + d_attn], preferred_element_type=F32) * q_scale).astype(BF16)
    o += d_attn
    k_ref[...] = jnp.dot(u, w_ref[:, o:o + d_attn], preferred_element_type=F32).astype(BF16)
    o += d_attn
    v_ref[...] = jnp.dot(u, w_ref[:, o:o + d_attn], preferred_element_type=F32).astype(BF16)
    o += d_attn
    pool_ref[...] = jnp.dot(u, w_ref[:, o:o + d_pool], preferred_element_type=F32)
    o += d_pool
    d_conv = conv_ref.shape[1]
    conv_ref[...] = (jnp.dot(u, w_ref[:, o:o + d_conv], preferred_element_type=F32)
                     * jax.nn.sigmoid(jnp.dot(u, w_ref[:, o + d_conv:], preferred_element_type=F32)))


def _in_proj(layer, h, g, w, d_attn, d_pool, d_conv):
    rows, d = h.shape
    tm = PROJ_ROWS
    row_spec = lambda n: pl.BlockSpec((tm, n), lambda i: (i, 0))
    head_dim = d_attn // N_HEADS_SB
    stream, stream_specs, layout = _stream_inputs(h, tm)
    return pl.pallas_call(
        functools.partial(_in_proj_kernel, n_stream=len(stream), layout=layout, d_attn=d_attn, d_pool=d_pool,
                          q_scale=LOG2_E * head_dim ** -0.5),
        grid=(rows // tm,),
        in_specs=stream_specs + [_layer_resident(layer, (1, d)), _layer_resident(layer, w.shape[1:])],
        out_specs=[row_spec(d_attn), row_spec(d_attn), row_spec(d_attn), row_spec(d_pool), row_spec(d_conv)],
        out_shape=[jax.ShapeDtypeStruct((rows, d_attn), BF16)] * 3
        + [jax.ShapeDtypeStruct((rows, d_pool), F32), jax.ShapeDtypeStruct((rows, d_conv), F32)],
        compiler_params=_params(1),
        name="in_proj",
    )(*stream, g, w)


def _attn_kernel(q_ref, k_ref, v_ref, tri_ref, g_ref, o_ref, q2_ref, acc_ref, c_ref, *, n_pairs):
    t = SEQ_TILE
    qi = pl.program_id(1)
    pairs = range(n_pairs)
    lanes = [slice(p * LANES, (p + 1) * LANES) for p in pairs]
    first_head = lax.broadcasted_iota(jnp.int32, (t, LANES), 1) < LANES // 2
    row = lax.broadcasted_iota(jnp.int32, (2 * t, t), 0)
    col = lax.broadcasted_iota(jnp.int32, (2 * t, t), 1)
    causal = col < (row & (t - 1))
    zero = jnp.zeros((), BF16)
    for p in pairs:
        qp = q_ref[:, lanes[p]]
        q2_ref[p] = jnp.concatenate([jnp.where(first_head, qp, zero), jnp.where(first_head, zero, qp)], axis=0)

    def head_rows(ref, p, r):
        return ref[p] if r == t else jnp.concatenate([ref[p, :r], ref[p, t:t + r]], axis=0)

    def key_tile(kb, r, diagonal=False):
        start = pl.multiple_of(kb * t, t)
        z = [lax.dot_general(head_rows(q2_ref, p, r), k_ref[pl.ds(start, t), lanes[p]], (((1,), (1,)), ((), ())),
                             preferred_element_type=F32) for p in pairs]
        v = [v_ref[pl.ds(start, t), lanes[p]] for p in pairs]
        c_old = None if diagonal else [head_rows(c_ref, p, r) for p in pairs]
        log_beta, later, c_new = [], [], []
        for p in pairs:
            z_pos = jnp.maximum(z[p], 0.0)
            z_neg = z[p] - z_pos
            log_term = jnp.log(1.0 + jnp.exp2(z_neg - z_pos)) * LOG2_E
            soft = z_pos + log_term
            if diagonal:
                soft = jnp.where(causal, soft, 0.0)
            later.append(jnp.dot(soft.astype(BF16), tri_ref[...], preferred_element_type=F32))
            log_beta.append(z_neg - log_term)
            c = jnp.sum(soft, axis=1, keepdims=True)
            c_new.append(jnp.broadcast_to(c, soft.shape) if diagonal else c_old[p] + c)
        o = []
        for p in pairs:
            x = log_beta[p] - later[p]
            if not diagonal:
                x = x - c_old[p]
            w = jnp.exp2(x)
            if diagonal:
                w = jnp.where(causal, w, 0.0)
            o.append(jnp.dot(w.astype(BF16), v[p], preferred_element_type=F32))
        c_min = None
        for p in pairs:
            first = lax.broadcasted_iota(jnp.int32, (r, LANES), 1) < LANES // 2
            o_pair = jnp.where(first, o[p][:r], o[p][r:])
            if diagonal:
                acc_ref[:, lanes[p]] = o_pair
            else:
                acc_ref[:r, lanes[p]] += o_pair
            if r == t:
                c_ref[p] = c_new[p]
            else:
                c_ref[p, :r] = c_new[p][:r]
                c_ref[p, t:t + r] = c_new[p][r:]
            c_min = c_new[p] if c_min is None else jnp.minimum(c_min, c_new[p])
        if r < t:
            return jnp.min(c_min)
        top = jnp.minimum(jnp.min(c_min[:TOP_ROWS]), jnp.min(c_min[t:t + TOP_ROWS]))
        rest = jnp.minimum(jnp.min(c_min[TOP_ROWS:t]), jnp.min(c_min[t + TOP_ROWS:]))
        return top, rest

    def unfinished(c_min):
        return c_min < -LOG2_WEIGHT_FLOOR

    def full_body(s):
        kb, _, _ = s
        return (kb - 1,) + key_tile(kb, t)

    def top_body(s):
        kb, _ = s
        return kb - 1, key_tile(kb, TOP_ROWS)

    kb, top, _ = lax.while_loop(lambda s: jnp.logical_and(s[0] >= 0, unfinished(s[2])), full_body,
                                (qi - 1,) + key_tile(qi, t, diagonal=True))
    lax.while_loop(lambda s: jnp.logical_and(s[0] >= 0, unfinished(s[1])), top_body, (kb, top))
    o_ref[...] = _rms(acc_ref[...], g_ref[...]).astype(BF16)


def _attention(layer, q, k, v, g, batch, rows_per_batch):
    rows, d_attn = q.shape
    t = SEQ_TILE
    nq = rows_per_batch // t
    j = jnp.arange(t)
    tri = (j[:, None] > j[None, :]).astype(BF16)
    n_pairs = d_attn // LANES
    kv_spec = pl.BlockSpec((rows_per_batch, d_attn), lambda b, i: (b, 0), pipeline_mode=pl.Buffered(1))
    tile_spec = pl.BlockSpec((t, d_attn), lambda b, i: (b * nq + i, 0))
    return pl.pallas_call(
        functools.partial(_attn_kernel, n_pairs=n_pairs),
        grid=(batch, nq),
        in_specs=[tile_spec, kv_spec, kv_spec, _resident(tri.shape), _layer_resident(layer, (1, d_attn))],
        out_specs=tile_spec,
        out_shape=jax.ShapeDtypeStruct((rows, d_attn), BF16),
        scratch_shapes=[pltpu.VMEM((n_pairs, 2 * t, LANES), BF16), pltpu.VMEM((t, d_attn), F32),
                        pltpu.VMEM((n_pairs, 2 * t, t), F32)],
        compiler_params=_params(2),
        name="sb_attention",
    )(q, k, v, tri, g)


def _mixers_kernel(pool_ref, pool_halo_ref, conv_ref, conv_halo_ref, w_pool_ref, pool_scale_ref, g_pool_ref,
                   w_dw_ref, b_dw_ref, ln_g_ref, ln_b_ref, w_pw_ref, g_conv_ref,
                   o_pool_ref, o_conv_ref, pool_ext, conv_ext, y_ref, *, d_conv):
    tl = pool_ref.shape[0]
    ti = pl.program_id(1)
    has_history = ti > 0

    pool_ext[:POOL_HALO, :] = jnp.where(has_history, pool_halo_ref[...], 0.0)
    pool_ext[POOL_HALO:, :] = pool_ref[...]
    pos = ti * tl + lax.broadcasted_iota(jnp.int32, (tl, 1), 0)
    mixed = []
    sq = jnp.zeros((tl, 1), F32)
    for gi, window in enumerate(POOL_WINDOWS):
        lanes = slice(gi * LANES, (gi + 1) * LANES)
        u = pool_ext[POOL_HALO:, lanes]
        acc = u
        for back in range(1, window):
            acc = acc + pool_ext[POOL_HALO - back:POOL_HALO - back + tl, lanes]
        count = jnp.minimum(pos + 1, window).astype(F32)
        pooled = acc / count - u
        m = jnp.dot(pooled.astype(BF16), w_pool_ref[gi], preferred_element_type=F32) * pool_scale_ref[:, lanes]
        sq = sq + jnp.sum(m * m, axis=-1, keepdims=True)
        mixed.append(m)
    d_pool = LANES * len(POOL_WINDOWS)
    inv = lax.rsqrt(sq / d_pool + EPS)
    for gi, m in enumerate(mixed):
        lanes = slice(gi * LANES, (gi + 1) * LANES)
        o_pool_ref[:, lanes] = (m * inv * g_pool_ref[:, lanes]).astype(BF16)

    n_ext = CONV_HALO + tl
    conv_ext[0, :CONV_HALO, :] = jnp.where(has_history, conv_halo_ref[...], 0.0)
    conv_ext[0, CONV_HALO:, :] = conv_ref[...]
    for r in range(1, SUBLANES):
        conv_ext[r, :n_ext - SUBLANES, :] = conv_ext[0, r:r + n_ext - SUBLANES, :]
    first_tap = CONV_HALO - (CONV_WIDTH - 1)

    for base in range(0, tl, CONV_CHUNK):
        acc = jnp.broadcast_to(b_dw_ref[...], (CONV_CHUNK, d_conv))
        for tap in range(CONV_WIDTH):
            shift = (first_tap + tap) % SUBLANES
            lo = base + first_tap + tap - shift
            acc = acc + w_dw_ref[tap:tap + 1, :] * conv_ext[shift, lo:lo + CONV_CHUNK, :]
        y_ref[base:base + CONV_CHUNK, :] = acc
    y = y_ref[...]
    mu = jnp.mean(y, axis=-1, keepdims=True)
    yc = y - mu
    var = jnp.mean(yc * yc, axis=-1, keepdims=True)
    y = yc * lax.rsqrt(var + EPS) * ln_g_ref[...] + ln_b_ref[...]
    y = y * jax.nn.sigmoid(y)
    o = jnp.dot(y.astype(BF16), w_pw_ref[...], preferred_element_type=F32)
    o_conv_ref[...] = _rms(o, g_conv_ref[...]).astype(BF16)


def _mixers(layer, u_pool, u_conv, w_pool, pool_scale, g_pool, w_dw, b_dw, ln_g, ln_b, w_pw, g_conv, batch,
            rows_per_batch):
    rows, d_pool = u_pool.shape
    d_conv = u_conv.shape[1]
    tl = MIX_ROWS
    nt = rows_per_batch // tl

    def cur(n):
        return pl.BlockSpec((tl, n), lambda b, i: (b * nt + i, 0))

    def halo(n, h):
        return pl.BlockSpec((h, n), lambda b, i: (jnp.maximum((b * nt + i) * (tl // h) - 1, 0), 0))

    lr = functools.partial(_layer_resident, layer)
    return pl.pallas_call(
        functools.partial(_mixers_kernel, d_conv=d_conv),
        grid=(batch, nt),
        in_specs=[cur(d_pool), halo(d_pool, POOL_HALO), cur(d_conv), halo(d_conv, CONV_HALO),
                  lr(w_pool.shape[1:]), lr((1, d_pool)), lr((1, d_pool)),
                  lr(w_dw.shape[1:]), lr((1, d_conv)), lr((1, d_conv)), lr((1, d_conv)),
                  lr(w_pw.shape[1:]), lr((1, d_conv))],
        out_specs=[cur(d_pool), cur(d_conv)],
        out_shape=[jax.ShapeDtypeStruct((rows, d_pool), BF16), jax.ShapeDtypeStruct((rows, d_conv), BF16)],
        scratch_shapes=[pltpu.VMEM((POOL_HALO + tl, d_pool), F32), pltpu.VMEM((SUBLANES, CONV_HALO + tl, d_conv), F32),
                        pltpu.VMEM((tl, d_conv), F32)],
        compiler_params=_params(2),
        name="pool_conv_mixers",
    )(u_pool, u_pool, u_conv, u_conv, w_pool, pool_scale, g_pool, w_dw, b_dw, ln_g, ln_b, w_pw, g_conv)


def _out_proj_kernel(*refs, n_stream, layout):
    attn_ref, pool_ref, conv_ref, w_ref, g_ref, o_ref = refs[n_stream:]
    d_attn, d_pool = attn_ref.shape[1], pool_ref.shape[1]
    y = jnp.dot(attn_ref[...], w_ref[:d_attn, :], preferred_element_type=F32)
    y = y + jnp.dot(pool_ref[...], w_ref[d_attn:d_attn + d_pool, :], preferred_element_type=F32)
    y = y + jnp.dot(conv_ref[...], w_ref[d_attn + d_pool:, :], preferred_element_type=F32)
    o_ref[...] = _stream_tile(refs[:n_stream], layout) + _rms(y, g_ref[...])


def _out_proj(layer, attn, pool, conv, h, w, g):
    rows, d = h.shape
    tm = PROJ_ROWS
    row_spec = lambda n: pl.BlockSpec((tm, n), lambda i: (i, 0))
    stream, stream_specs, layout = _stream_inputs(h, tm)
    return pl.pallas_call(
        functools.partial(_out_proj_kernel, n_stream=len(stream), layout=layout),
        grid=(rows // tm,),
        in_specs=stream_specs + [row_spec(attn.shape[1]), row_spec(pool.shape[1]), row_spec(conv.shape[1]),
                                 _layer_resident(layer, w.shape[1:]), _layer_resident(layer, (1, d))],
        out_specs=row_spec(d),
        out_shape=jax.ShapeDtypeStruct((rows, d), F32),
        compiler_params=_params(1),
        name="out_proj",
    )(*stream, attn, pool, conv, w, g)


def _ffn_kernel(h_ref, g_pre_ref, w_gate_ref, w_up_ref, w_down_ref, g_post_ref, o_ref, u_ref, acc_ref):
    c = pl.program_id(1)

    @pl.when(c == 0)
    def _():
        u_ref[...] = _rms(h_ref[...], g_pre_ref[...]).astype(BF16)
        acc_ref[...] = jnp.zeros_like(acc_ref)

    u = u_ref[...]
    gate = jnp.dot(u, w_gate_ref[...], preferred_element_type=F32)
    up = jnp.dot(u, w_up_ref[...], preferred_element_type=F32)
    a = (gate * jax.nn.sigmoid(gate) * up).astype(BF16)
    tf = w_down_ref.shape[0]
    for n in range(0, acc_ref.shape[1], tf):
        acc_ref[:, n:n + tf] += jnp.dot(a, w_down_ref[:, n:n + tf], preferred_element_type=F32)

    @pl.when(c == pl.num_programs(1) - 1)
    def _():
        o_ref[...] = h_ref[...] + _rms(acc_ref[...], g_post_ref[...])


def _ffn(layer, h, g_pre, w_gate, w_up, w_down, g_post, keep=None):
    rows, d = h.shape
    d_ff = w_gate.shape[2]
    tf = FFN_COLS
    if keep is None:
        tm, n_tiles = FFN_ROWS, rows // FFN_ROWS
        h_spec = pl.BlockSpec((tm, d), lambda i, c: (i, 0))
    else:
        rows_per_batch, first, count = keep
        tm = LAST_FFN_ROWS
        per_batch = count // tm
        n_tiles = (rows // rows_per_batch) * per_batch
        h_spec = pl.BlockSpec((pl.Element(tm), pl.Element(d)),
                              lambda i, c: (pl.multiple_of(
                                  (i // per_batch) * rows_per_batch + first + (i % per_batch) * tm, SUBLANES), 0))
        assert rows_per_batch % SUBLANES == 0 and first % SUBLANES == 0
    return pl.pallas_call(
        _ffn_kernel,
        grid=(n_tiles, d_ff // tf),
        in_specs=[h_spec, _layer_resident(layer, (1, d)),
                  pl.BlockSpec((None, d, tf), lambda i, c: (layer, 0, c)),
                  pl.BlockSpec((None, d, tf), lambda i, c: (layer, 0, c)),
                  pl.BlockSpec((None, tf, d), lambda i, c: (layer, c, 0)), _layer_resident(layer, (1, d))],
        out_specs=pl.BlockSpec((tm, d), lambda i, c: (i, 0)),
        out_shape=jax.ShapeDtypeStruct((n_tiles * tm, d), F32),
        scratch_shapes=[pltpu.VMEM((tm, d), BF16), pltpu.VMEM((tm, d), F32)],
        compiler_params=_params(2),
        name="swiglu_ffn",
    )(h, g_pre, w_gate, w_up, w_down, g_post)


def kernel(x, meta_tokens, pre_mix_g, w_in, w_pool, pool_scale, w_dw, b_dw, conv_ln_g, conv_ln_b, w_pw, mix_out_g,
           w_out, post_mix_g, pre_ffn_g, w_gate, w_up, w_down, post_ffn_g):
    batch, seq, d = x.shape
    depth = w_in.shape[0]
    d_pool = pool_scale.shape[1]
    d_conv = w_pw.shape[1]
    d_attn = w_out.shape[1] - d_pool - d_conv
    assert d_pool == LANES * len(POOL_WINDOWS) and d_attn % (2 * LANES) == 0
    assert d_attn // N_HEADS_SB == LANES // 2

    length = N_META + seq
    rows_per_batch = -(-length // ROW_ALIGN) * ROW_ALIGN
    assert rows_per_batch % SEQ_TILE == 0 and (batch * rows_per_batch) % FFN_ROWS == 0 and seq % LAST_FFN_ROWS == 0
    assert meta_tokens.shape[0] == N_META
    h = _Tokens(x.reshape(batch * seq, d), meta_tokens.astype(x.dtype), seq, rows_per_batch)

    w_in, w_pool, w_pw, w_out, w_gate, w_up, w_down = (
        w.astype(BF16) for w in (w_in, w_pool, w_pw, w_out, w_gate, w_up, w_down))
    vec = lambda a: a[:, None, :]
    pre_mix_g, pool_scale, b_dw, conv_ln_g, conv_ln_b, post_mix_g, pre_ffn_g, post_ffn_g = (
        vec(a) for a in (pre_mix_g, pool_scale, b_dw, conv_ln_g, conv_ln_b, post_mix_g, pre_ffn_g, post_ffn_g))
    g_attn, g_pool, g_conv = (vec(a) for a in jnp.split(mix_out_g, [d_attn, d_attn + d_pool], axis=1))

    for i in range(depth):
        q, k, v, u_pool, u_conv = _in_proj(i, h, pre_mix_g, w_in, d_attn, d_pool, d_conv)
        o_attn = _attention(i, q, k, v, g_attn, batch, rows_per_batch)
        o_pool, o_conv = _mixers(i, u_pool, u_conv, w_pool, pool_scale, g_pool, w_dw, b_dw, conv_ln_g, conv_ln_b,
                                 w_pw, g_conv, batch, rows_per_batch)
        h = _out_proj(i, o_attn, o_pool, o_conv, h, w_out, post_mix_g)
        keep = (rows_per_batch, N_META, seq) if i == depth - 1 else None
        h = _ffn(i, h, pre_ffn_g, w_gate, w_up, w_down, post_ffn_g, keep)

    return h.reshape(batch, seq, d)
```

```python
import functools
from typing import NamedTuple

import jax
import jax.numpy as jnp
from jax import lax
from jax.experimental import pallas as pl
from jax.experimental.pallas import tpu as pltpu

N_META = 16
N_HEADS_SB = 16
POOL_WINDOWS = (2, 4, 8, 16)
CONV_WIDTH = 31
EPS = 1e-6

LANES = 128
SUBLANES = 8
VMEM_BYTES_V7X = 64 * 1024 * 1024
VMEM_LIMIT = VMEM_BYTES_V7X - 8 * 1024 * 1024

SEQ_TILE = 128
TOP_ROWS = 32
ROW_ALIGN = 384
PROJ_ROWS = 384
MIX_ROWS = 384
FFN_ROWS = 768
LAST_FFN_ROWS = 512
FFN_COLS = 512
CONV_HALO = 32
POOL_HALO = 16
CONV_CHUNK = 32
CAST_BLOCK = 256

LOG2_E = 1.4426950408889634
LOG2_WEIGHT_FLOOR = -127.0

F32 = jnp.float32
BF16 = jnp.bfloat16


def _rms(x, g):
    return x * lax.rsqrt(jnp.mean(x * x, axis=-1, keepdims=True) + EPS) * g


def _resident(shape):
    return pl.BlockSpec(shape, lambda *_: (0,) * len(shape), pipeline_mode=pl.Buffered(1))


def _layer_resident(layer, shape):
    return pl.BlockSpec((None,) + tuple(shape), lambda *_: (layer,) + (0,) * len(shape),
                        pipeline_mode=pl.Buffered(1))


def _params(n_grid_axes):
    return pltpu.CompilerParams(dimension_semantics=("arbitrary",) * n_grid_axes, vmem_limit_bytes=VMEM_LIMIT)


class _Tokens(NamedTuple):
    x: jax.Array
    meta: jax.Array
    seq: int
    rows_per_batch: int

    @property
    def shape(self):
        return (self.x.shape[0] // self.seq * self.rows_per_batch, self.x.shape[1])


def _stream_inputs(h, tm):
    if not isinstance(h, _Tokens):
        return [h], [pl.BlockSpec((tm, h.shape[1]), lambda i, *_: (i, 0))], None
    seq, d = h.seq, h.x.shape[1]
    n_meta = h.meta.shape[0]
    tiles_per_batch = h.rows_per_batch // tm
    n_pad = h.rows_per_batch - n_meta - seq
    assert h.rows_per_batch % tm == 0 and n_pad <= tm and n_meta <= tm
    assert all(n % SUBLANES == 0 for n in (n_meta, n_pad, seq, tm))

    def start(i):
        b, t = i // tiles_per_batch, i % tiles_per_batch
        return pl.multiple_of(b * seq + jnp.clip(t * tm - n_meta, 0, seq - tm), SUBLANES)

    spec = pl.BlockSpec((pl.Element(tm), pl.Element(d)), lambda i, *_: (start(i), 0))
    return [h.x, h.meta], [spec, _resident(h.meta.shape)], (tiles_per_batch, n_pad)


def _stream_tile(stream_refs, layout):
    if layout is None:
        return stream_refs[0][...]
    x_ref, meta_ref = stream_refs
    tiles_per_batch, n_pad = layout
    blk = x_ref[...]
    tm, d = blk.shape
    t = lax.rem(pl.program_id(0), tiles_per_batch)
    first = jnp.concatenate([meta_ref[...], blk[:tm - meta_ref.shape[0]]], axis=0)
    last = jnp.concatenate([blk[n_pad:], jnp.zeros((n_pad, d), blk.dtype)], axis=0)
    return jnp.where(t == 0, first, jnp.where(t == tiles_per_batch - 1, last, blk))


def _in_proj_kernel(*refs, n_stream, layout, d_attn, d_pool, q_scale):
    g_ref, w_ref, q_ref, k_ref, v_ref, pool_ref, conv_ref = refs[n_stream:]
    u = _rms(_stream_tile(refs[:n_stream], layout), g_ref[...]).astype(BF16)
    o = 0
    q_ref[...] = (jnp.dot(u, w_ref[:, o:o + d_attn], preferred_element_type=F32) * q_scale).astype(BF16)
    o += d_attn
    k_ref[...] = jnp.dot(u, w_ref[:, o:o + d_attn], preferred_element_type=F32).astype(BF16)
    o += d_attn
    v_ref[...] = jnp.dot(u, w_ref[:, o:o + d_attn], preferred_element_type=F32).astype(BF16)
    o += d_attn
    pool_ref[...] = jnp.dot(u, w_ref[:, o:o + d_pool], preferred_element_type=F32)
    o += d_pool
    d_conv = conv_ref.shape[1]
    conv_ref[...] = (jnp.dot(u, w_ref[:, o:o + d_conv], preferred_element_type=F32)
                     * jax.nn.sigmoid(jnp.dot(u, w_ref[:, o + d_conv:], preferred_element_type=F32)))


def _in_proj(layer, h, g, w, d_attn, d_pool, d_conv):
    rows, d = h.shape
    tm = PROJ_ROWS
    row_spec = lambda n: pl.BlockSpec((tm, n), lambda i: (i, 0))
    head_dim = d_attn // N_HEADS_SB
    stream, stream_specs, layout = _stream_inputs(h, tm)
    return pl.pallas_call(
        functools.partial(_in_proj_kernel, n_stream=len(stream), layout=layout, d_attn=d_attn, d_pool=d_pool,
                          q_scale=LOG2_E * head_dim ** -0.5),
        grid=(rows // tm,),
        in_specs=stream_specs + [_layer_resident(layer, (1, d)), _resident(w.shape)],
        out_specs=[row_spec(d_attn), row_spec(d_attn), row_spec(d_attn), row_spec(d_pool), row_spec(d_conv)],
        out_shape=[jax.ShapeDtypeStruct((rows, d_attn), BF16)] * 3
        + [jax.ShapeDtypeStruct((rows, d_pool), F32), jax.ShapeDtypeStruct((rows, d_conv), F32)],
        compiler_params=_params(1),
        name="in_proj",
    )(*stream, g, w)


def _attn_kernel(q_ref, k_ref, v_ref, tri_ref, g_ref, o_ref, q2_ref, acc_ref, c_ref, *, n_pairs):
    t = SEQ_TILE
    qi = pl.program_id(1)
    pairs = range(n_pairs)
    lanes = [slice(p * LANES, (p + 1) * LANES) for p in pairs]
    first_head = lax.broadcasted_iota(jnp.int32, (t, LANES), 1) < LANES // 2
    row = lax.broadcasted_iota(jnp.int32, (2 * t, t), 0)
    col = lax.broadcasted_iota(jnp.int32, (2 * t, t), 1)
    causal = col < (row & (t - 1))
    zero = jnp.zeros((), BF16)
    for p in pairs:
        qp = q_ref[:, lanes[p]]
        q2_ref[p] = jnp.concatenate([jnp.where(first_head, qp, zero), jnp.where(first_head, zero, qp)], axis=0)

    def head_rows(ref, p, r):
        return ref[p] if r == t else jnp.concatenate([ref[p, :r], ref[p, t:t + r]], axis=0)

    def key_tile(kb, r, diagonal=False):
        start = pl.multiple_of(kb * t, t)
        z = [lax.dot_general(head_rows(q2_ref, p, r), k_ref[pl.ds(start, t), lanes[p]], (((1,), (1,)), ((), ())),
                             preferred_element_type=F32) for p in pairs]
        v = [v_ref[pl.ds(start, t), lanes[p]] for p in pairs]
        c_old = None if diagonal else [head_rows(c_ref, p, r) for p in pairs]
        log_beta, later, c_new = [], [], []
        for p in pairs:
            z_pos = jnp.maximum(z[p], 0.0)
            z_neg = z[p] - z_pos
            log_term = jnp.log(1.0 + jnp.exp2(z_neg - z_pos)) * LOG2_E
            soft = z_pos + log_term
            if diagonal:
                soft = jnp.where(causal, soft, 0.0)
            later.append(jnp.dot(soft.astype(BF16), tri_ref[...], preferred_element_type=F32))
            log_beta.append(z_neg - log_term)
            c = jnp.sum(soft, axis=1, keepdims=True)
            c_new.append(jnp.broadcast_to(c, soft.shape) if diagonal else c_old[p] + c)
        o = []
        for p in pairs:
            x = log_beta[p] - later[p]
            if not diagonal:
                x = x - c_old[p]
            w = jnp.exp2(x)
            if diagonal:
                w = jnp.where(causal, w, 0.0)
            o.append(jnp.dot(w.astype(BF16), v[p], preferred_element_type=F32))
        c_min = None
        for p in pairs:
            first = lax.broadcasted_iota(jnp.int32, (r, LANES), 1) < LANES // 2
            o_pair = jnp.where(first, o[p][:r], o[p][r:])
            if diagonal:
                acc_ref[:, lanes[p]] = o_pair
            else:
                acc_ref[:r, lanes[p]] += o_pair
            if r == t:
                c_ref[p] = c_new[p]
            else:
                c_ref[p, :r] = c_new[p][:r]
                c_ref[p, t:t + r] = c_new[p][r:]
            c_min = c_new[p] if c_min is None else jnp.minimum(c_min, c_new[p])
        if r < t:
            return jnp.min(c_min)
        top = jnp.minimum(jnp.min(c_min[:TOP_ROWS]), jnp.min(c_min[t:t + TOP_ROWS]))
        rest = jnp.minimum(jnp.min(c_min[TOP_ROWS:t]), jnp.min(c_min[t + TOP_ROWS:]))
        return top, rest

    def unfinished(c_min):
        return c_min < -LOG2_WEIGHT_FLOOR

    def full_body(s):
        kb, _, _ = s
        return (kb - 1,) + key_tile(kb, t)

    def top_body(s):
        kb, _ = s
        return kb - 1, key_tile(kb, TOP_ROWS)

    kb, top, _ = lax.while_loop(lambda s: jnp.logical_and(s[0] >= 0, unfinished(s[2])), full_body,
                                (qi - 1,) + key_tile(qi, t, diagonal=True))
    lax.while_loop(lambda s: jnp.logical_and(s[0] >= 0, unfinished(s[1])), top_body, (kb, top))
    o_ref[...] = _rms(acc_ref[...], g_ref[...]).astype(BF16)


def _attention(layer, q, k, v, g, batch, rows_per_batch):
    rows, d_attn = q.shape
    t = SEQ_TILE
    nq = rows_per_batch // t
    j = jnp.arange(t)
    tri = (j[:, None] > j[None, :]).astype(BF16)
    n_pairs = d_attn // LANES
    kv_spec = pl.BlockSpec((rows_per_batch, d_attn), lambda b, i: (b, 0), pipeline_mode=pl.Buffered(1))
    tile_spec = pl.BlockSpec((t, d_attn), lambda b, i: (b * nq + i, 0))
    return pl.pallas_call(
        functools.partial(_attn_kernel, n_pairs=n_pairs),
        grid=(batch, nq),
        in_specs=[tile_spec, kv_spec, kv_spec, _resident(tri.shape), _layer_resident(layer, (1, d_attn))],
        out_specs=tile_spec,
        out_shape=jax.ShapeDtypeStruct((rows, d_attn), BF16),
        scratch_shapes=[pltpu.VMEM((n_pairs, 2 * t, LANES), BF16), pltpu.VMEM((t, d_attn), F32),
                        pltpu.VMEM((n_pairs, 2 * t, t), F32)],
        compiler_params=_params(2),
        name="sb_attention",
    )(q, k, v, tri, g)


def _mixers_kernel(*refs, d_conv, n_casts):
    (pool_ref, pool_halo_ref, conv_ref, conv_halo_ref, w_pool_ref, pool_scale_ref, g_pool_ref,
     w_dw_ref, b_dw_ref, ln_g_ref, ln_b_ref, w_pw_ref, g_conv_ref) = refs[:13]
    cast_src = refs[13:13 + n_casts]
    o_pool_ref, o_conv_ref = refs[13 + n_casts:15 + n_casts]
    cast_dst = refs[15 + n_casts:15 + 2 * n_casts]
    pool_ext, conv_ext, y_ref = refs[15 + 2 * n_casts:]
    for src, dst in zip(cast_src, cast_dst):
        dst[...] = src[...].astype(BF16)
    tl = pool_ref.shape[0]
    ti = pl.program_id(1)
    has_history = ti > 0

    pool_ext[:POOL_HALO, :] = jnp.where(has_history, pool_halo_ref[...], 0.0)
    pool_ext[POOL_HALO:, :] = pool_ref[...]
    pos = ti * tl + lax.broadcasted_iota(jnp.int32, (tl, 1), 0)
    mixed = []
    sq = jnp.zeros((tl, 1), F32)
    for gi, window in enumerate(POOL_WINDOWS):
        lanes = slice(gi * LANES, (gi + 1) * LANES)
        u = pool_ext[POOL_HALO:, lanes]
        acc = u
        for back in range(1, window):
            acc = acc + pool_ext[POOL_HALO - back:POOL_HALO - back + tl, lanes]
        count = jnp.minimum(pos + 1, window).astype(F32)
        pooled = acc / count - u
        m = jnp.dot(pooled.astype(BF16), w_pool_ref[gi], preferred_element_type=F32) * pool_scale_ref[:, lanes]
        sq = sq + jnp.sum(m * m, axis=-1, keepdims=True)
        mixed.append(m)
    d_pool = LANES * len(POOL_WINDOWS)
    inv = lax.rsqrt(sq / d_pool + EPS)
    for gi, m in enumerate(mixed):
        lanes = slice(gi * LANES, (gi + 1) * LANES)
        o_pool_ref[:, lanes] = (m * inv * g_pool_ref[:, lanes]).astype(BF16)

    n_ext = CONV_HALO + tl
    conv_ext[0, :CONV_HALO, :] = jnp.where(has_history, conv_halo_ref[...], 0.0)
    conv_ext[0, CONV_HALO:, :] = conv_ref[...]
    for r in range(1, SUBLANES):
        conv_ext[r, :n_ext - SUBLANES, :] = conv_ext[0, r:r + n_ext - SUBLANES, :]
    first_tap = CONV_HALO - (CONV_WIDTH - 1)

    for base in range(0, tl, CONV_CHUNK):
        acc = jnp.broadcast_to(b_dw_ref[...], (CONV_CHUNK, d_conv))
        for tap in range(CONV_WIDTH):
            shift = (first_tap + tap) % SUBLANES
            lo = base + first_tap + tap - shift
            acc = acc + w_dw_ref[tap:tap + 1, :] * conv_ext[shift, lo:lo + CONV_CHUNK, :]
        y_ref[base:base + CONV_CHUNK, :] = acc
    y = y_ref[...]
    mu = jnp.mean(y, axis=-1, keepdims=True)
    yc = y - mu
    var = jnp.mean(yc * yc, axis=-1, keepdims=True)
    y = yc * lax.rsqrt(var + EPS) * ln_g_ref[...] + ln_b_ref[...]
    y = y * jax.nn.sigmoid(y)
    o = jnp.dot(y.astype(BF16), w_pw_ref[...], preferred_element_type=F32)
    o_conv_ref[...] = _rms(o, g_conv_ref[...]).astype(BF16)


def _cast_specs(w, layer, axis, step_of, n_steps):
    k, n = w.shape[1:]
    n_blocks = w.shape[axis] // CAST_BLOCK
    assert w.shape[axis] % CAST_BLOCK == 0 and n_blocks <= n_steps
    block = (CAST_BLOCK, n) if axis == 1 else (k, CAST_BLOCK)

    def index(*grid_ids):
        j = jnp.minimum(step_of(*grid_ids), n_blocks - 1)
        return (j, 0) if axis == 1 else (0, j)

    return (pl.BlockSpec((None,) + block, lambda *g: (layer,) + index(*g)), pl.BlockSpec(block, index),
            jax.ShapeDtypeStruct((k, n), BF16))


def _mixers(layer, u_pool, u_conv, w_pool, pool_scale, g_pool, w_dw, b_dw, ln_g, ln_b, w_pw, g_conv, batch,
            rows_per_batch, casts):
    rows, d_pool = u_pool.shape
    d_conv = u_conv.shape[1]
    tl = MIX_ROWS
    nt = rows_per_batch // tl

    def cur(n):
        return pl.BlockSpec((tl, n), lambda b, i: (b * nt + i, 0))

    def halo(n, h):
        return pl.BlockSpec((h, n), lambda b, i: (jnp.maximum((b * nt + i) * (tl // h) - 1, 0), 0))

    lr = functools.partial(_layer_resident, layer)
    cast_specs = [_cast_specs(w, l, axis, lambda b, i: b * nt + i, batch * nt) for w, l, axis in casts]
    return pl.pallas_call(
        functools.partial(_mixers_kernel, d_conv=d_conv, n_casts=len(casts)),
        grid=(batch, nt),
        in_specs=[cur(d_pool), halo(d_pool, POOL_HALO), cur(d_conv), halo(d_conv, CONV_HALO),
                  lr(w_pool.shape[1:]), lr((1, d_pool)), lr((1, d_pool)),
                  lr(w_dw.shape[1:]), lr((1, d_conv)), lr((1, d_conv)), lr((1, d_conv)),
                  lr(w_pw.shape[1:]), lr((1, d_conv))] + [c[0] for c in cast_specs],
        out_specs=[cur(d_pool), cur(d_conv)] + [c[1] for c in cast_specs],
        out_shape=[jax.ShapeDtypeStruct((rows, d_pool), BF16), jax.ShapeDtypeStruct((rows, d_conv), BF16)]
        + [c[2] for c in cast_specs],
        scratch_shapes=[pltpu.VMEM((POOL_HALO + tl, d_pool), F32), pltpu.VMEM((SUBLANES, CONV_HALO + tl, d_conv), F32),
                        pltpu.VMEM((tl, d_conv), F32)],
        compiler_params=_params(2),
        name="pool_conv_mixers",
    )(u_pool, u_pool, u_conv, u_conv, w_pool, pool_scale, g_pool, w_dw, b_dw, ln_g, ln_b, w_pw, g_conv,
      *[w for w, _, _ in casts])


def _out_proj_kernel(*refs, n_stream, layout):
    attn_ref, pool_ref, conv_ref, w_ref, g_ref, o_ref = refs[n_stream:]
    d_attn, d_pool = attn_ref.shape[1], pool_ref.shape[1]
    y = jnp.dot(attn_ref[...], w_ref[:d_attn, :], preferred_element_type=F32)
    y = y + jnp.dot(pool_ref[...], w_ref[d_attn:d_attn + d_pool, :], preferred_element_type=F32)
    y = y + jnp.dot(conv_ref[...], w_ref[d_attn + d_pool:, :], preferred_element_type=F32)
    o_ref[...] = _stream_tile(refs[:n_stream], layout) + _rms(y, g_ref[...])


def _out_proj(layer, attn, pool, conv, h, w, g):
    rows, d = h.shape
    tm = PROJ_ROWS
    row_spec = lambda n: pl.BlockSpec((tm, n), lambda i: (i, 0))
    stream, stream_specs, layout = _stream_inputs(h, tm)
    return pl.pallas_call(
        functools.partial(_out_proj_kernel, n_stream=len(stream), layout=layout),
        grid=(rows // tm,),
        in_specs=stream_specs + [row_spec(attn.shape[1]), row_spec(pool.shape[1]), row_spec(conv.shape[1]),
                                 _resident(w.shape), _layer_resident(layer, (1, d))],
        out_specs=row_spec(d),
        out_shape=jax.ShapeDtypeStruct((rows, d), F32),
        compiler_params=_params(1),
        name="out_proj",
    )(*stream, attn, pool, conv, w, g)


def _ffn_kernel(h_ref, g_pre_ref, w_gate_ref, w_up_ref, w_down_ref, g_post_ref, o_ref, u_ref, acc_ref):
    c = pl.program_id(1)

    @pl.when(c == 0)
    def _():
        u_ref[...] = _rms(h_ref[...], g_pre_ref[...]).astype(BF16)
        acc_ref[...] = jnp.zeros_like(acc_ref)

    u = u_ref[...]
    gate = jnp.dot(u, w_gate_ref[...], preferred_element_type=F32)
    up = jnp.dot(u, w_up_ref[...], preferred_element_type=F32)
    a = (gate * jax.nn.sigmoid(gate) * up).astype(BF16)
    tf = w_down_ref.shape[0]
    for n in range(0, acc_ref.shape[1], tf):
        acc_ref[:, n:n + tf] += jnp.dot(a, w_down_ref[:, n:n + tf], preferred_element_type=F32)

    @pl.when(c == pl.num_programs(1) - 1)
    def _():
        o_ref[...] = h_ref[...] + _rms(acc_ref[...], g_post_ref[...])


def _ffn(layer, h, g_pre, w_gate, w_up, w_down, g_post, keep=None):
    rows, d = h.shape
    d_ff = w_gate.shape[1]
    tf = FFN_COLS
    if keep is None:
        tm, n_tiles = FFN_ROWS, rows // FFN_ROWS
        h_spec = pl.BlockSpec((tm, d), lambda i, c: (i, 0))
    else:
        rows_per_batch, first, count = keep
        tm = LAST_FFN_ROWS
        per_batch = count // tm
        n_tiles = (rows // rows_per_batch) * per_batch
        h_spec = pl.BlockSpec((pl.Element(tm), pl.Element(d)),
                              lambda i, c: (pl.multiple_of(
                                  (i // per_batch) * rows_per_batch + first + (i % per_batch) * tm, SUBLANES), 0))
        assert rows_per_batch % SUBLANES == 0 and first % SUBLANES == 0
    return pl.pallas_call(
        _ffn_kernel,
        grid=(n_tiles, d_ff // tf),
        in_specs=[h_spec, _layer_resident(layer, (1, d)),
                  pl.BlockSpec((d, tf), lambda i, c: (0, c)), pl.BlockSpec((d, tf), lambda i, c: (0, c)),
                  pl.BlockSpec((tf, d), lambda i, c: (c, 0)), _layer_resident(layer, (1, d))],
        out_specs=pl.BlockSpec((tm, d), lambda i, c: (i, 0)),
        out_shape=jax.ShapeDtypeStruct((n_tiles * tm, d), F32),
        scratch_shapes=[pltpu.VMEM((tm, d), BF16), pltpu.VMEM((tm, d), F32)],
        compiler_params=_params(2),
        name="swiglu_ffn",
    )(h, g_pre, w_gate, w_up, w_down, g_post)


def kernel(x, meta_tokens, pre_mix_g, w_in, w_pool, pool_scale, w_dw, b_dw, conv_ln_g, conv_ln_b, w_pw, mix_out_g,
           w_out, post_mix_g, pre_ffn_g, w_gate, w_up, w_down, post_ffn_g):
    batch, seq, d = x.shape
    depth = w_in.shape[0]
    d_pool = pool_scale.shape[1]
    d_conv = w_pw.shape[1]
    d_attn = w_out.shape[1] - d_pool - d_conv
    assert d_pool == LANES * len(POOL_WINDOWS) and d_attn % (2 * LANES) == 0
    assert d_attn // N_HEADS_SB == LANES // 2

    length = N_META + seq
    rows_per_batch = -(-length // ROW_ALIGN) * ROW_ALIGN
    assert rows_per_batch % SEQ_TILE == 0 and (batch * rows_per_batch) % FFN_ROWS == 0 and seq % LAST_FFN_ROWS == 0
    assert meta_tokens.shape[0] == N_META
    h = _Tokens(x.reshape(batch * seq, d), meta_tokens.astype(x.dtype), seq, rows_per_batch)

    w_pool, w_pw = w_pool.astype(BF16), w_pw.astype(BF16)
    w_in_l = w_in[0].astype(BF16)
    vec = lambda a: a[:, None, :]
    pre_mix_g, pool_scale, b_dw, conv_ln_g, conv_ln_b, post_mix_g, pre_ffn_g, post_ffn_g = (
        vec(a) for a in (pre_mix_g, pool_scale, b_dw, conv_ln_g, conv_ln_b, post_mix_g, pre_ffn_g, post_ffn_g))
    g_attn, g_pool, g_conv = (vec(a) for a in jnp.split(mix_out_g, [d_attn, d_attn + d_pool], axis=1))

    for i in range(depth):
        q, k, v, u_pool, u_conv = _in_proj(i, h, pre_mix_g, w_in_l, d_attn, d_pool, d_conv)
        o_attn = _attention(i, q, k, v, g_attn, batch, rows_per_batch)
        casts = [(w_out, i, 2), (w_gate, i, 2), (w_up, i, 2), (w_down, i, 1)]
        if i + 1 < depth:
            casts.append((w_in, i + 1, 2))
        o_pool, o_conv, w_out_l, w_gate_l, w_up_l, w_down_l, *w_in_next = _mixers(
            i, u_pool, u_conv, w_pool, pool_scale, g_pool, w_dw, b_dw, conv_ln_g, conv_ln_b, w_pw, g_conv, batch,
            rows_per_batch, casts)
        if w_in_next:
            w_in_l = w_in_next[0]
        h = _out_proj(i, o_attn, o_pool, o_conv, h, w_out_l, post_mix_g)
        keep = (rows_per_batch, N_META, seq) if i == depth - 1 else None
        h = _ffn(i, h, pre_ffn_g, w_gate_l, w_up_l, w_down_l, post_ffn_g, keep)

    return h.reshape(batch, seq, d)
```

```python
import functools
from typing import NamedTuple

import jax
import jax.numpy as jnp
from jax import lax
from jax.experimental import pallas as pl
from jax.experimental.pallas import tpu as pltpu

N_META = 16
N_HEADS_SB = 16
POOL_WINDOWS = (2, 4, 8, 16)
CONV_WIDTH = 31
EPS = 1e-6

LANES = 128
SUBLANES = 8
VMEM_BYTES_V7X = 64 * 1024 * 1024
VMEM_LIMIT = VMEM_BYTES_V7X - 8 * 1024 * 1024

SEQ_TILE = 128
TOP_ROWS = 32
ROW_ALIGN = 384
PROJ_ROWS = 384
MIX_ROWS = 384
FFN_ROWS = 768
LAST_FFN_ROWS = 512
FFN_COLS = 512
CONV_HALO = 32
POOL_HALO = 16
CONV_CHUNK = 32
MIX_CAST_BLOCK = 256
ATTN_CAST_BLOCK = 128
PROJ_CAST_BLOCK = 256

LOG2_E = 1.4426950408889634
LOG2_WEIGHT_FLOOR = -127.0

F32 = jnp.float32
BF16 = jnp.bfloat16


def _rms(x, g):
    return x * lax.rsqrt(jnp.mean(x * x, axis=-1, keepdims=True) + EPS) * g


def _resident(shape):
    return pl.BlockSpec(shape, lambda *_: (0,) * len(shape), pipeline_mode=pl.Buffered(1))


def _layer_resident(layer, shape):
    return pl.BlockSpec((None,) + tuple(shape), lambda *_: (layer,) + (0,) * len(shape),
                        pipeline_mode=pl.Buffered(1))


def _params(n_grid_axes):
    return pltpu.CompilerParams(dimension_semantics=("arbitrary",) * n_grid_axes, vmem_limit_bytes=VMEM_LIMIT)


class _Tokens(NamedTuple):
    x: jax.Array
    meta: jax.Array
    seq: int
    rows_per_batch: int

    @property
    def shape(self):
        return (self.x.shape[0] // self.seq * self.rows_per_batch, self.x.shape[1])


def _stream_inputs(h, tm):
    if not isinstance(h, _Tokens):
        return [h], [pl.BlockSpec((tm, h.shape[1]), lambda i, *_: (i, 0))], None
    seq, d = h.seq, h.x.shape[1]
    n_meta = h.meta.shape[0]
    tiles_per_batch = h.rows_per_batch // tm
    n_pad = h.rows_per_batch - n_meta - seq
    assert h.rows_per_batch % tm == 0 and n_pad <= tm and n_meta <= tm
    assert all(n % SUBLANES == 0 for n in (n_meta, n_pad, seq, tm))

    def start(i):
        b, t = i // tiles_per_batch, i % tiles_per_batch
        return pl.multiple_of(b * seq + jnp.clip(t * tm - n_meta, 0, seq - tm), SUBLANES)

    spec = pl.BlockSpec((pl.Element(tm), pl.Element(d)), lambda i, *_: (start(i), 0))
    return [h.x, h.meta], [spec, _resident(h.meta.shape)], (tiles_per_batch, n_pad)


def _stream_tile(stream_refs, layout):
    if layout is None:
        return stream_refs[0][...]
    x_ref, meta_ref = stream_refs
    tiles_per_batch, n_pad = layout
    blk = x_ref[...]
    tm, d = blk.shape
    t = lax.rem(pl.program_id(0), tiles_per_batch)
    first = jnp.concatenate([meta_ref[...], blk[:tm - meta_ref.shape[0]]], axis=0)
    last = jnp.concatenate([blk[n_pad:], jnp.zeros((n_pad, d), blk.dtype)], axis=0)
    return jnp.where(t == 0, first, jnp.where(t == tiles_per_batch - 1, last, blk))


def _in_proj_kernel(*refs, n_stream, layout, n_casts, d_attn, d_pool, q_scale):
    g_ref, w_ref = refs[n_stream:n_stream + 2]
    q_ref, k_ref, v_ref, pool_ref, conv_ref = refs[n_stream + 2 + n_casts:n_stream + 7 + n_casts]
    _cast_blocks(refs[n_stream + 2:n_stream + 2 + n_casts], refs[n_stream + 7 + n_casts:])
    u = _rms(_stream_tile(refs[:n_stream], layout), g_ref[...]).astype(BF16)
    o = 0
    q_ref[...] = (jnp.dot(u, w_ref[:, o:o + d_attn], preferred_element_type=F32) * q_scale).astype(BF16)
    o += d_attn
    k_ref[...] = jnp.dot(u, w_ref[:, o:o + d_attn], preferred_element_type=F32).astype(BF16)
    o += d_attn
    v_ref[...] = jnp.dot(u, w_ref[:, o:o + d_attn], preferred_element_type=F32).astype(BF16)
    o += d_attn
    pool_ref[...] = jnp.dot(u, w_ref[:, o:o + d_pool], preferred_element_type=F32)
    o += d_pool
    d_conv = conv_ref.shape[1]
    conv_ref[...] = (jnp.dot(u, w_ref[:, o:o + d_conv], preferred_element_type=F32)
                     * jax.nn.sigmoid(jnp.dot(u, w_ref[:, o + d_conv:], preferred_element_type=F32)))


def _in_proj(layer, h, g, w, d_attn, d_pool, d_conv, casts):
    rows, d = h.shape
    tm = PROJ_ROWS
    row_spec = lambda n: pl.BlockSpec((tm, n), lambda i: (i, 0))
    head_dim = d_attn // N_HEADS_SB
    stream, stream_specs, layout = _stream_inputs(h, tm)
    cast_specs = [_cast_specs(w_, l, axis, PROJ_CAST_BLOCK, lambda i: i, rows // tm) for w_, l, axis in casts]
    return pl.pallas_call(
        functools.partial(_in_proj_kernel, n_stream=len(stream), layout=layout, n_casts=len(casts), d_attn=d_attn,
                          d_pool=d_pool, q_scale=LOG2_E * head_dim ** -0.5),
        grid=(rows // tm,),
        in_specs=stream_specs + [_layer_resident(layer, (1, d)), _resident(w.shape)] + [c[0] for c in cast_specs],
        out_specs=[row_spec(d_attn), row_spec(d_attn), row_spec(d_attn), row_spec(d_pool), row_spec(d_conv)]
        + [c[1] for c in cast_specs],
        out_shape=[jax.ShapeDtypeStruct((rows, d_attn), BF16)] * 3
        + [jax.ShapeDtypeStruct((rows, d_pool), F32), jax.ShapeDtypeStruct((rows, d_conv), F32)]
        + [c[2] for c in cast_specs],
        compiler_params=_params(1),
        name="in_proj",
    )(*stream, g, w, *[w_ for w_, _, _ in casts])


def _cast_specs(w, layer, axis, block_size, step_of, n_steps):
    k, n = w.shape[1:]
    n_blocks = w.shape[axis] // block_size
    assert w.shape[axis] % block_size == 0 and n_blocks <= n_steps
    block = (block_size, n) if axis == 1 else (k, block_size)

    def index(*grid_ids):
        j = jnp.minimum(step_of(*grid_ids), n_blocks - 1)
        return (j, 0) if axis == 1 else (0, j)

    return (pl.BlockSpec((None,) + block, lambda *g: (layer,) + index(*g)), pl.BlockSpec(block, index),
            jax.ShapeDtypeStruct((k, n), BF16))


def _cast_blocks(cast_src, cast_dst):
    for src, dst in zip(cast_src, cast_dst):
        dst[...] = src[...].astype(BF16)


def _attn_kernel(*refs, n_pairs, n_casts):
    q_ref, k_ref, v_ref, tri_ref, g_ref = refs[:5]
    o_ref = refs[5 + n_casts]
    q2_ref, acc_ref, c_ref = refs[6 + 2 * n_casts:]
    _cast_blocks(refs[5:5 + n_casts], refs[6 + n_casts:6 + 2 * n_casts])
    t = SEQ_TILE
    qi = pl.program_id(1)
    pairs = range(n_pairs)
    lanes = [slice(p * LANES, (p + 1) * LANES) for p in pairs]
    first_head = lax.broadcasted_iota(jnp.int32, (t, LANES), 1) < LANES // 2
    row = lax.broadcasted_iota(jnp.int32, (2 * t, t), 0)
    col = lax.broadcasted_iota(jnp.int32, (2 * t, t), 1)
    causal = col < (row & (t - 1))
    zero = jnp.zeros((), BF16)
    for p in pairs:
        qp = q_ref[:, lanes[p]]
        q2_ref[p] = jnp.concatenate([jnp.where(first_head, qp, zero), jnp.where(first_head, zero, qp)], axis=0)

    def head_rows(ref, p, r):
        return ref[p] if r == t else jnp.concatenate([ref[p, :r], ref[p, t:t + r]], axis=0)

    def key_tile(kb, r, diagonal=False):
        start = pl.multiple_of(kb * t, t)
        z = [lax.dot_general(head_rows(q2_ref, p, r), k_ref[pl.ds(start, t), lanes[p]], (((1,), (1,)), ((), ())),
                             preferred_element_type=F32) for p in pairs]
        v = [v_ref[pl.ds(start, t), lanes[p]] for p in pairs]
        c_old = None if diagonal else [head_rows(c_ref, p, r) for p in pairs]
        log_beta, later, c_new = [], [], []
        for p in pairs:
            z_pos = jnp.maximum(z[p], 0.0)
            z_neg = z[p] - z_pos
            log_term = jnp.log(1.0 + jnp.exp2(z_neg - z_pos)) * LOG2_E
            soft = z_pos + log_term
            if diagonal:
                soft = jnp.where(causal, soft, 0.0)
            later.append(jnp.dot(soft.astype(BF16), tri_ref[...], preferred_element_type=F32))
            log_beta.append(z_neg - log_term)
            c = jnp.sum(soft, axis=1, keepdims=True)
            c_new.append(jnp.broadcast_to(c, soft.shape) if diagonal else c_old[p] + c)
        o = []
        for p in pairs:
            x = log_beta[p] - later[p]
            if not diagonal:
                x = x - c_old[p]
            w = jnp.exp2(x)
            if diagonal:
                w = jnp.where(causal, w, 0.0)
            o.append(jnp.dot(w.astype(BF16), v[p], preferred_element_type=F32))
        c_min = None
        for p in pairs:
            first = lax.broadcasted_iota(jnp.int32, (r, LANES), 1) < LANES // 2
            o_pair = jnp.where(first, o[p][:r], o[p][r:])
            if diagonal:
                acc_ref[:, lanes[p]] = o_pair
            else:
                acc_ref[:r, lanes[p]] += o_pair
            if r == t:
                c_ref[p] = c_new[p]
            else:
                c_ref[p, :r] = c_new[p][:r]
                c_ref[p, t:t + r] = c_new[p][r:]
            c_min = c_new[p] if c_min is None else jnp.minimum(c_min, c_new[p])
        if r < t:
            return jnp.min(c_min)
        top = jnp.minimum(jnp.min(c_min[:TOP_ROWS]), jnp.min(c_min[t:t + TOP_ROWS]))
        rest = jnp.minimum(jnp.min(c_min[TOP_ROWS:t]), jnp.min(c_min[t + TOP_ROWS:]))
        return top, rest

    def unfinished(c_min):
        return c_min < -LOG2_WEIGHT_FLOOR

    def full_body(s):
        kb, _, _ = s
        return (kb - 1,) + key_tile(kb, t)

    def top_body(s):
        kb, _ = s
        return kb - 1, key_tile(kb, TOP_ROWS)

    kb, top, _ = lax.while_loop(lambda s: jnp.logical_and(s[0] >= 0, unfinished(s[2])), full_body,
                                (qi - 1,) + key_tile(qi, t, diagonal=True))
    lax.while_loop(lambda s: jnp.logical_and(s[0] >= 0, unfinished(s[1])), top_body, (kb, top))
    o_ref[...] = _rms(acc_ref[...], g_ref[...]).astype(BF16)


def _attention(layer, q, k, v, g, batch, rows_per_batch, casts):
    rows, d_attn = q.shape
    t = SEQ_TILE
    nq = rows_per_batch // t
    j = jnp.arange(t)
    tri = (j[:, None] > j[None, :]).astype(BF16)
    n_pairs = d_attn // LANES
    kv_spec = pl.BlockSpec((rows_per_batch, d_attn), lambda b, i: (b, 0), pipeline_mode=pl.Buffered(1))
    tile_spec = pl.BlockSpec((t, d_attn), lambda b, i: (b * nq + i, 0))
    cast_specs = [_cast_specs(w, l, axis, ATTN_CAST_BLOCK, lambda b, i: b * nq + i, batch * nq) for w, l, axis in casts]
    return pl.pallas_call(
        functools.partial(_attn_kernel, n_pairs=n_pairs, n_casts=len(casts)),
        grid=(batch, nq),
        in_specs=[tile_spec, kv_spec, kv_spec, _resident(tri.shape), _layer_resident(layer, (1, d_attn))]
        + [c[0] for c in cast_specs],
        out_specs=[tile_spec] + [c[1] for c in cast_specs],
        out_shape=[jax.ShapeDtypeStruct((rows, d_attn), BF16)] + [c[2] for c in cast_specs],
        scratch_shapes=[pltpu.VMEM((n_pairs, 2 * t, LANES), BF16), pltpu.VMEM((t, d_attn), F32),
                        pltpu.VMEM((n_pairs, 2 * t, t), F32)],
        compiler_params=_params(2),
        name="sb_attention",
    )(q, k, v, tri, g, *[w for w, _, _ in casts])


def _mixers_kernel(*refs, d_conv, n_casts):
    (pool_ref, pool_halo_ref, conv_ref, conv_halo_ref, w_pool_ref, pool_scale_ref, g_pool_ref,
     w_dw_ref, b_dw_ref, ln_g_ref, ln_b_ref, w_pw_ref, g_conv_ref) = refs[:13]
    cast_src = refs[13:13 + n_casts]
    o_pool_ref, o_conv_ref = refs[13 + n_casts:15 + n_casts]
    cast_dst = refs[15 + n_casts:15 + 2 * n_casts]
    pool_ext, conv_ext, y_ref = refs[15 + 2 * n_casts:]
    _cast_blocks(cast_src, cast_dst)
    tl = pool_ref.shape[0]
    ti = pl.program_id(1)
    has_history = ti > 0

    pool_ext[:POOL_HALO, :] = jnp.where(has_history, pool_halo_ref[...], 0.0)
    pool_ext[POOL_HALO:, :] = pool_ref[...]
    pos = ti * tl + lax.broadcasted_iota(jnp.int32, (tl, 1), 0)
    mixed = []
    sq = jnp.zeros((tl, 1), F32)
    for gi, window in enumerate(POOL_WINDOWS):
        lanes = slice(gi * LANES, (gi + 1) * LANES)
        u = pool_ext[POOL_HALO:, lanes]
        acc = u
        for back in range(1, window):
            acc = acc + pool_ext[POOL_HALO - back:POOL_HALO - back + tl, lanes]
        count = jnp.minimum(pos + 1, window).astype(F32)
        pooled = acc / count - u
        m = jnp.dot(pooled.astype(BF16), w_pool_ref[gi], preferred_element_type=F32) * pool_scale_ref[:, lanes]
        sq = sq + jnp.sum(m * m, axis=-1, keepdims=True)
        mixed.append(m)
    d_pool = LANES * len(POOL_WINDOWS)
    inv = lax.rsqrt(sq / d_pool + EPS)
    for gi, m in enumerate(mixed):
        lanes = slice(gi * LANES, (gi + 1) * LANES)
        o_pool_ref[:, lanes] = (m * inv * g_pool_ref[:, lanes]).astype(BF16)

    n_ext = CONV_HALO + tl
    conv_ext[0, :CONV_HALO, :] = jnp.where(has_history, conv_halo_ref[...], 0.0)
    conv_ext[0, CONV_HALO:, :] = conv_ref[...]
    for r in range(1, SUBLANES):
        conv_ext[r, :n_ext - SUBLANES, :] = conv_ext[0, r:r + n_ext - SUBLANES, :]
    first_tap = CONV_HALO - (CONV_WIDTH - 1)

    for base in range(0, tl, CONV_CHUNK):
        acc = jnp.broadcast_to(b_dw_ref[...], (CONV_CHUNK, d_conv))
        for tap in range(CONV_WIDTH):
            shift = (first_tap + tap) % SUBLANES
            lo = base + first_tap + tap - shift
            acc = acc + w_dw_ref[tap:tap + 1, :] * conv_ext[shift, lo:lo + CONV_CHUNK, :]
        y_ref[base:base + CONV_CHUNK, :] = acc
    y = y_ref[...]
    mu = jnp.mean(y, axis=-1, keepdims=True)
    yc = y - mu
    var = jnp.mean(yc * yc, axis=-1, keepdims=True)
    y = yc * lax.rsqrt(var + EPS) * ln_g_ref[...] + ln_b_ref[...]
    y = y * jax.nn.sigmoid(y)
    o = jnp.dot(y.astype(BF16), w_pw_ref[...], preferred_element_type=F32)
    o_conv_ref[...] = _rms(o, g_conv_ref[...]).astype(BF16)


def _mixers(layer, u_pool, u_conv, w_pool, pool_scale, g_pool, w_dw, b_dw, ln_g, ln_b, w_pw, g_conv, batch,
            rows_per_batch, casts):
    rows, d_pool = u_pool.shape
    d_conv = u_conv.shape[1]
    tl = MIX_ROWS
    nt = rows_per_batch // tl

    def cur(n):
        return pl.BlockSpec((tl, n), lambda b, i: (b * nt + i, 0))

    def halo(n, h):
        return pl.BlockSpec((h, n), lambda b, i: (jnp.maximum((b * nt + i) * (tl // h) - 1, 0), 0))

    lr = functools.partial(_layer_resident, layer)
    cast_specs = [_cast_specs(w, l, axis, MIX_CAST_BLOCK, lambda b, i: b * nt + i, batch * nt) for w, l, axis in casts]
    return pl.pallas_call(
        functools.partial(_mixers_kernel, d_conv=d_conv, n_casts=len(casts)),
        grid=(batch, nt),
        in_specs=[cur(d_pool), halo(d_pool, POOL_HALO), cur(d_conv), halo(d_conv, CONV_HALO),
                  lr(w_pool.shape[1:]), lr((1, d_pool)), lr((1, d_pool)),
                  lr(w_dw.shape[1:]), lr((1, d_conv)), lr((1, d_conv)), lr((1, d_conv)),
                  lr(w_pw.shape[1:]), lr((1, d_conv))] + [c[0] for c in cast_specs],
        out_specs=[cur(d_pool), cur(d_conv)] + [c[1] for c in cast_specs],
        out_shape=[jax.ShapeDtypeStruct((rows, d_pool), BF16), jax.ShapeDtypeStruct((rows, d_conv), BF16)]
        + [c[2] for c in cast_specs],
        scratch_shapes=[pltpu.VMEM((POOL_HALO + tl, d_pool), F32), pltpu.VMEM((SUBLANES, CONV_HALO + tl, d_conv), F32),
                        pltpu.VMEM((tl, d_conv), F32)],
        compiler_params=_params(2),
        name="pool_conv_mixers",
    )(u_pool, u_pool, u_conv, u_conv, w_pool, pool_scale, g_pool, w_dw, b_dw, ln_g, ln_b, w_pw, g_conv,
      *[w for w, _, _ in casts])


def _out_proj_kernel(*refs, n_stream, layout):
    attn_ref, pool_ref, conv_ref, w_ref, g_ref, o_ref = refs[n_stream:]
    d_attn, d_pool = attn_ref.shape[1], pool_ref.shape[1]
    y = jnp.dot(attn_ref[...], w_ref[:d_attn, :], preferred_element_type=F32)
    y = y + jnp.dot(pool_ref[...], w_ref[d_attn:d_attn + d_pool, :], preferred_element_type=F32)
    y = y + jnp.dot(conv_ref[...], w_ref[d_attn + d_pool:, :], preferred_element_type=F32)
    o_ref[...] = _stream_tile(refs[:n_stream], layout) + _rms(y, g_ref[...])


def _out_proj(layer, attn, pool, conv, h, w, g):
    rows, d = h.shape
    tm = PROJ_ROWS
    row_spec = lambda n: pl.BlockSpec((tm, n), lambda i: (i, 0))
    stream, stream_specs, layout = _stream_inputs(h, tm)
    return pl.pallas_call(
        functools.partial(_out_proj_kernel, n_stream=len(stream), layout=layout),
        grid=(rows // tm,),
        in_specs=stream_specs + [row_spec(attn.shape[1]), row_spec(pool.shape[1]), row_spec(conv.shape[1]),
                                 _resident(w.shape), _layer_resident(layer, (1, d))],
        out_specs=row_spec(d),
        out_shape=jax.ShapeDtypeStruct((rows, d), F32),
        compiler_params=_params(1),
        name="out_proj",
    )(*stream, attn, pool, conv, w, g)


def _ffn_kernel(h_ref, g_pre_ref, w_gate_ref, w_up_ref, w_down_ref, g_post_ref, o_ref, u_ref, acc_ref):
    c = pl.program_id(1)

    @pl.when(c == 0)
    def _():
        u_ref[...] = _rms(h_ref[...], g_pre_ref[...]).astype(BF16)
        acc_ref[...] = jnp.zeros_like(acc_ref)

    u = u_ref[...]
    gate = jnp.dot(u, w_gate_ref[...], preferred_element_type=F32)
    up = jnp.dot(u, w_up_ref[...], preferred_element_type=F32)
    a = (gate * jax.nn.sigmoid(gate) * up).astype(BF16)
    tf = w_down_ref.shape[0]
    for n in range(0, acc_ref.shape[1], tf):
        acc_ref[:, n:n + tf] += jnp.dot(a, w_down_ref[:, n:n + tf], preferred_element_type=F32)

    @pl.when(c == pl.num_programs(1) - 1)
    def _():
        o_ref[...] = h_ref[...] + _rms(acc_ref[...], g_post_ref[...])


def _ffn(layer, h, g_pre, w_gate, w_up, w_down, g_post, keep=None):
    rows, d = h.shape
    d_ff = w_gate.shape[1]
    tf = FFN_COLS
    if keep is None:
        tm, n_tiles = FFN_ROWS, rows // FFN_ROWS
        h_spec = pl.BlockSpec((tm, d), lambda i, c: (i, 0))
    else:
        rows_per_batch, first, count = keep
        tm = LAST_FFN_ROWS
        per_batch = count // tm
        n_tiles = (rows // rows_per_batch) * per_batch
        h_spec = pl.BlockSpec((pl.Element(tm), pl.Element(d)),
                              lambda i, c: (pl.multiple_of(
                                  (i // per_batch) * rows_per_batch + first + (i % per_batch) * tm, SUBLANES), 0))
        assert rows_per_batch % SUBLANES == 0 and first % SUBLANES == 0
    return pl.pallas_call(
        _ffn_kernel,
        grid=(n_tiles, d_ff // tf),
        in_specs=[h_spec, _layer_resident(layer, (1, d)),
                  pl.BlockSpec((d, tf), lambda i, c: (0, c)), pl.BlockSpec((d, tf), lambda i, c: (0, c)),
                  pl.BlockSpec((tf, d), lambda i, c: (c, 0)), _layer_resident(layer, (1, d))],
        out_specs=pl.BlockSpec((tm, d), lambda i, c: (i, 0)),
        out_shape=jax.ShapeDtypeStruct((n_tiles * tm, d), F32),
        scratch_shapes=[pltpu.VMEM((tm, d), BF16), pltpu.VMEM((tm, d), F32)],
        compiler_params=_params(2),
        name="swiglu_ffn",
    )(h, g_pre, w_gate, w_up, w_down, g_post)


def kernel(x, meta_tokens, pre_mix_g, w_in, w_pool, pool_scale, w_dw, b_dw, conv_ln_g, conv_ln_b, w_pw, mix_out_g,
           w_out, post_mix_g, pre_ffn_g, w_gate, w_up, w_down, post_ffn_g):
    batch, seq, d = x.shape
    depth = w_in.shape[0]
    d_pool = pool_scale.shape[1]
    d_conv = w_pw.shape[1]
    d_attn = w_out.shape[1] - d_pool - d_conv
    assert d_pool == LANES * len(POOL_WINDOWS) and d_attn % (2 * LANES) == 0
    assert d_attn // N_HEADS_SB == LANES // 2

    length = N_META + seq
    rows_per_batch = -(-length // ROW_ALIGN) * ROW_ALIGN
    assert rows_per_batch % SEQ_TILE == 0 and (batch * rows_per_batch) % FFN_ROWS == 0 and seq % LAST_FFN_ROWS == 0
    assert meta_tokens.shape[0] == N_META
    h = _Tokens(x.reshape(batch * seq, d), meta_tokens.astype(x.dtype), seq, rows_per_batch)

    w_pool, w_pw = w_pool.astype(BF16), w_pw.astype(BF16)
    w_in_l = w_in[0].astype(BF16)
    vec = lambda a: a[:, None, :]
    pre_mix_g, pool_scale, b_dw, conv_ln_g, conv_ln_b, post_mix_g, pre_ffn_g, post_ffn_g = (
        vec(a) for a in (pre_mix_g, pool_scale, b_dw, conv_ln_g, conv_ln_b, post_mix_g, pre_ffn_g, post_ffn_g))
    g_attn, g_pool, g_conv = (vec(a) for a in jnp.split(mix_out_g, [d_attn, d_attn + d_pool], axis=1))

    for i in range(depth):
        q, k, v, u_pool, u_conv, w_gate_l, w_up_l, w_down_l = _in_proj(
            i, h, pre_mix_g, w_in_l, d_attn, d_pool, d_conv, [(w_gate, i, 2), (w_up, i, 2), (w_down, i, 1)])
        o_attn, *w_in_next = _attention(i, q, k, v, g_attn, batch, rows_per_batch,
                                        [(w_in, i + 1, 2)] if i + 1 < depth else [])
        o_pool, o_conv, w_out_l = _mixers(i, u_pool, u_conv, w_pool, pool_scale, g_pool, w_dw, b_dw, conv_ln_g,
                                          conv_ln_b, w_pw, g_conv, batch, rows_per_batch, [(w_out, i, 2)])
        if w_in_next:
            w_in_l = w_in_next[0]
        h = _out_proj(i, o_attn, o_pool, o_conv, h, w_out_l, post_mix_g)
        keep = (rows_per_batch, N_META, seq) if i == depth - 1 else None
        h = _ffn(i, h, pre_ffn_g, w_gate_l, w_up_l, w_down_l, post_ffn_g, keep)

    return h.reshape(batch, seq, d)
```

```python
import functools
from typing import NamedTuple

import jax
import jax.numpy as jnp
from jax import lax
from jax.experimental import pallas as pl
from jax.experimental.pallas import tpu as pltpu

N_META = 16
N_HEADS_SB = 16
POOL_WINDOWS = (2, 4, 8, 16)
CONV_WIDTH = 31
EPS = 1e-6

LANES = 128
SUBLANES = 8
VMEM_BYTES_V7X = 64 * 1024 * 1024
VMEM_LIMIT = VMEM_BYTES_V7X - 8 * 1024 * 1024

SEQ_TILE = 128
TOP_ROWS = 32
ROW_ALIGN = 384
PROJ_ROWS = 384
MIX_ROWS = 384
FFN_ROWS = 768
LAST_FFN_ROWS = 512
FFN_COLS = 512
CONV_HALO = 32
POOL_HALO = SUBLANES * len(POOL_WINDOWS)
CONV_CHUNK = 32
MIX_CAST_BLOCK = 256
ATTN_CAST_BLOCK = 128
PROJ_CAST_BLOCK = 256

LOG2_E = 1.4426950408889634
LOG2_WEIGHT_FLOOR = -127.0

F32 = jnp.float32
BF16 = jnp.bfloat16


def _rms(x, g):
    return x * lax.rsqrt(jnp.mean(x * x, axis=-1, keepdims=True) + EPS) * g


def _resident(shape):
    return pl.BlockSpec(shape, lambda *_: (0,) * len(shape), pipeline_mode=pl.Buffered(1))


def _layer_resident(layer, shape):
    return pl.BlockSpec((None,) + tuple(shape), lambda *_: (layer,) + (0,) * len(shape),
                        pipeline_mode=pl.Buffered(1))


def _params(n_grid_axes):
    return pltpu.CompilerParams(dimension_semantics=("arbitrary",) * n_grid_axes, vmem_limit_bytes=VMEM_LIMIT)


class _Tokens(NamedTuple):
    x: jax.Array
    meta: jax.Array
    seq: int
    rows_per_batch: int

    @property
    def shape(self):
        return (self.x.shape[0] // self.seq * self.rows_per_batch, self.x.shape[1])


def _stream_inputs(h, tm):
    if not isinstance(h, _Tokens):
        return [h], [pl.BlockSpec((tm, h.shape[1]), lambda i, *_: (i, 0))], None
    seq, d = h.seq, h.x.shape[1]
    n_meta = h.meta.shape[0]
    tiles_per_batch = h.rows_per_batch // tm
    n_pad = h.rows_per_batch - n_meta - seq
    assert h.rows_per_batch % tm == 0 and n_pad <= tm and n_meta <= tm
    assert all(n % SUBLANES == 0 for n in (n_meta, n_pad, seq, tm))

    def start(i):
        b, t = i // tiles_per_batch, i % tiles_per_batch
        return pl.multiple_of(b * seq + jnp.clip(t * tm - n_meta, 0, seq - tm), SUBLANES)

    spec = pl.BlockSpec((pl.Element(tm), pl.Element(d)), lambda i, *_: (start(i), 0))
    return [h.x, h.meta], [spec, _resident(h.meta.shape)], (tiles_per_batch, n_pad)


def _stream_tile(stream_refs, layout):
    if layout is None:
        return stream_refs[0][...]
    x_ref, meta_ref = stream_refs
    tiles_per_batch, n_pad = layout
    blk = x_ref[...]
    tm, d = blk.shape
    t = lax.rem(pl.program_id(0), tiles_per_batch)
    first = jnp.concatenate([meta_ref[...], blk[:tm - meta_ref.shape[0]]], axis=0)
    last = jnp.concatenate([blk[n_pad:], jnp.zeros((n_pad, d), blk.dtype)], axis=0)
    return jnp.where(t == 0, first, jnp.where(t == tiles_per_batch - 1, last, blk))


def _in_proj_kernel(*refs, n_stream, layout, n_casts, d_attn, d_pool, q_scale):
    g_ref, w_ref = refs[n_stream:n_stream + 2]
    q_ref, k_ref, v_ref, pool_ref, conv_ref = refs[n_stream + 2 + n_casts:n_stream + 7 + n_casts]
    _cast_blocks(refs[n_stream + 2:n_stream + 2 + n_casts], refs[n_stream + 7 + n_casts:])
    u = _rms(_stream_tile(refs[:n_stream], layout), g_ref[...]).astype(BF16)
    o = 0
    q_ref[...] = (jnp.dot(u, w_ref[:, o:o + d_attn], preferred_element_type=F32) * q_scale).astype(BF16)
    o += d_attn
    k_ref[...] = jnp.dot(u, w_ref[:, o:o + d_attn], preferred_element_type=F32).astype(BF16)
    o += d_attn
    v_ref[...] = jnp.dot(u, w_ref[:, o:o + d_attn], preferred_element_type=F32).astype(BF16)
    o += d_attn
    pool_ref[...] = jnp.dot(u, w_ref[:, o:o + d_pool], preferred_element_type=F32)
    o += d_pool
    d_conv = conv_ref.shape[1]
    conv_ref[...] = (jnp.dot(u, w_ref[:, o:o + d_conv], preferred_element_type=F32)
                     * jax.nn.sigmoid(jnp.dot(u, w_ref[:, o + d_conv:], preferred_element_type=F32)))


def _in_proj(layer, h, g, w, d_attn, d_pool, d_conv, casts):
    rows, d = h.shape
    tm = PROJ_ROWS
    row_spec = lambda n: pl.BlockSpec((tm, n), lambda i: (i, 0))
    head_dim = d_attn // N_HEADS_SB
    stream, stream_specs, layout = _stream_inputs(h, tm)
    cast_specs = [_cast_specs(w_, l, axis, PROJ_CAST_BLOCK, lambda i: i, rows // tm) for w_, l, axis in casts]
    return pl.pallas_call(
        functools.partial(_in_proj_kernel, n_stream=len(stream), layout=layout, n_casts=len(casts), d_attn=d_attn,
                          d_pool=d_pool, q_scale=LOG2_E * head_dim ** -0.5),
        grid=(rows // tm,),
        in_specs=stream_specs + [_layer_resident(layer, (1, d)), _resident(w.shape)] + [c[0] for c in cast_specs],
        out_specs=[row_spec(d_attn), row_spec(d_attn), row_spec(d_attn), row_spec(d_pool), row_spec(d_conv)]
        + [c[1] for c in cast_specs],
        out_shape=[jax.ShapeDtypeStruct((rows, d_attn), BF16)] * 3
        + [jax.ShapeDtypeStruct((rows, d_pool), F32), jax.ShapeDtypeStruct((rows, d_conv), F32)]
        + [c[2] for c in cast_specs],
        compiler_params=_params(1),
        name="in_proj",
    )(*stream, g, w, *[w_ for w_, _, _ in casts])


def _cast_specs(w, layer, axis, block_size, step_of, n_steps):
    k, n = w.shape[1:]
    n_blocks = w.shape[axis] // block_size
    assert w.shape[axis] % block_size == 0 and n_blocks <= n_steps
    block = (block_size, n) if axis == 1 else (k, block_size)

    def index(*grid_ids):
        j = jnp.minimum(step_of(*grid_ids), n_blocks - 1)
        return (j, 0) if axis == 1 else (0, j)

    return (pl.BlockSpec((None,) + block, lambda *g: (layer,) + index(*g)), pl.BlockSpec(block, index),
            jax.ShapeDtypeStruct((k, n), BF16))


def _cast_blocks(cast_src, cast_dst):
    for src, dst in zip(cast_src, cast_dst):
        dst[...] = src[...].astype(BF16)


def _attn_kernel(*refs, n_pairs, n_casts):
    q_ref, k_ref, v_ref, tri_ref, g_ref = refs[:5]
    o_ref = refs[5 + n_casts]
    q2_ref, acc_ref, c_ref = refs[6 + 2 * n_casts:]
    _cast_blocks(refs[5:5 + n_casts], refs[6 + n_casts:6 + 2 * n_casts])
    t = SEQ_TILE
    qi = pl.program_id(1)
    pairs = range(n_pairs)
    lanes = [slice(p * LANES, (p + 1) * LANES) for p in pairs]
    first_head = lax.broadcasted_iota(jnp.int32, (t, LANES), 1) < LANES // 2
    row = lax.broadcasted_iota(jnp.int32, (2 * t, t), 0)
    col = lax.broadcasted_iota(jnp.int32, (2 * t, t), 1)
    causal = col < (row & (t - 1))
    zero = jnp.zeros((), BF16)
    for p in pairs:
        qp = q_ref[:, lanes[p]]
        q2_ref[p] = jnp.concatenate([jnp.where(first_head, qp, zero), jnp.where(first_head, zero, qp)], axis=0)

    def head_rows(ref, p, r):
        return ref[p] if r == t else jnp.concatenate([ref[p, :r], ref[p, t:t + r]], axis=0)

    def key_tile(kb, r, diagonal=False):
        start = pl.multiple_of(kb * t, t)
        z = [lax.dot_general(head_rows(q2_ref, p, r), k_ref[pl.ds(start, t), lanes[p]], (((1,), (1,)), ((), ())),
                             preferred_element_type=F32) for p in pairs]
        v = [v_ref[pl.ds(start, t), lanes[p]] for p in pairs]
        c_old = None if diagonal else [head_rows(c_ref, p, r) for p in pairs]
        log_beta, later, c_new = [], [], []
        for p in pairs:
            z_pos = jnp.maximum(z[p], 0.0)
            z_neg = z[p] - z_pos
            log_term = jnp.log(1.0 + jnp.exp2(z_neg - z_pos)) * LOG2_E
            soft = z_pos + log_term
            if diagonal:
                soft = jnp.where(causal, soft, 0.0)
            later.append(jnp.dot(soft.astype(BF16), tri_ref[...], preferred_element_type=F32))
            log_beta.append(z_neg - log_term)
            c = jnp.sum(soft, axis=1, keepdims=True)
            c_new.append(jnp.broadcast_to(c, soft.shape) if diagonal else c_old[p] + c)
        o = []
        for p in pairs:
            x = log_beta[p] - later[p]
            if not diagonal:
                x = x - c_old[p]
            w = jnp.exp2(x)
            if diagonal:
                w = jnp.where(causal, w, 0.0)
            o.append(jnp.dot(w.astype(BF16), v[p], preferred_element_type=F32))
        c_min = None
        for p in pairs:
            first = lax.broadcasted_iota(jnp.int32, (r, LANES), 1) < LANES // 2
            o_pair = jnp.where(first, o[p][:r], o[p][r:])
            if diagonal:
                acc_ref[:, lanes[p]] = o_pair
            else:
                acc_ref[:r, lanes[p]] += o_pair
            if r == t:
                c_ref[p] = c_new[p]
            else:
                c_ref[p, :r] = c_new[p][:r]
                c_ref[p, t:t + r] = c_new[p][r:]
            c_min = c_new[p] if c_min is None else jnp.minimum(c_min, c_new[p])
        if r < t:
            return jnp.min(c_min)
        top = jnp.minimum(jnp.min(c_min[:TOP_ROWS]), jnp.min(c_min[t:t + TOP_ROWS]))
        rest = jnp.minimum(jnp.min(c_min[TOP_ROWS:t]), jnp.min(c_min[t + TOP_ROWS:]))
        return top, rest

    def unfinished(c_min):
        return c_min < -LOG2_WEIGHT_FLOOR

    def full_body(s):
        kb, _, _ = s
        return (kb - 1,) + key_tile(kb, t)

    def top_body(s):
        kb, _ = s
        return kb - 1, key_tile(kb, TOP_ROWS)

    kb, top, _ = lax.while_loop(lambda s: jnp.logical_and(s[0] >= 0, unfinished(s[2])), full_body,
                                (qi - 1,) + key_tile(qi, t, diagonal=True))
    lax.while_loop(lambda s: jnp.logical_and(s[0] >= 0, unfinished(s[1])), top_body, (kb, top))
    o_ref[...] = _rms(acc_ref[...], g_ref[...]).astype(BF16)


def _attention(layer, q, k, v, g, batch, rows_per_batch, casts):
    rows, d_attn = q.shape
    t = SEQ_TILE
    nq = rows_per_batch // t
    j = jnp.arange(t)
    tri = (j[:, None] > j[None, :]).astype(BF16)
    n_pairs = d_attn // LANES
    kv_spec = pl.BlockSpec((rows_per_batch, d_attn), lambda b, i: (b, 0), pipeline_mode=pl.Buffered(1))
    tile_spec = pl.BlockSpec((t, d_attn), lambda b, i: (b * nq + i, 0))
    cast_specs = [_cast_specs(w, l, axis, ATTN_CAST_BLOCK, lambda b, i: b * nq + i, batch * nq) for w, l, axis in casts]
    return pl.pallas_call(
        functools.partial(_attn_kernel, n_pairs=n_pairs, n_casts=len(casts)),
        grid=(batch, nq),
        in_specs=[tile_spec, kv_spec, kv_spec, _resident(tri.shape), _layer_resident(layer, (1, d_attn))]
        + [c[0] for c in cast_specs],
        out_specs=[tile_spec] + [c[1] for c in cast_specs],
        out_shape=[jax.ShapeDtypeStruct((rows, d_attn), BF16)] + [c[2] for c in cast_specs],
        scratch_shapes=[pltpu.VMEM((n_pairs, 2 * t, LANES), BF16), pltpu.VMEM((t, d_attn), F32),
                        pltpu.VMEM((n_pairs, 2 * t, t), F32)],
        compiler_params=_params(2),
        name="sb_attention",
    )(q, k, v, tri, g, *[w for w, _, _ in casts])


def _mixers_kernel(*refs, d_conv, n_casts):
    (pool_ref, pool_halo_ref, conv_ref, conv_halo_ref, w_pool_ref, pool_scale_ref, g_pool_ref,
     w_dw_ref, b_dw_ref, ln_g_ref, ln_b_ref, w_pw_ref, g_conv_ref) = refs[:13]
    cast_src = refs[13:13 + n_casts]
    o_pool_ref, o_conv_ref = refs[13 + n_casts:15 + n_casts]
    cast_dst = refs[15 + n_casts:15 + 2 * n_casts]
    pool_ext, pool_sums, conv_ext, y_ref = refs[15 + 2 * n_casts:]
    _cast_blocks(cast_src, cast_dst)
    tl = pool_ref.shape[0]
    ti = pl.program_id(1)
    has_history = ti > 0

    pool_ext[:POOL_HALO, :] = jnp.where(has_history, pool_halo_ref[...], 0.0)
    pool_ext[POOL_HALO:, :] = pool_ref[...]
    n_pool = POOL_HALO + tl
    level, window_sum = pool_ext, []
    for s in range(len(POOL_WINDOWS)):
        start, shift, lo = SUBLANES * (s + 1), 2 ** s, s * LANES
        both = level[start:n_pool, lo:] + level[start - shift:n_pool - shift, lo:]
        window_sum.append(both[POOL_HALO - start:, :LANES])
        if s + 1 < len(POOL_WINDOWS):
            pool_sums[s, start:n_pool, lo:] = both
            level = pool_sums.at[s]
    pos = ti * tl + lax.broadcasted_iota(jnp.int32, (tl, 1), 0)
    mixed = []
    sq = jnp.zeros((tl, 1), F32)
    for gi, window in enumerate(POOL_WINDOWS):
        lanes = slice(gi * LANES, (gi + 1) * LANES)
        u = pool_ext[POOL_HALO:, lanes]
        count = jnp.minimum(pos + 1, window).astype(F32)
        pooled = window_sum[gi] / count - u
        m = jnp.dot(pooled.astype(BF16), w_pool_ref[gi], preferred_element_type=F32) * pool_scale_ref[:, lanes]
        sq = sq + jnp.sum(m * m, axis=-1, keepdims=True)
        mixed.append(m)
    d_pool = LANES * len(POOL_WINDOWS)
    inv = lax.rsqrt(sq / d_pool + EPS)
    for gi, m in enumerate(mixed):
        lanes = slice(gi * LANES, (gi + 1) * LANES)
        o_pool_ref[:, lanes] = (m * inv * g_pool_ref[:, lanes]).astype(BF16)

    n_ext = CONV_HALO + tl
    conv_ext[0, :CONV_HALO, :] = jnp.where(has_history, conv_halo_ref[...], 0.0)
    conv_ext[0, CONV_HALO:, :] = conv_ref[...]
    for r in range(1, SUBLANES):
        conv_ext[r, :n_ext - SUBLANES, :] = conv_ext[0, r:r + n_ext - SUBLANES, :]
    first_tap = CONV_HALO - (CONV_WIDTH - 1)

    for base in range(0, tl, CONV_CHUNK):
        acc = jnp.broadcast_to(b_dw_ref[...], (CONV_CHUNK, d_conv))
        for tap in range(CONV_WIDTH):
            shift = (first_tap + tap) % SUBLANES
            lo = base + first_tap + tap - shift
            acc = acc + w_dw_ref[tap:tap + 1, :] * conv_ext[shift, lo:lo + CONV_CHUNK, :]
        y_ref[base:base + CONV_CHUNK, :] = acc
    y = y_ref[...]
    mu = jnp.mean(y, axis=-1, keepdims=True)
    yc = y - mu
    var = jnp.mean(yc * yc, axis=-1, keepdims=True)
    y = yc * lax.rsqrt(var + EPS) * ln_g_ref[...] + ln_b_ref[...]
    y = y * jax.nn.sigmoid(y)
    o = jnp.dot(y.astype(BF16), w_pw_ref[...], preferred_element_type=F32)
    o_conv_ref[...] = _rms(o, g_conv_ref[...]).astype(BF16)


def _mixers(layer, u_pool, u_conv, w_pool, pool_scale, g_pool, w_dw, b_dw, ln_g, ln_b, w_pw, g_conv, batch,
            rows_per_batch, casts):
    rows, d_pool = u_pool.shape
    d_conv = u_conv.shape[1]
    tl = MIX_ROWS
    nt = rows_per_batch // tl

    def cur(n):
        return pl.BlockSpec((tl, n), lambda b, i: (b * nt + i, 0))

    def halo(n, h):
        return pl.BlockSpec((h, n), lambda b, i: (jnp.maximum((b * nt + i) * (tl // h) - 1, 0), 0))

    lr = functools.partial(_layer_resident, layer)
    cast_specs = [_cast_specs(w, l, axis, MIX_CAST_BLOCK, lambda b, i: b * nt + i, batch * nt) for w, l, axis in casts]
    return pl.pallas_call(
        functools.partial(_mixers_kernel, d_conv=d_conv, n_casts=len(casts)),
        grid=(batch, nt),
        in_specs=[cur(d_pool), halo(d_pool, POOL_HALO), cur(d_conv), halo(d_conv, CONV_HALO),
                  lr(w_pool.shape[1:]), lr((1, d_pool)), lr((1, d_pool)),
                  lr(w_dw.shape[1:]), lr((1, d_conv)), lr((1, d_conv)), lr((1, d_conv)),
                  lr(w_pw.shape[1:]), lr((1, d_conv))] + [c[0] for c in cast_specs],
        out_specs=[cur(d_pool), cur(d_conv)] + [c[1] for c in cast_specs],
        out_shape=[jax.ShapeDtypeStruct((rows, d_pool), BF16), jax.ShapeDtypeStruct((rows, d_conv), BF16)]
        + [c[2] for c in cast_specs],
        scratch_shapes=[pltpu.VMEM((POOL_HALO + tl, d_pool), F32),
                        pltpu.VMEM((len(POOL_WINDOWS) - 1, POOL_HALO + tl, d_pool), F32),
                        pltpu.VMEM((SUBLANES, CONV_HALO + tl, d_conv), F32), pltpu.VMEM((tl, d_conv), F32)],
        compiler_params=_params(2),
        name="pool_conv_mixers",
    )(u_pool, u_pool, u_conv, u_conv, w_pool, pool_scale, g_pool, w_dw, b_dw, ln_g, ln_b, w_pw, g_conv,
      *[w for w, _, _ in casts])


def _out_proj_kernel(*refs, n_stream, layout):
    attn_ref, pool_ref, conv_ref, w_ref, g_ref, o_ref = refs[n_stream:]
    d_attn, d_pool = attn_ref.shape[1], pool_ref.shape[1]
    y = jnp.dot(attn_ref[...], w_ref[:d_attn, :], preferred_element_type=F32)
    y = y + jnp.dot(pool_ref[...], w_ref[d_attn:d_attn + d_pool, :], preferred_element_type=F32)
    y = y + jnp.dot(conv_ref[...], w_ref[d_attn + d_pool:, :], preferred_element_type=F32)
    o_ref[...] = _stream_tile(refs[:n_stream], layout) + _rms(y, g_ref[...])


def _out_proj(layer, attn, pool, conv, h, w, g):
    rows, d = h.shape
    tm = PROJ_ROWS
    row_spec = lambda n: pl.BlockSpec((tm, n), lambda i: (i, 0))
    stream, stream_specs, layout = _stream_inputs(h, tm)
    return pl.pallas_call(
        functools.partial(_out_proj_kernel, n_stream=len(stream), layout=layout),
        grid=(rows // tm,),
        in_specs=stream_specs + [row_spec(attn.shape[1]), row_spec(pool.shape[1]), row_spec(conv.shape[1]),
                                 _resident(w.shape), _layer_resident(layer, (1, d))],
        out_specs=row_spec(d),
        out_shape=jax.ShapeDtypeStruct((rows, d), F32),
        compiler_params=_params(1),
        name="out_proj",
    )(*stream, attn, pool, conv, w, g)


def _ffn_kernel(h_ref, g_pre_ref, w_gate_ref, w_up_ref, w_down_ref, g_post_ref, o_ref, u_ref, acc_ref):
    c = pl.program_id(1)

    @pl.when(c == 0)
    def _():
        u_ref[...] = _rms(h_ref[...], g_pre_ref[...]).astype(BF16)
        acc_ref[...] = jnp.zeros_like(acc_ref)

    u = u_ref[...]
    gate = jnp.dot(u, w_gate_ref[...], preferred_element_type=F32)
    up = jnp.dot(u, w_up_ref[...], preferred_element_type=F32)
    a = (gate * jax.nn.sigmoid(gate) * up).astype(BF16)
    tf = w_down_ref.shape[0]
    for n in range(0, acc_ref.shape[1], tf):
        acc_ref[:, n:n + tf] += jnp.dot(a, w_down_ref[:, n:n + tf], preferred_element_type=F32)

    @pl.when(c == pl.num_programs(1) - 1)
    def _():
        o_ref[...] = h_ref[...] + _rms(acc_ref[...], g_post_ref[...])


def _ffn(layer, h, g_pre, w_gate, w_up, w_down, g_post, keep=None):
    rows, d = h.shape
    d_ff = w_gate.shape[1]
    tf = FFN_COLS
    if keep is None:
        tm, n_tiles = FFN_ROWS, rows // FFN_ROWS
        h_spec = pl.BlockSpec((tm, d), lambda i, c: (i, 0))
    else:
        rows_per_batch, first, count = keep
        tm = LAST_FFN_ROWS
        per_batch = count // tm
        n_tiles = (rows // rows_per_batch) * per_batch
        h_spec = pl.BlockSpec((pl.Element(tm), pl.Element(d)),
                              lambda i, c: (pl.multiple_of(
                                  (i // per_batch) * rows_per_batch + first + (i % per_batch) * tm, SUBLANES), 0))
        assert rows_per_batch % SUBLANES == 0 and first % SUBLANES == 0
    return pl.pallas_call(
        _ffn_kernel,
        grid=(n_tiles, d_ff // tf),
        in_specs=[h_spec, _layer_resident(layer, (1, d)),
                  pl.BlockSpec((d, tf), lambda i, c: (0, c)), pl.BlockSpec((d, tf), lambda i, c: (0, c)),
                  pl.BlockSpec((tf, d), lambda i, c: (c, 0)), _layer_resident(layer, (1, d))],
        out_specs=pl.BlockSpec((tm, d), lambda i, c: (i, 0)),
        out_shape=jax.ShapeDtypeStruct((n_tiles * tm, d), F32),
        scratch_shapes=[pltpu.VMEM((tm, d), BF16), pltpu.VMEM((tm, d), F32)],
        compiler_params=_params(2),
        name="swiglu_ffn",
    )(h, g_pre, w_gate, w_up, w_down, g_post)


def kernel(x, meta_tokens, pre_mix_g, w_in, w_pool, pool_scale, w_dw, b_dw, conv_ln_g, conv_ln_b, w_pw, mix_out_g,
           w_out, post_mix_g, pre_ffn_g, w_gate, w_up, w_down, post_ffn_g):
    batch, seq, d = x.shape
    depth = w_in.shape[0]
    d_pool = pool_scale.shape[1]
    d_conv = w_pw.shape[1]
    d_attn = w_out.shape[1] - d_pool - d_conv
    assert d_pool == LANES * len(POOL_WINDOWS) and d_attn % (2 * LANES) == 0
    assert POOL_WINDOWS == tuple(2 ** (g + 1) for g in range(len(POOL_WINDOWS)))
    assert d_attn // N_HEADS_SB == LANES // 2

    length = N_META + seq
    rows_per_batch = -(-length // ROW_ALIGN) * ROW_ALIGN
    assert rows_per_batch % SEQ_TILE == 0 and (batch * rows_per_batch) % FFN_ROWS == 0 and seq % LAST_FFN_ROWS == 0
    assert meta_tokens.shape[0] == N_META
    h = _Tokens(x.reshape(batch * seq, d), meta_tokens.astype(x.dtype), seq, rows_per_batch)

    w_pool, w_pw = w_pool.astype(BF16), w_pw.astype(BF16)
    w_in_l = w_in[0].astype(BF16)
    vec = lambda a: a[:, None, :]
    pre_mix_g, pool_scale, b_dw, conv_ln_g, conv_ln_b, post_mix_g, pre_ffn_g, post_ffn_g = (
        vec(a) for a in (pre_mix_g, pool_scale, b_dw, conv_ln_g, conv_ln_b, post_mix_g, pre_ffn_g, post_ffn_g))
    g_attn, g_pool, g_conv = (vec(a) for a in jnp.split(mix_out_g, [d_attn, d_attn + d_pool], axis=1))

    for i in range(depth):
        q, k, v, u_pool, u_conv, w_gate_l, w_up_l, w_down_l = _in_proj(
            i, h, pre_mix_g, w_in_l, d_attn, d_pool, d_conv, [(w_gate, i, 2), (w_up, i, 2), (w_down, i, 1)])
        o_attn, *w_in_next = _attention(i, q, k, v, g_attn, batch, rows_per_batch,
                                        [(w_in, i + 1, 2)] if i + 1 < depth else [])
        o_pool, o_conv, w_out_l = _mixers(i, u_pool, u_conv, w_pool, pool_scale, g_pool, w_dw, b_dw, conv_ln_g,
                                          conv_ln_b, w_pw, g_conv, batch, rows_per_batch, [(w_out, i, 2)])
        if w_in_next:
            w_in_l = w_in_next[0]
        h = _out_proj(i, o_attn, o_pool, o_conv, h, w_out_l, post_mix_g)
        keep = (rows_per_batch, N_META, seq) if i == depth - 1 else None
        h = _ffn(i, h, pre_ffn_g, w_gate_l, w_up_l, w_down_l, post_ffn_g, keep)

    return h.reshape(batch, seq, d)
```

```python
import functools
from typing import NamedTuple

import jax
import jax.numpy as jnp
from jax import lax
from jax.experimental import pallas as pl
from jax.experimental.pallas import tpu as pltpu

N_META = 16
N_HEADS_SB = 16
POOL_WINDOWS = (2, 4, 8, 16)
CONV_WIDTH = 31
EPS = 1e-6

LANES = 128
SUBLANES = 8
VMEM_BYTES_V7X = 64 * 1024 * 1024
VMEM_LIMIT = VMEM_BYTES_V7X - 8 * 1024 * 1024

SEQ_TILE = 128
TOP_ROWS = 32
ROW_ALIGN = 384
PROJ_ROWS = 384
FFN_ROWS = 768
LAST_FFN_ROWS = 512
FFN_COLS = 512
CONV_HALO = 32
POOL_HALO = SUBLANES * len(POOL_WINDOWS)
CONV_CHUNK = 32
ATTN_CAST_BLOCK = 128

LOG2_E = 1.4426950408889634
LOG2_WEIGHT_FLOOR = -127.0

F32 = jnp.float32
BF16 = jnp.bfloat16


def _rms(x, g):
    return x * lax.rsqrt(jnp.mean(x * x, axis=-1, keepdims=True) + EPS) * g


def _resident(shape):
    return pl.BlockSpec(shape, lambda *_: (0,) * len(shape), pipeline_mode=pl.Buffered(1))


def _layer_resident(layer, shape):
    return pl.BlockSpec((None,) + tuple(shape), lambda *_: (layer,) + (0,) * len(shape),
                        pipeline_mode=pl.Buffered(1))


def _params(n_grid_axes):
    return pltpu.CompilerParams(dimension_semantics=("arbitrary",) * n_grid_axes, vmem_limit_bytes=VMEM_LIMIT)


class _Tokens(NamedTuple):
    x: jax.Array
    meta: jax.Array
    seq: int
    rows_per_batch: int

    @property
    def shape(self):
        return (self.x.shape[0] // self.seq * self.rows_per_batch, self.x.shape[1])


def _stream_inputs(h, tm, tile_of=lambda i: i):
    if not isinstance(h, _Tokens):
        return [h], [pl.BlockSpec((tm, h.shape[1]), lambda i, *_: (tile_of(i), 0))], None
    seq, d = h.seq, h.x.shape[1]
    n_meta = h.meta.shape[0]
    tiles_per_batch = h.rows_per_batch // tm
    n_pad = h.rows_per_batch - n_meta - seq
    assert h.rows_per_batch % tm == 0 and n_pad <= tm and n_meta <= tm
    assert all(n % SUBLANES == 0 for n in (n_meta, n_pad, seq, tm))

    def start(i):
        b, t = i // tiles_per_batch, i % tiles_per_batch
        return pl.multiple_of(b * seq + jnp.clip(t * tm - n_meta, 0, seq - tm), SUBLANES)

    spec = pl.BlockSpec((pl.Element(tm), pl.Element(d)), lambda i, *_: (start(tile_of(i)), 0))
    return [h.x, h.meta], [spec, _resident(h.meta.shape)], (tiles_per_batch, n_pad)


def _stream_tile(stream_refs, layout, tile=None):
    if layout is None:
        return stream_refs[0][...]
    x_ref, meta_ref = stream_refs
    tiles_per_batch, n_pad = layout
    blk = x_ref[...]
    tm, d = blk.shape
    t = lax.rem(pl.program_id(0) if tile is None else tile, tiles_per_batch)
    first = jnp.concatenate([meta_ref[...], blk[:tm - meta_ref.shape[0]]], axis=0)
    last = jnp.concatenate([blk[n_pad:], jnp.zeros((n_pad, d), blk.dtype)], axis=0)
    return jnp.where(t == 0, first, jnp.where(t == tiles_per_batch - 1, last, blk))


def _cast_specs(w, layer, axis, block_size, step_of, n_steps):
    k, n = w.shape[1:]
    n_blocks = w.shape[axis] // block_size
    assert w.shape[axis] % block_size == 0 and n_blocks <= n_steps
    block = (block_size, n) if axis == 1 else (k, block_size)

    def index(*grid_ids):
        j = jnp.minimum(step_of(*grid_ids), n_blocks - 1)
        return (j, 0) if axis == 1 else (0, j)

    return (pl.BlockSpec((None,) + block, lambda *g: (layer,) + index(*g)), pl.BlockSpec(block, index),
            jax.ShapeDtypeStruct((k, n), BF16))


def _cast_blocks(cast_src, cast_dst):
    for src, dst in zip(cast_src, cast_dst):
        dst[...] = src[...].astype(BF16)


def _attn_kernel(*refs, n_pairs, n_casts):
    q_ref, k_ref, v_ref, tri_ref, g_ref = refs[:5]
    o_ref = refs[5 + n_casts]
    q2_ref, acc_ref, c_ref = refs[6 + 2 * n_casts:]
    _cast_blocks(refs[5:5 + n_casts], refs[6 + n_casts:6 + 2 * n_casts])
    t = SEQ_TILE
    qi = pl.program_id(1)
    pairs = range(n_pairs)
    lanes = [slice(p * LANES, (p + 1) * LANES) for p in pairs]
    first_head = lax.broadcasted_iota(jnp.int32, (t, LANES), 1) < LANES // 2
    row = lax.broadcasted_iota(jnp.int32, (2 * t, t), 0)
    col = lax.broadcasted_iota(jnp.int32, (2 * t, t), 1)
    causal = col < (row & (t - 1))
    zero = jnp.zeros((), BF16)
    for p in pairs:
        qp = q_ref[:, lanes[p]]
        q2_ref[p] = jnp.concatenate([jnp.where(first_head, qp, zero), jnp.where(first_head, zero, qp)], axis=0)

    def head_rows(ref, p, r):
        return ref[p] if r == t else jnp.concatenate([ref[p, :r], ref[p, t:t + r]], axis=0)

    def key_tile(kb, r, diagonal=False):
        start = pl.multiple_of(kb * t, t)
        z = [lax.dot_general(head_rows(q2_ref, p, r), k_ref[pl.ds(start, t), lanes[p]], (((1,), (1,)), ((), ())),
                             preferred_element_type=F32) for p in pairs]
        v = [v_ref[pl.ds(start, t), lanes[p]] for p in pairs]
        c_old = None if diagonal else [head_rows(c_ref, p, r) for p in pairs]
        log_beta, later, c_new = [], [], []
        for p in pairs:
            z_pos = jnp.maximum(z[p], 0.0)
            z_neg = z[p] - z_pos
            log_term = jnp.log(1.0 + jnp.exp2(z_neg - z_pos)) * LOG2_E
            soft = z_pos + log_term
            if diagonal:
                soft = jnp.where(causal, soft, 0.0)
            later.append(jnp.dot(soft.astype(BF16), tri_ref[...], preferred_element_type=F32))
            log_beta.append(z_neg - log_term)
            c = jnp.sum(soft, axis=1, keepdims=True)
            c_new.append(jnp.broadcast_to(c, soft.shape) if diagonal else c_old[p] + c)
        o = []
        for p in pairs:
            x = log_beta[p] - later[p]
            if not diagonal:
                x = x - c_old[p]
            w = jnp.exp2(x)
            if diagonal:
                w = jnp.where(causal, w, 0.0)
            o.append(jnp.dot(w.astype(BF16), v[p], preferred_element_type=F32))
        c_min = None
        for p in pairs:
            first = lax.broadcasted_iota(jnp.int32, (r, LANES), 1) < LANES // 2
            o_pair = jnp.where(first, o[p][:r], o[p][r:])
            if diagonal:
                acc_ref[:, lanes[p]] = o_pair
            else:
                acc_ref[:r, lanes[p]] += o_pair
            if r == t:
                c_ref[p] = c_new[p]
            else:
                c_ref[p, :r] = c_new[p][:r]
                c_ref[p, t:t + r] = c_new[p][r:]
            c_min = c_new[p] if c_min is None else jnp.minimum(c_min, c_new[p])
        if r < t:
            return jnp.min(c_min)
        top = jnp.minimum(jnp.min(c_min[:TOP_ROWS]), jnp.min(c_min[t:t + TOP_ROWS]))
        rest = jnp.minimum(jnp.min(c_min[TOP_ROWS:t]), jnp.min(c_min[t + TOP_ROWS:]))
        return top, rest

    def unfinished(c_min):
        return c_min < -LOG2_WEIGHT_FLOOR

    def full_body(s):
        kb, _, _ = s
        return (kb - 1,) + key_tile(kb, t)

    def top_body(s):
        kb, _ = s
        return kb - 1, key_tile(kb, TOP_ROWS)

    kb, top, _ = lax.while_loop(lambda s: jnp.logical_and(s[0] >= 0, unfinished(s[2])), full_body,
                                (qi - 1,) + key_tile(qi, t, diagonal=True))
    lax.while_loop(lambda s: jnp.logical_and(s[0] >= 0, unfinished(s[1])), top_body, (kb, top))
    o_ref[...] = _rms(acc_ref[...], g_ref[...]).astype(BF16)


def _attention(layer, q, k, v, g, batch, rows_per_batch, casts):
    rows, d_attn = q.shape
    t = SEQ_TILE
    nq = rows_per_batch // t
    j = jnp.arange(t)
    tri = (j[:, None] > j[None, :]).astype(BF16)
    n_pairs = d_attn // LANES
    kv_spec = pl.BlockSpec((rows_per_batch, d_attn), lambda b, i: (b, 0), pipeline_mode=pl.Buffered(1))
    tile_spec = pl.BlockSpec((t, d_attn), lambda b, i: (b * nq + i, 0))
    cast_specs = [_cast_specs(w, l, axis, ATTN_CAST_BLOCK, lambda b, i: b * nq + i, batch * nq) for w, l, axis in casts]
    return pl.pallas_call(
        functools.partial(_attn_kernel, n_pairs=n_pairs, n_casts=len(casts)),
        grid=(batch, nq),
        in_specs=[tile_spec, kv_spec, kv_spec, _resident(tri.shape), _layer_resident(layer, (1, d_attn))]
        + [c[0] for c in cast_specs],
        out_specs=[tile_spec] + [c[1] for c in cast_specs],
        out_shape=[jax.ShapeDtypeStruct((rows, d_attn), BF16)] + [c[2] for c in cast_specs],
        scratch_shapes=[pltpu.VMEM((n_pairs, 2 * t, LANES), BF16), pltpu.VMEM((t, d_attn), F32),
                        pltpu.VMEM((n_pairs, 2 * t, t), F32)],
        compiler_params=_params(2),
        name="sb_attention",
    )(q, k, v, tri, g, *[w for w, _, _ in casts])


def _digest(v):
    v = v.astype(F32)
    rows, cols = v.shape
    top = jnp.max(v.reshape(rows // SUBLANES, SUBLANES, cols), axis=0)
    return functools.reduce(jnp.maximum, [top[:, c:c + LANES] for c in range(0, cols, LANES)])


def _zero_after(digests):
    bits = pltpu.bitcast(functools.reduce(jnp.maximum, digests), jnp.uint32)
    zero = ((bits >> 16) >> 16).astype(F32)
    return jnp.concatenate([zero, zero], axis=0).astype(BF16)


def _proj_mix_kernel(*refs, n_stream, layout, n_tiles, tiles_per_batch, d_attn, d_pool, d_conv, q_scale):
    ns = n_stream
    (g_ref, w_ref, w_pool_ref, pool_scale_ref, g_pool_ref, w_dw_ref, b_dw_ref, ln_g_ref, ln_b_ref, w_pw_ref,
     g_conv_ref) = refs[ns:ns + 11]
    q_ref, k_ref, v_ref, o_pool_ref, o_conv_ref, pool_ext, pool_sums, conv_ext, y_ref = refs[ns + 11:]
    i = pl.program_id(0)
    tile = jnp.minimum(i, n_tiles - 1)
    tm = q_ref.shape[0]

    @pl.when(i == 0)
    def _():
        pool_ext[...] = jnp.zeros_like(pool_ext)
        conv_ext[0] = jnp.zeros(conv_ext.shape[1:], F32)

    u = _rms(_stream_tile(refs[:ns], layout, tile), g_ref[...]).astype(BF16)

    def proj(lo, hi, wait_for=()):
        lhs = u
        if wait_for:
            head = jnp.concatenate([u[:16, :LANES] + _zero_after(wait_for), u[:16, LANES:]], axis=1)
            lhs = jnp.concatenate([head, u[16:]], axis=0)
        return jnp.dot(lhs, w_ref[:, lo:hi], preferred_element_type=F32)

    def mix_pool():
        n_pool = POOL_HALO + tm
        level, window_sum = pool_ext, []
        for s in range(len(POOL_WINDOWS)):
            start, shift, lo = SUBLANES * (s + 1), 2 ** s, s * LANES
            both = level[start:n_pool, lo:] + level[start - shift:n_pool - shift, lo:]
            window_sum.append(both[POOL_HALO - start:, :LANES])
            if s + 1 < len(POOL_WINDOWS):
                pool_sums[s, start:n_pool, lo:] = both
                level = pool_sums.at[s]
        ti = lax.rem(i + tiles_per_batch - 1, tiles_per_batch)
        pos = ti * tm + lax.broadcasted_iota(jnp.int32, (tm, 1), 0)
        mixed = []
        sq = jnp.zeros((tm, 1), F32)
        for gi, window in enumerate(POOL_WINDOWS):
            lanes = slice(gi * LANES, (gi + 1) * LANES)
            count = jnp.minimum(pos + 1, window).astype(F32)
            pooled = window_sum[gi] / count - pool_ext[POOL_HALO:, lanes]
            m = jnp.dot(pooled.astype(BF16), w_pool_ref[gi], preferred_element_type=F32) * pool_scale_ref[:, lanes]
            sq = sq + jnp.sum(m * m, axis=-1, keepdims=True)
            mixed.append(m)
        inv = lax.rsqrt(sq / d_pool + EPS)
        out = [(m * inv * g_pool_ref[:, gi * LANES:(gi + 1) * LANES]).astype(BF16) for gi, m in enumerate(mixed)]
        for gi, o in enumerate(out):
            o_pool_ref[:, gi * LANES:(gi + 1) * LANES] = o
        return [_digest(o) for o in out]

    n_ext = CONV_HALO + tm

    def conv_copies():
        digests = []
        for r in range(1, SUBLANES):
            shifted = conv_ext[0, r:r + n_ext - SUBLANES, :]
            conv_ext[r, :n_ext - SUBLANES, :] = shifted
            digests.append(_digest(shifted))
        return digests

    first_tap = CONV_HALO - (CONV_WIDTH - 1)

    def conv_rows(base):
        acc = jnp.broadcast_to(b_dw_ref[...], (CONV_CHUNK, d_conv))
        for tap in range(CONV_WIDTH):
            shift = (first_tap + tap) % SUBLANES
            lo = base + first_tap + tap - shift
            acc = acc + w_dw_ref[tap:tap + 1, :] * conv_ext[shift, lo:lo + CONV_CHUNK, :]
        y_ref[base:base + CONV_CHUNK, :] = acc
        return _digest(acc)

    def conv_finish():
        y = y_ref[...]
        mu = jnp.mean(y, axis=-1, keepdims=True)
        yc = y - mu
        var = jnp.mean(yc * yc, axis=-1, keepdims=True)
        y = yc * lax.rsqrt(var + EPS) * ln_g_ref[...] + ln_b_ref[...]
        y = y * jax.nn.sigmoid(y)
        o = _rms(jnp.dot(y.astype(BF16), w_pw_ref[...], preferred_element_type=F32), g_conv_ref[...]).astype(BF16)
        o_conv_ref[...] = o
        return [_digest(o)]

    half = d_attn // 2
    chunks = list(range(0, tm, CONV_CHUNK))
    per_group = -(-len(chunks) // 4)
    conv_group = lambda g: [conv_rows(base) for base in chunks[g * per_group:(g + 1) * per_group]]
    q_ref[:, :half] = (proj(0, half) * q_scale).astype(BF16)
    q_ref[:, half:] = (proj(half, d_attn) * q_scale).astype(BF16)
    done = mix_pool()
    k_ref[:, :half] = proj(d_attn, d_attn + half, done).astype(BF16)
    done = conv_copies()
    k_ref[:, half:] = proj(d_attn + half, 2 * d_attn, done).astype(BF16)
    done = conv_group(0)
    v_ref[:, :half] = proj(2 * d_attn, 2 * d_attn + half, done).astype(BF16)
    done = conv_group(1)
    v_ref[:, half:] = proj(2 * d_attn + half, 3 * d_attn, done).astype(BF16)
    done = conv_group(2)
    o = 3 * d_attn
    new_pool = proj(o, o + d_pool, done)
    done = conv_group(3)
    o += d_pool
    conv_value = proj(o, o + d_conv, done)
    done = conv_finish()
    new_conv = conv_value * jax.nn.sigmoid(proj(o + d_conv, o + 2 * d_conv, done))

    starts_batch = lax.rem(tile, tiles_per_batch) == 0
    pool_ext[:POOL_HALO, :] = jnp.where(starts_batch, 0.0, pool_ext[tm:, :])
    pool_ext[POOL_HALO:, :] = new_pool
    conv_ext[0, :CONV_HALO, :] = jnp.where(starts_batch, 0.0, conv_ext[0, tm:, :])
    conv_ext[0, CONV_HALO:, :] = new_conv


def _proj_mix(layer, h, g, w, w_pool, pool_scale, g_pool, w_dw, b_dw, ln_g, ln_b, w_pw, g_conv, d_attn,
              rows_per_batch):
    rows, d = h.shape
    d_pool, d_conv = pool_scale.shape[-1], w_pw.shape[-1]
    tm = PROJ_ROWS
    n = rows // tm
    head_dim = d_attn // N_HEADS_SB
    tile_of = lambda i: jnp.minimum(i, n - 1)
    cur = lambda width: pl.BlockSpec((tm, width), lambda i: (tile_of(i), 0))
    prev = lambda width: pl.BlockSpec((tm, width), lambda i: (jnp.maximum(i - 1, 0), 0))
    lr = functools.partial(_layer_resident, layer)
    stream, stream_specs, layout = _stream_inputs(h, tm, tile_of)
    return pl.pallas_call(
        functools.partial(_proj_mix_kernel, n_stream=len(stream), layout=layout, n_tiles=n,
                          tiles_per_batch=rows_per_batch // tm, d_attn=d_attn, d_pool=d_pool, d_conv=d_conv,
                          q_scale=LOG2_E * head_dim ** -0.5),
        grid=(n + 1,),
        in_specs=stream_specs + [lr((1, d)), _resident(w.shape), lr(w_pool.shape[1:]), lr((1, d_pool)),
                                 lr((1, d_pool)), lr(w_dw.shape[1:]), lr((1, d_conv)), lr((1, d_conv)),
                                 lr((1, d_conv)), lr(w_pw.shape[1:]), lr((1, d_conv))],
        out_specs=[cur(d_attn), cur(d_attn), cur(d_attn), prev(d_pool), prev(d_conv)],
        out_shape=[jax.ShapeDtypeStruct((rows, d_attn), BF16)] * 3
        + [jax.ShapeDtypeStruct((rows, d_pool), BF16), jax.ShapeDtypeStruct((rows, d_conv), BF16)],
        scratch_shapes=[pltpu.VMEM((POOL_HALO + tm, d_pool), F32),
                        pltpu.VMEM((len(POOL_WINDOWS) - 1, POOL_HALO + tm, d_pool), F32),
                        pltpu.VMEM((SUBLANES, CONV_HALO + tm, d_conv), F32), pltpu.VMEM((tm, d_conv), F32)],
        compiler_params=_params(1),
        name="in_proj_mixers",
    )(*stream, g, w, w_pool, pool_scale, g_pool, w_dw, b_dw, ln_g, ln_b, w_pw, g_conv)


def _out_proj_kernel(*refs, n_stream, layout):
    attn_ref, pool_ref, conv_ref, w_ref, g_ref, o_ref = refs[n_stream:]
    merged = jnp.concatenate([attn_ref[...], pool_ref[...], conv_ref[...]], axis=1)
    y = jnp.dot(merged, w_ref[...], preferred_element_type=F32)
    o_ref[...] = _stream_tile(refs[:n_stream], layout) + _rms(y, g_ref[...])


def _out_proj(layer, attn, pool, conv, h, w, g):
    rows, d = h.shape
    tm = PROJ_ROWS
    row_spec = lambda n: pl.BlockSpec((tm, n), lambda i: (i, 0))
    stream, stream_specs, layout = _stream_inputs(h, tm)
    return pl.pallas_call(
        functools.partial(_out_proj_kernel, n_stream=len(stream), layout=layout),
        grid=(rows // tm,),
        in_specs=stream_specs + [row_spec(attn.shape[1]), row_spec(pool.shape[1]), row_spec(conv.shape[1]),
                                 _resident(w.shape), _layer_resident(layer, (1, d))],
        out_specs=row_spec(d),
        out_shape=jax.ShapeDtypeStruct((rows, d), F32),
        compiler_params=_params(1),
        name="out_proj",
    )(*stream, attn, pool, conv, w, g)


def _ffn_kernel(h_ref, g_pre_ref, w_gate_ref, w_up_ref, w_down_ref, g_post_ref, o_ref, u_ref, acc_ref):
    c = pl.program_id(1)

    @pl.when(c == 0)
    def _():
        u_ref[...] = _rms(h_ref[...], g_pre_ref[...]).astype(BF16)
        acc_ref[...] = jnp.zeros_like(acc_ref)

    u = u_ref[...]
    gate = jnp.dot(u, w_gate_ref[...], preferred_element_type=F32)
    up = jnp.dot(u, w_up_ref[...], preferred_element_type=F32)
    a = (gate * jax.nn.sigmoid(gate) * up).astype(BF16)
    tf = w_down_ref.shape[0]
    for n in range(0, acc_ref.shape[1], tf):
        acc_ref[:, n:n + tf] += jnp.dot(a, w_down_ref[:, n:n + tf], preferred_element_type=F32)

    @pl.when(c == pl.num_programs(1) - 1)
    def _():
        o_ref[...] = h_ref[...] + _rms(acc_ref[...], g_post_ref[...])


def _ffn(layer, h, g_pre, w_gate, w_up, w_down, g_post, keep=None):
    rows, d = h.shape
    d_ff = w_gate.shape[1]
    tf = FFN_COLS
    if keep is None:
        tm, n_tiles = FFN_ROWS, rows // FFN_ROWS
        h_spec = pl.BlockSpec((tm, d), lambda i, c: (i, 0))
    else:
        rows_per_batch, first, count = keep
        tm = LAST_FFN_ROWS
        per_batch = count // tm
        n_tiles = (rows // rows_per_batch) * per_batch
        h_spec = pl.BlockSpec((pl.Element(tm), pl.Element(d)),
                              lambda i, c: (pl.multiple_of(
                                  (i // per_batch) * rows_per_batch + first + (i % per_batch) * tm, SUBLANES), 0))
        assert rows_per_batch % SUBLANES == 0 and first % SUBLANES == 0
    return pl.pallas_call(
        _ffn_kernel,
        grid=(n_tiles, d_ff // tf),
        in_specs=[h_spec, _layer_resident(layer, (1, d)),
                  pl.BlockSpec((d, tf), lambda i, c: (0, c)), pl.BlockSpec((d, tf), lambda i, c: (0, c)),
                  pl.BlockSpec((tf, d), lambda i, c: (c, 0)), _layer_resident(layer, (1, d))],
        out_specs=pl.BlockSpec((tm, d), lambda i, c: (i, 0)),
        out_shape=jax.ShapeDtypeStruct((n_tiles * tm, d), F32),
        scratch_shapes=[pltpu.VMEM((tm, d), BF16), pltpu.VMEM((tm, d), F32)],
        compiler_params=_params(2),
        name="swiglu_ffn",
    )(h, g_pre, w_gate, w_up, w_down, g_post)


def kernel(x, meta_tokens, pre_mix_g, w_in, w_pool, pool_scale, w_dw, b_dw, conv_ln_g, conv_ln_b, w_pw, mix_out_g,
           w_out, post_mix_g, pre_ffn_g, w_gate, w_up, w_down, post_ffn_g):
    batch, seq, d = x.shape
    depth = w_in.shape[0]
    d_pool = pool_scale.shape[1]
    d_conv = w_pw.shape[1]
    d_attn = w_out.shape[1] - d_pool - d_conv
    assert d_pool == LANES * len(POOL_WINDOWS) and d_attn % (2 * LANES) == 0
    assert POOL_WINDOWS == tuple(2 ** (g + 1) for g in range(len(POOL_WINDOWS)))
    assert d_attn // N_HEADS_SB == LANES // 2

    length = N_META + seq
    rows_per_batch = -(-length // ROW_ALIGN) * ROW_ALIGN
    assert rows_per_batch % SEQ_TILE == 0 and (batch * rows_per_batch) % FFN_ROWS == 0 and seq % LAST_FFN_ROWS == 0
    assert meta_tokens.shape[0] == N_META
    h = _Tokens(x.reshape(batch * seq, d), meta_tokens.astype(x.dtype), seq, rows_per_batch)

    w_pool, w_pw = w_pool.astype(BF16), w_pw.astype(BF16)
    w_in_l = w_in[0].astype(BF16)
    vec = lambda a: a[:, None, :]
    pre_mix_g, pool_scale, b_dw, conv_ln_g, conv_ln_b, post_mix_g, pre_ffn_g, post_ffn_g = (
        vec(a) for a in (pre_mix_g, pool_scale, b_dw, conv_ln_g, conv_ln_b, post_mix_g, pre_ffn_g, post_ffn_g))
    g_attn, g_pool, g_conv = (vec(a) for a in jnp.split(mix_out_g, [d_attn, d_attn + d_pool], axis=1))

    for i in range(depth):
        q, k, v, o_pool, o_conv = _proj_mix(i, h, pre_mix_g, w_in_l, w_pool, pool_scale, g_pool, w_dw, b_dw, conv_ln_g,
                                            conv_ln_b, w_pw, g_conv, d_attn, rows_per_batch)
        casts = [(w_out, i, 2), (w_gate, i, 2), (w_up, i, 2), (w_down, i, 1)]
        if i + 1 < depth:
            casts.append((w_in, i + 1, 2))
        o_attn, w_out_l, w_gate_l, w_up_l, w_down_l, *w_in_next = _attention(i, q, k, v, g_attn, batch,
                                                                             rows_per_batch, casts)
        if w_in_next:
            w_in_l = w_in_next[0]
        h = _out_proj(i, o_attn, o_pool, o_conv, h, w_out_l, post_mix_g)
        keep = (rows_per_batch, N_META, seq) if i == depth - 1 else None
        h = _ffn(i, h, pre_ffn_g, w_gate_l, w_up_l, w_down_l, post_ffn_g, keep)

    return h.reshape(batch, seq, d)
```

```python
import functools
from typing import NamedTuple

import jax
import jax.numpy as jnp
from jax import lax
from jax.experimental import pallas as pl
from jax.experimental.pallas import tpu as pltpu

N_META = 16
N_HEADS_SB = 16
POOL_WINDOWS = (2, 4, 8, 16)
CONV_WIDTH = 31
EPS = 1e-6

LANES = 128
SUBLANES = 8
VMEM_BYTES_V7X = 64 * 1024 * 1024
VMEM_LIMIT = VMEM_BYTES_V7X - 8 * 1024 * 1024

SEQ_TILE = 128
TOP_ROWS = 32
ROW_ALIGN = 384
PROJ_ROWS = 384
FFN_ROWS = 768
LAST_FFN_ROWS = 512
FFN_COLS = 512
CONV_HALO = 32
POOL_HALO = SUBLANES * len(POOL_WINDOWS)
CONV_CHUNK = 32
ATTN_CAST_BLOCK = 128
PROJ_CAST_BLOCK = 256

LOG2_E = 1.4426950408889634
LOG2_WEIGHT_FLOOR = -127.0

F32 = jnp.float32
BF16 = jnp.bfloat16


def _rms(x, g):
    return x * lax.rsqrt(jnp.mean(x * x, axis=-1, keepdims=True) + EPS) * g


def _resident(shape):
    return pl.BlockSpec(shape, lambda *_: (0,) * len(shape), pipeline_mode=pl.Buffered(1))


def _layer_resident(layer, shape):
    return pl.BlockSpec((None,) + tuple(shape), lambda *_: (layer,) + (0,) * len(shape),
                        pipeline_mode=pl.Buffered(1))


def _params(n_grid_axes):
    return pltpu.CompilerParams(dimension_semantics=("arbitrary",) * n_grid_axes, vmem_limit_bytes=VMEM_LIMIT)


class _Tokens(NamedTuple):
    x: jax.Array
    meta: jax.Array
    seq: int
    rows_per_batch: int

    @property
    def shape(self):
        return (self.x.shape[0] // self.seq * self.rows_per_batch, self.x.shape[1])


def _stream_inputs(h, tm, tile_of=lambda i: i):
    if not isinstance(h, _Tokens):
        return [h], [pl.BlockSpec((tm, h.shape[1]), lambda i, *_: (tile_of(i), 0))], None
    seq, d = h.seq, h.x.shape[1]
    n_meta = h.meta.shape[0]
    tiles_per_batch = h.rows_per_batch // tm
    n_pad = h.rows_per_batch - n_meta - seq
    assert h.rows_per_batch % tm == 0 and n_pad <= tm and n_meta <= tm
    assert all(n % SUBLANES == 0 for n in (n_meta, n_pad, seq, tm))

    def start(i):
        b, t = i // tiles_per_batch, i % tiles_per_batch
        return pl.multiple_of(b * seq + jnp.clip(t * tm - n_meta, 0, seq - tm), SUBLANES)

    spec = pl.BlockSpec((pl.Element(tm), pl.Element(d)), lambda i, *_: (start(tile_of(i)), 0))
    return [h.x, h.meta], [spec, _resident(h.meta.shape)], (tiles_per_batch, n_pad)


def _stream_tile(stream_refs, layout, tile=None):
    if layout is None:
        return stream_refs[0][...]
    x_ref, meta_ref = stream_refs
    tiles_per_batch, n_pad = layout
    blk = x_ref[...]
    tm, d = blk.shape
    t = lax.rem(pl.program_id(0) if tile is None else tile, tiles_per_batch)
    first = jnp.concatenate([meta_ref[...], blk[:tm - meta_ref.shape[0]]], axis=0)
    last = jnp.concatenate([blk[n_pad:], jnp.zeros((n_pad, d), blk.dtype)], axis=0)
    return jnp.where(t == 0, first, jnp.where(t == tiles_per_batch - 1, last, blk))


def _cast_specs(w, layer, axis, block_size, step_of, n_steps):
    k, n = w.shape[1:]
    n_blocks = w.shape[axis] // block_size
    assert w.shape[axis] % block_size == 0 and n_blocks <= n_steps
    block = (block_size, n) if axis == 1 else (k, block_size)

    def index(*grid_ids):
        j = jnp.minimum(step_of(*grid_ids), n_blocks - 1)
        return (j, 0) if axis == 1 else (0, j)

    return (pl.BlockSpec((None,) + block, lambda *g: (layer,) + index(*g)), pl.BlockSpec(block, index),
            jax.ShapeDtypeStruct((k, n), BF16))


def _cast_blocks(cast_src, cast_dst):
    for src, dst in zip(cast_src, cast_dst):
        dst[...] = src[...].astype(BF16)


def _attn_kernel(*refs, n_pairs, n_casts):
    q_ref, k_ref, v_ref, tri_ref, g_ref = refs[:5]
    o_ref = refs[5 + n_casts]
    q2_ref, acc_ref, c_ref = refs[6 + 2 * n_casts:]
    _cast_blocks(refs[5:5 + n_casts], refs[6 + n_casts:6 + 2 * n_casts])
    t = SEQ_TILE
    qi = pl.program_id(1)
    pairs = range(n_pairs)
    lanes = [slice(p * LANES, (p + 1) * LANES) for p in pairs]
    first_head = lax.broadcasted_iota(jnp.int32, (t, LANES), 1) < LANES // 2
    row = lax.broadcasted_iota(jnp.int32, (2 * t, t), 0)
    col = lax.broadcasted_iota(jnp.int32, (2 * t, t), 1)
    causal = col < (row & (t - 1))
    zero = jnp.zeros((), BF16)
    for p in pairs:
        qp = q_ref[:, lanes[p]]
        q2_ref[p] = jnp.concatenate([jnp.where(first_head, qp, zero), jnp.where(first_head, zero, qp)], axis=0)

    def head_rows(ref, p, r):
        return ref[p] if r == t else jnp.concatenate([ref[p, :r], ref[p, t:t + r]], axis=0)

    def key_tile(kb, r, diagonal=False):
        start = pl.multiple_of(kb * t, t)
        z = [lax.dot_general(head_rows(q2_ref, p, r), k_ref[pl.ds(start, t), lanes[p]], (((1,), (1,)), ((), ())),
                             preferred_element_type=F32) for p in pairs]
        v = [v_ref[pl.ds(start, t), lanes[p]] for p in pairs]
        c_old = None if diagonal else [head_rows(c_ref, p, r) for p in pairs]
        log_beta, later, c_new = [], [], []
        for p in pairs:
            z_pos = jnp.maximum(z[p], 0.0)
            z_neg = z[p] - z_pos
            log_term = jnp.log(1.0 + jnp.exp2(z_neg - z_pos)) * LOG2_E
            soft = z_pos + log_term
            if diagonal:
                soft = jnp.where(causal, soft, 0.0)
            later.append(jnp.dot(soft.astype(BF16), tri_ref[...], preferred_element_type=F32))
            log_beta.append(z_neg - log_term)
            c = jnp.sum(soft, axis=1, keepdims=True)
            c_new.append(jnp.broadcast_to(c, soft.shape) if diagonal else c_old[p] + c)
        o = []
        for p in pairs:
            x = log_beta[p] - later[p]
            if not diagonal:
                x = x - c_old[p]
            w = jnp.exp2(x)
            if diagonal:
                w = jnp.where(causal, w, 0.0)
            o.append(jnp.dot(w.astype(BF16), v[p], preferred_element_type=F32))
        c_min = None
        for p in pairs:
            first = lax.broadcasted_iota(jnp.int32, (r, LANES), 1) < LANES // 2
            o_pair = jnp.where(first, o[p][:r], o[p][r:])
            if diagonal:
                acc_ref[:, lanes[p]] = o_pair
            else:
                acc_ref[:r, lanes[p]] += o_pair
            if r == t:
                c_ref[p] = c_new[p]
            else:
                c_ref[p, :r] = c_new[p][:r]
                c_ref[p, t:t + r] = c_new[p][r:]
            c_min = c_new[p] if c_min is None else jnp.minimum(c_min, c_new[p])
        if r < t:
            return jnp.min(c_min)
        top = jnp.minimum(jnp.min(c_min[:TOP_ROWS]), jnp.min(c_min[t:t + TOP_ROWS]))
        rest = jnp.minimum(jnp.min(c_min[TOP_ROWS:t]), jnp.min(c_min[t + TOP_ROWS:]))
        return top, rest

    def unfinished(c_min):
        return c_min < -LOG2_WEIGHT_FLOOR

    def full_body(s):
        kb, _, _ = s
        return (kb - 1,) + key_tile(kb, t)

    def top_body(s):
        kb, _ = s
        return kb - 1, key_tile(kb, TOP_ROWS)

    kb, top, _ = lax.while_loop(lambda s: jnp.logical_and(s[0] >= 0, unfinished(s[2])), full_body,
                                (qi - 1,) + key_tile(qi, t, diagonal=True))
    lax.while_loop(lambda s: jnp.logical_and(s[0] >= 0, unfinished(s[1])), top_body, (kb, top))
    o_ref[...] = _rms(acc_ref[...], g_ref[...]).astype(BF16)


def _attention(layer, q, k, v, g, batch, rows_per_batch, casts):
    rows, d_attn = q.shape
    t = SEQ_TILE
    nq = rows_per_batch // t
    j = jnp.arange(t)
    tri = (j[:, None] > j[None, :]).astype(BF16)
    n_pairs = d_attn // LANES
    kv_spec = pl.BlockSpec((rows_per_batch, d_attn), lambda b, i: (b, 0), pipeline_mode=pl.Buffered(1))
    tile_spec = pl.BlockSpec((t, d_attn), lambda b, i: (b * nq + i, 0))
    cast_specs = [_cast_specs(w, l, axis, ATTN_CAST_BLOCK, lambda b, i: b * nq + i, batch * nq) for w, l, axis in casts]
    return pl.pallas_call(
        functools.partial(_attn_kernel, n_pairs=n_pairs, n_casts=len(casts)),
        grid=(batch, nq),
        in_specs=[tile_spec, kv_spec, kv_spec, _resident(tri.shape), _layer_resident(layer, (1, d_attn))]
        + [c[0] for c in cast_specs],
        out_specs=[tile_spec] + [c[1] for c in cast_specs],
        out_shape=[jax.ShapeDtypeStruct((rows, d_attn), BF16)] + [c[2] for c in cast_specs],
        scratch_shapes=[pltpu.VMEM((n_pairs, 2 * t, LANES), BF16), pltpu.VMEM((t, d_attn), F32),
                        pltpu.VMEM((n_pairs, 2 * t, t), F32)],
        compiler_params=_params(2),
        name="sb_attention",
    )(q, k, v, tri, g, *[w for w, _, _ in casts])


def _digest(v):
    v = v.astype(F32)
    rows, cols = v.shape
    top = jnp.max(v.reshape(rows // SUBLANES, SUBLANES, cols), axis=0)
    return functools.reduce(jnp.maximum, [top[:, c:c + LANES] for c in range(0, cols, LANES)])


def _zero_after(digests):
    bits = pltpu.bitcast(functools.reduce(jnp.maximum, digests), jnp.uint32)
    zero = ((bits >> 16) >> 16).astype(F32)
    return jnp.concatenate([zero, zero], axis=0).astype(BF16)


def _proj_mix_kernel(*refs, n_stream, layout, n_casts, n_tiles, tiles_per_batch, d_attn, d_pool, d_conv, q_scale):
    ns = n_stream
    (g_ref, w_ref, w_pool_ref, pool_scale_ref, g_pool_ref, w_dw_ref, b_dw_ref, ln_g_ref, ln_b_ref, w_pw_ref,
     g_conv_ref) = refs[ns:ns + 11]
    cast_src = refs[ns + 11:ns + 11 + n_casts]
    q_ref, k_ref, v_ref, o_pool_ref, o_conv_ref = refs[ns + 11 + n_casts:ns + 16 + n_casts]
    cast_dst = refs[ns + 16 + n_casts:ns + 16 + 2 * n_casts]
    pool_ext, pool_sums, conv_ext, y_ref = refs[ns + 16 + 2 * n_casts:]
    i = pl.program_id(0)
    tile = jnp.minimum(i, n_tiles - 1)
    tm = q_ref.shape[0]

    @pl.when(i == 0)
    def _():
        pool_ext[...] = jnp.zeros_like(pool_ext)
        conv_ext[0] = jnp.zeros(conv_ext.shape[1:], F32)

    u = _rms(_stream_tile(refs[:ns], layout, tile), g_ref[...]).astype(BF16)

    def proj(lo, hi, wait_for=()):
        lhs = u
        if wait_for:
            head = jnp.concatenate([u[:16, :LANES] + _zero_after(wait_for), u[:16, LANES:]], axis=1)
            lhs = jnp.concatenate([head, u[16:]], axis=0)
        return jnp.dot(lhs, w_ref[:, lo:hi], preferred_element_type=F32)

    def mix_pool():
        n_pool = POOL_HALO + tm
        level, window_sum = pool_ext, []
        for s in range(len(POOL_WINDOWS)):
            start, shift, lo = SUBLANES * (s + 1), 2 ** s, s * LANES
            both = level[start:n_pool, lo:] + level[start - shift:n_pool - shift, lo:]
            window_sum.append(both[POOL_HALO - start:, :LANES])
            if s + 1 < len(POOL_WINDOWS):
                pool_sums[s, start:n_pool, lo:] = both
                level = pool_sums.at[s]
        ti = lax.rem(i + tiles_per_batch - 1, tiles_per_batch)
        pos = ti * tm + lax.broadcasted_iota(jnp.int32, (tm, 1), 0)
        mixed = []
        sq = jnp.zeros((tm, 1), F32)
        for gi, window in enumerate(POOL_WINDOWS):
            lanes = slice(gi * LANES, (gi + 1) * LANES)
            count = jnp.minimum(pos + 1, window).astype(F32)
            pooled = window_sum[gi] / count - pool_ext[POOL_HALO:, lanes]
            m = jnp.dot(pooled.astype(BF16), w_pool_ref[gi], preferred_element_type=F32) * pool_scale_ref[:, lanes]
            sq = sq + jnp.sum(m * m, axis=-1, keepdims=True)
            mixed.append(m)
        inv = lax.rsqrt(sq / d_pool + EPS)
        out = [(m * inv * g_pool_ref[:, gi * LANES:(gi + 1) * LANES]).astype(BF16) for gi, m in enumerate(mixed)]
        for gi, o in enumerate(out):
            o_pool_ref[:, gi * LANES:(gi + 1) * LANES] = o
        return [_digest(o) for o in out]

    n_ext = CONV_HALO + tm

    def conv_copies():
        digests = []
        for r in range(1, SUBLANES):
            shifted = conv_ext[0, r:r + n_ext - SUBLANES, :]
            conv_ext[r, :n_ext - SUBLANES, :] = shifted
            digests.append(_digest(shifted))
        return digests

    first_tap = CONV_HALO - (CONV_WIDTH - 1)

    def conv_rows(base):
        acc = jnp.broadcast_to(b_dw_ref[...], (CONV_CHUNK, d_conv))
        for tap in range(CONV_WIDTH):
            shift = (first_tap + tap) % SUBLANES
            lo = base + first_tap + tap - shift
            acc = acc + w_dw_ref[tap:tap + 1, :] * conv_ext[shift, lo:lo + CONV_CHUNK, :]
        y_ref[base:base + CONV_CHUNK, :] = acc
        return _digest(acc)

    def conv_finish():
        y = y_ref[...]
        mu = jnp.mean(y, axis=-1, keepdims=True)
        yc = y - mu
        var = jnp.mean(yc * yc, axis=-1, keepdims=True)
        y = yc * lax.rsqrt(var + EPS) * ln_g_ref[...] + ln_b_ref[...]
        y = y * jax.nn.sigmoid(y)
        o = _rms(jnp.dot(y.astype(BF16), w_pw_ref[...], preferred_element_type=F32), g_conv_ref[...]).astype(BF16)
        o_conv_ref[...] = o
        return [_digest(o)]

    half = d_attn // 2
    chunks = list(range(0, tm, CONV_CHUNK))
    per_group = -(-len(chunks) // 4)
    conv_group = lambda g: [conv_rows(base) for base in chunks[g * per_group:(g + 1) * per_group]]
    q_ref[:, :half] = (proj(0, half) * q_scale).astype(BF16)
    q_ref[:, half:] = (proj(half, d_attn) * q_scale).astype(BF16)
    done = mix_pool()
    k_ref[:, :half] = proj(d_attn, d_attn + half, done).astype(BF16)
    done = conv_copies()
    k_ref[:, half:] = proj(d_attn + half, 2 * d_attn, done).astype(BF16)
    done = conv_group(0)
    v_ref[:, :half] = proj(2 * d_attn, 2 * d_attn + half, done).astype(BF16)
    done = conv_group(1)
    v_ref[:, half:] = proj(2 * d_attn + half, 3 * d_attn, done).astype(BF16)
    done = conv_group(2)
    o = 3 * d_attn
    new_pool = proj(o, o + d_pool, done)
    done = conv_group(3)
    o += d_pool
    conv_value = proj(o, o + d_conv, done)
    done = conv_finish()
    new_conv = conv_value * jax.nn.sigmoid(proj(o + d_conv, o + 2 * d_conv, done))
    _cast_blocks(cast_src, cast_dst)

    starts_batch = lax.rem(tile, tiles_per_batch) == 0
    pool_ext[:POOL_HALO, :] = jnp.where(starts_batch, 0.0, pool_ext[tm:, :])
    pool_ext[POOL_HALO:, :] = new_pool
    conv_ext[0, :CONV_HALO, :] = jnp.where(starts_batch, 0.0, conv_ext[0, tm:, :])
    conv_ext[0, CONV_HALO:, :] = new_conv


def _proj_mix(layer, h, g, w, w_pool, pool_scale, g_pool, w_dw, b_dw, ln_g, ln_b, w_pw, g_conv, d_attn,
              rows_per_batch, casts):
    rows, d = h.shape
    d_pool, d_conv = pool_scale.shape[-1], w_pw.shape[-1]
    tm = PROJ_ROWS
    n = rows // tm
    head_dim = d_attn // N_HEADS_SB
    tile_of = lambda i: jnp.minimum(i, n - 1)
    cur = lambda width: pl.BlockSpec((tm, width), lambda i: (tile_of(i), 0))
    prev = lambda width: pl.BlockSpec((tm, width), lambda i: (jnp.maximum(i - 1, 0), 0))
    lr = functools.partial(_layer_resident, layer)
    stream, stream_specs, layout = _stream_inputs(h, tm, tile_of)
    cast_specs = [_cast_specs(w_, l, axis, PROJ_CAST_BLOCK, lambda i: i, n + 1) for w_, l, axis in casts]
    return pl.pallas_call(
        functools.partial(_proj_mix_kernel, n_stream=len(stream), layout=layout, n_casts=len(casts), n_tiles=n,
                          tiles_per_batch=rows_per_batch // tm, d_attn=d_attn, d_pool=d_pool, d_conv=d_conv,
                          q_scale=LOG2_E * head_dim ** -0.5),
        grid=(n + 1,),
        in_specs=stream_specs + [lr((1, d)), _resident(w.shape), lr(w_pool.shape[1:]), lr((1, d_pool)),
                                 lr((1, d_pool)), lr(w_dw.shape[1:]), lr((1, d_conv)), lr((1, d_conv)),
                                 lr((1, d_conv)), lr(w_pw.shape[1:]), lr((1, d_conv))] + [c[0] for c in cast_specs],
        out_specs=[cur(d_attn), cur(d_attn), cur(d_attn), prev(d_pool), prev(d_conv)] + [c[1] for c in cast_specs],
        out_shape=[jax.ShapeDtypeStruct((rows, d_attn), BF16)] * 3
        + [jax.ShapeDtypeStruct((rows, d_pool), BF16), jax.ShapeDtypeStruct((rows, d_conv), BF16)]
        + [c[2] for c in cast_specs],
        scratch_shapes=[pltpu.VMEM((POOL_HALO + tm, d_pool), F32),
                        pltpu.VMEM((len(POOL_WINDOWS) - 1, POOL_HALO + tm, d_pool), F32),
                        pltpu.VMEM((SUBLANES, CONV_HALO + tm, d_conv), F32), pltpu.VMEM((tm, d_conv), F32)],
        compiler_params=_params(1),
        name="in_proj_mixers",
    )(*stream, g, w, w_pool, pool_scale, g_pool, w_dw, b_dw, ln_g, ln_b, w_pw, g_conv, *[w_ for w_, _, _ in casts])


def _out_proj_kernel(*refs, n_stream, layout, n_casts):
    attn_ref, pool_ref, conv_ref, w_ref, g_ref = refs[n_stream:n_stream + 5]
    o_ref = refs[n_stream + 5 + n_casts]
    _cast_blocks(refs[n_stream + 5:n_stream + 5 + n_casts], refs[n_stream + 6 + n_casts:])
    merged = jnp.concatenate([attn_ref[...], pool_ref[...], conv_ref[...]], axis=1)
    y = jnp.dot(merged, w_ref[...], preferred_element_type=F32)
    o_ref[...] = _stream_tile(refs[:n_stream], layout) + _rms(y, g_ref[...])


def _out_proj(layer, attn, pool, conv, h, w, g, casts):
    rows, d = h.shape
    tm = PROJ_ROWS
    row_spec = lambda n: pl.BlockSpec((tm, n), lambda i: (i, 0))
    stream, stream_specs, layout = _stream_inputs(h, tm)
    cast_specs = [_cast_specs(w_, l, axis, PROJ_CAST_BLOCK, lambda i: i, rows // tm) for w_, l, axis in casts]
    return pl.pallas_call(
        functools.partial(_out_proj_kernel, n_stream=len(stream), layout=layout, n_casts=len(casts)),
        grid=(rows // tm,),
        in_specs=stream_specs + [row_spec(attn.shape[1]), row_spec(pool.shape[1]), row_spec(conv.shape[1]),
                                 _resident(w.shape), _layer_resident(layer, (1, d))] + [c[0] for c in cast_specs],
        out_specs=[row_spec(d)] + [c[1] for c in cast_specs],
        out_shape=[jax.ShapeDtypeStruct((rows, d), F32)] + [c[2] for c in cast_specs],
        compiler_params=_params(1),
        name="out_proj",
    )(*stream, attn, pool, conv, w, g, *[w_ for w_, _, _ in casts])


def _ffn_kernel(h_ref, g_pre_ref, w_gate_ref, w_up_ref, w_down_ref, g_post_ref, o_ref, u_ref, acc_ref):
    c = pl.program_id(1)

    @pl.when(c == 0)
    def _():
        u_ref[...] = _rms(h_ref[...], g_pre_ref[...]).astype(BF16)
        acc_ref[...] = jnp.zeros_like(acc_ref)

    u = u_ref[...]
    gate = jnp.dot(u, w_gate_ref[...], preferred_element_type=F32)
    up = jnp.dot(u, w_up_ref[...], preferred_element_type=F32)
    a = (gate * jax.nn.sigmoid(gate) * up).astype(BF16)
    tf = w_down_ref.shape[0]
    for n in range(0, acc_ref.shape[1], tf):
        acc_ref[:, n:n + tf] += jnp.dot(a, w_down_ref[:, n:n + tf], preferred_element_type=F32)

    @pl.when(c == pl.num_programs(1) - 1)
    def _():
        o_ref[...] = h_ref[...] + _rms(acc_ref[...], g_post_ref[...])


def _ffn(layer, h, g_pre, w_gate, w_up, w_down, g_post, keep=None):
    rows, d = h.shape
    d_ff = w_gate.shape[1]
    tf = FFN_COLS
    if keep is None:
        tm, n_tiles = FFN_ROWS, rows // FFN_ROWS
        h_spec = pl.BlockSpec((tm, d), lambda i, c: (i, 0))
    else:
        rows_per_batch, first, count = keep
        tm = LAST_FFN_ROWS
        per_batch = count // tm
        n_tiles = (rows // rows_per_batch) * per_batch
        h_spec = pl.BlockSpec((pl.Element(tm), pl.Element(d)),
                              lambda i, c: (pl.multiple_of(
                                  (i // per_batch) * rows_per_batch + first + (i % per_batch) * tm, SUBLANES), 0))
        assert rows_per_batch % SUBLANES == 0 and first % SUBLANES == 0
    return pl.pallas_call(
        _ffn_kernel,
        grid=(n_tiles, d_ff // tf),
        in_specs=[h_spec, _layer_resident(layer, (1, d)),
                  pl.BlockSpec((d, tf), lambda i, c: (0, c)), pl.BlockSpec((d, tf), lambda i, c: (0, c)),
                  pl.BlockSpec((tf, d), lambda i, c: (c, 0)), _layer_resident(layer, (1, d))],
        out_specs=pl.BlockSpec((tm, d), lambda i, c: (i, 0)),
        out_shape=jax.ShapeDtypeStruct((n_tiles * tm, d), F32),
        scratch_shapes=[pltpu.VMEM((tm, d), BF16), pltpu.VMEM((tm, d), F32)],
        compiler_params=_params(2),
        name="swiglu_ffn",
    )(h, g_pre, w_gate, w_up, w_down, g_post)


def kernel(x, meta_tokens, pre_mix_g, w_in, w_pool, pool_scale, w_dw, b_dw, conv_ln_g, conv_ln_b, w_pw, mix_out_g,
           w_out, post_mix_g, pre_ffn_g, w_gate, w_up, w_down, post_ffn_g):
    batch, seq, d = x.shape
    depth = w_in.shape[0]
    d_pool = pool_scale.shape[1]
    d_conv = w_pw.shape[1]
    d_attn = w_out.shape[1] - d_pool - d_conv
    assert d_pool == LANES * len(POOL_WINDOWS) and d_attn % (2 * LANES) == 0
    assert POOL_WINDOWS == tuple(2 ** (g + 1) for g in range(len(POOL_WINDOWS)))
    assert d_attn // N_HEADS_SB == LANES // 2

    length = N_META + seq
    rows_per_batch = -(-length // ROW_ALIGN) * ROW_ALIGN
    assert rows_per_batch % SEQ_TILE == 0 and (batch * rows_per_batch) % FFN_ROWS == 0 and seq % LAST_FFN_ROWS == 0
    assert meta_tokens.shape[0] == N_META
    h = _Tokens(x.reshape(batch * seq, d), meta_tokens.astype(x.dtype), seq, rows_per_batch)

    w_pool, w_pw = w_pool.astype(BF16), w_pw.astype(BF16)
    w_in_l = w_in[0].astype(BF16)
    vec = lambda a: a[:, None, :]
    pre_mix_g, pool_scale, b_dw, conv_ln_g, conv_ln_b, post_mix_g, pre_ffn_g, post_ffn_g = (
        vec(a) for a in (pre_mix_g, pool_scale, b_dw, conv_ln_g, conv_ln_b, post_mix_g, pre_ffn_g, post_ffn_g))
    g_attn, g_pool, g_conv = (vec(a) for a in jnp.split(mix_out_g, [d_attn, d_attn + d_pool], axis=1))

    for i in range(depth):
        q, k, v, o_pool, o_conv, w_gate_l = _proj_mix(i, h, pre_mix_g, w_in_l, w_pool, pool_scale, g_pool, w_dw, b_dw,
                                                      conv_ln_g, conv_ln_b, w_pw, g_conv, d_attn, rows_per_batch,
                                                      [(w_gate, i, 2)])
        o_attn, w_out_l, *w_in_next = _attention(i, q, k, v, g_attn, batch, rows_per_batch,
                                                 [(w_out, i, 2)] + ([(w_in, i + 1, 2)] if i + 1 < depth else []))
        if w_in_next:
            w_in_l = w_in_next[0]
        h, w_up_l, w_down_l = _out_proj(i, o_attn, o_pool, o_conv, h, w_out_l, post_mix_g,
                                        [(w_up, i, 2), (w_down, i, 1)])
        keep = (rows_per_batch, N_META, seq) if i == depth - 1 else None
        h = _ffn(i, h, pre_ffn_g, w_gate_l, w_up_l, w_down_l, post_ffn_g, keep)

    return h.reshape(batch, seq, d)
```

```python
import functools
from typing import NamedTuple

import jax
import jax.numpy as jnp
from jax import lax
from jax.experimental import pallas as pl
from jax.experimental.pallas import tpu as pltpu

N_META = 16
N_HEADS_SB = 16
POOL_WINDOWS = (2, 4, 8, 16)
CONV_WIDTH = 31
EPS = 1e-6

LANES = 128
SUBLANES = 8
VMEM_BYTES_V7X = 64 * 1024 * 1024
VMEM_LIMIT = VMEM_BYTES_V7X - 8 * 1024 * 1024

SEQ_TILE = 128
TOP_ROWS = 32
ROW_ALIGN = 384
PROJ_ROWS = 384
FFN_ROWS = 768
LAST_FFN_ROWS = 512
FFN_COLS = 512
CONV_HALO = 32
POOL_HALO = SUBLANES * len(POOL_WINDOWS)
CONV_CHUNK = 32
ATTN_CAST_BLOCK = 128
PROJ_CAST_BLOCK = 256

LOG2_E = 1.4426950408889634
LOG2_WEIGHT_FLOOR = -127.0

F32 = jnp.float32
BF16 = jnp.bfloat16


def _rms(x, g):
    return x * lax.rsqrt(jnp.mean(x * x, axis=-1, keepdims=True) + EPS) * g


def _resident(shape):
    return pl.BlockSpec(shape, lambda *_: (0,) * len(shape), pipeline_mode=pl.Buffered(1))


def _layer_resident(layer, shape):
    return pl.BlockSpec((None,) + tuple(shape), lambda *_: (layer,) + (0,) * len(shape),
                        pipeline_mode=pl.Buffered(1))


def _params(n_grid_axes):
    return pltpu.CompilerParams(dimension_semantics=("arbitrary",) * n_grid_axes, vmem_limit_bytes=VMEM_LIMIT)


class _Tokens(NamedTuple):
    x: jax.Array
    meta: jax.Array
    seq: int
    rows_per_batch: int

    @property
    def shape(self):
        return (self.x.shape[0] // self.seq * self.rows_per_batch, self.x.shape[1])


def _stream_inputs(h, tm, tile_of=lambda i: i):
    if not isinstance(h, _Tokens):
        return [h], [pl.BlockSpec((tm, h.shape[1]), lambda i, *_: (tile_of(i), 0))], None
    seq, d = h.seq, h.x.shape[1]
    n_meta = h.meta.shape[0]
    tiles_per_batch = h.rows_per_batch // tm
    n_pad = h.rows_per_batch - n_meta - seq
    assert h.rows_per_batch % tm == 0 and n_pad <= tm and n_meta <= tm
    assert all(n % SUBLANES == 0 for n in (n_meta, n_pad, seq, tm))

    def start(i):
        b, t = i // tiles_per_batch, i % tiles_per_batch
        return pl.multiple_of(b * seq + jnp.clip(t * tm - n_meta, 0, seq - tm), SUBLANES)

    spec = pl.BlockSpec((pl.Element(tm), pl.Element(d)), lambda i, *_: (start(tile_of(i)), 0))
    return [h.x, h.meta], [spec, _resident(h.meta.shape)], (tiles_per_batch, n_pad)


def _stream_tile(stream_refs, layout, tile=None):
    if layout is None:
        return stream_refs[0][...]
    x_ref, meta_ref = stream_refs
    tiles_per_batch, n_pad = layout
    blk = x_ref[...]
    tm, d = blk.shape
    t = lax.rem(pl.program_id(0) if tile is None else tile, tiles_per_batch)
    first = jnp.concatenate([meta_ref[...], blk[:tm - meta_ref.shape[0]]], axis=0)
    last = jnp.concatenate([blk[n_pad:], jnp.zeros((n_pad, d), blk.dtype)], axis=0)
    return jnp.where(t == 0, first, jnp.where(t == tiles_per_batch - 1, last, blk))


def _cast_specs(w, layer, axis, block_size, step_of, n_steps):
    k, n = w.shape[1:]
    n_blocks = w.shape[axis] // block_size
    assert w.shape[axis] % block_size == 0 and n_blocks <= n_steps
    block = (block_size, n) if axis == 1 else (k, block_size)

    def index(*grid_ids):
        j = jnp.minimum(step_of(*grid_ids), n_blocks - 1)
        return (j, 0) if axis == 1 else (0, j)

    return (pl.BlockSpec((None,) + block, lambda *g: (layer,) + index(*g)), pl.BlockSpec(block, index),
            jax.ShapeDtypeStruct((k, n), BF16))


def _cast_blocks(cast_src, cast_dst):
    for src, dst in zip(cast_src, cast_dst):
        dst[...] = src[...].astype(BF16)


def _attn_kernel(*refs, n_pairs, n_casts):
    q_ref, k_ref, v_ref, tri_ref, g_ref = refs[:5]
    o_ref = refs[5 + n_casts]
    q2_ref, acc_ref, c_ref = refs[6 + 2 * n_casts:]
    _cast_blocks(refs[5:5 + n_casts], refs[6 + n_casts:6 + 2 * n_casts])
    t = SEQ_TILE
    qi = pl.program_id(1)
    pairs = range(n_pairs)
    lanes = [slice(p * LANES, (p + 1) * LANES) for p in pairs]
    first_head = lax.broadcasted_iota(jnp.int32, (t, LANES), 1) < LANES // 2
    row = lax.broadcasted_iota(jnp.int32, (2 * t, t), 0)
    col = lax.broadcasted_iota(jnp.int32, (2 * t, t), 1)
    causal = col < (row & (t - 1))
    zero = jnp.zeros((), BF16)
    for p in pairs:
        qp = q_ref[:, lanes[p]]
        q2_ref[p] = jnp.concatenate([jnp.where(first_head, qp, zero), jnp.where(first_head, zero, qp)], axis=0)

    def head_rows(ref, p, r):
        return ref[p] if r == t else jnp.concatenate([ref[p, :r], ref[p, t:t + r]], axis=0)

    def key_tile(kb, r, diagonal=False):
        start = pl.multiple_of(kb * t, t)
        z = [lax.dot_general(head_rows(q2_ref, p, r), k_ref[pl.ds(start, t), lanes[p]], (((1,), (1,)), ((), ())),
                             preferred_element_type=F32) for p in pairs]
        v = [v_ref[pl.ds(start, t), lanes[p]] for p in pairs]
        c_old = None if diagonal else [head_rows(c_ref, p, r) for p in pairs]
        log_beta, later, c_new = [], [], []
        for p in pairs:
            z_pos = jnp.maximum(z[p], 0.0)
            z_neg = z[p] - z_pos
            log_term = jnp.log(1.0 + jnp.exp2(z_neg - z_pos)) * LOG2_E
            soft = z_pos + log_term
            if diagonal:
                soft = jnp.where(causal, soft, 0.0)
            later.append(jnp.dot(soft.astype(BF16), tri_ref[...], preferred_element_type=F32))
            log_beta.append(z_neg - log_term)
            c = jnp.sum(soft, axis=1, keepdims=True)
            c_new.append(jnp.broadcast_to(c, soft.shape) if diagonal else c_old[p] + c)
        o = []
        for p in pairs:
            x = log_beta[p] - later[p]
            if not diagonal:
                x = x - c_old[p]
            w = jnp.exp2(x)
            if diagonal:
                w = jnp.where(causal, w, 0.0)
            o.append(jnp.dot(w.astype(BF16), v[p], preferred_element_type=F32))
        c_min = None
        for p in pairs:
            first = lax.broadcasted_iota(jnp.int32, (r, LANES), 1) < LANES // 2
            o_pair = jnp.where(first, o[p][:r], o[p][r:])
            if diagonal:
                acc_ref[:, lanes[p]] = o_pair
            else:
                acc_ref[:r, lanes[p]] += o_pair
            if r == t:
                c_ref[p] = c_new[p]
            else:
                c_ref[p, :r] = c_new[p][:r]
                c_ref[p, t:t + r] = c_new[p][r:]
            c_min = c_new[p] if c_min is None else jnp.minimum(c_min, c_new[p])
        if r < t:
            return jnp.min(c_min)
        top = jnp.minimum(jnp.min(c_min[:TOP_ROWS]), jnp.min(c_min[t:t + TOP_ROWS]))
        rest = jnp.minimum(jnp.min(c_min[TOP_ROWS:t]), jnp.min(c_min[t + TOP_ROWS:]))
        return top, rest

    def unfinished(c_min):
        return c_min < -LOG2_WEIGHT_FLOOR

    def full_body(s):
        kb, _, _ = s
        return (kb - 1,) + key_tile(kb, t)

    def top_body(s):
        kb, _ = s
        return kb - 1, key_tile(kb, TOP_ROWS)

    kb, top, _ = lax.while_loop(lambda s: jnp.logical_and(s[0] >= 0, unfinished(s[2])), full_body,
                                (qi - 1,) + key_tile(qi, t, diagonal=True))
    lax.while_loop(lambda s: jnp.logical_and(s[0] >= 0, unfinished(s[1])), top_body, (kb, top))
    o_ref[...] = _rms(acc_ref[...], g_ref[...]).astype(BF16)


def _attention(layer, q, k, v, g, batch, rows_per_batch, casts):
    rows, d_attn = q.shape
    t = SEQ_TILE
    nq = rows_per_batch // t
    j = jnp.arange(t)
    tri = (j[:, None] > j[None, :]).astype(BF16)
    n_pairs = d_attn // LANES
    kv_spec = pl.BlockSpec((rows_per_batch, d_attn), lambda b, i: (b, 0), pipeline_mode=pl.Buffered(1))
    tile_spec = pl.BlockSpec((t, d_attn), lambda b, i: (b * nq + i, 0))
    cast_specs = [_cast_specs(w, l, axis, ATTN_CAST_BLOCK, lambda b, i: b * nq + i, batch * nq) for w, l, axis in casts]
    return pl.pallas_call(
        functools.partial(_attn_kernel, n_pairs=n_pairs, n_casts=len(casts)),
        grid=(batch, nq),
        in_specs=[tile_spec, kv_spec, kv_spec, _resident(tri.shape), _layer_resident(layer, (1, d_attn))]
        + [c[0] for c in cast_specs],
        out_specs=[tile_spec] + [c[1] for c in cast_specs],
        out_shape=[jax.ShapeDtypeStruct((rows, d_attn), BF16)] + [c[2] for c in cast_specs],
        scratch_shapes=[pltpu.VMEM((n_pairs, 2 * t, LANES), BF16), pltpu.VMEM((t, d_attn), F32),
                        pltpu.VMEM((n_pairs, 2 * t, t), F32)],
        compiler_params=_params(2),
        name="sb_attention",
    )(q, k, v, tri, g, *[w for w, _, _ in casts])


def _digest(v):
    v = v.astype(F32)
    rows, cols = v.shape
    top = jnp.max(v.reshape(rows // SUBLANES, SUBLANES, cols), axis=0)
    return functools.reduce(jnp.maximum, [top[:, c:c + LANES] for c in range(0, cols, LANES)])


def _zero_after(digests):
    bits = pltpu.bitcast(functools.reduce(jnp.maximum, digests), jnp.uint32)
    zero = ((bits >> 16) >> 16).astype(F32)
    return jnp.concatenate([zero, zero], axis=0).astype(BF16)


def _proj_mix_kernel(*refs, n_stream, layout, n_casts, n_tiles, tiles_per_batch, d_attn, d_pool, d_conv, q_scale):
    ns = n_stream
    (g_ref, w_ref, w_pool_ref, pool_scale_ref, g_pool_ref, w_dw_ref, b_dw_ref, ln_g_ref, ln_b_ref, w_pw_ref,
     g_conv_ref) = refs[ns:ns + 11]
    cast_src = refs[ns + 11:ns + 11 + n_casts]
    q_ref, k_ref, v_ref, o_pool_ref, o_conv_ref = refs[ns + 11 + n_casts:ns + 16 + n_casts]
    cast_dst = refs[ns + 16 + n_casts:ns + 16 + 2 * n_casts]
    pool_ext, pool_sums, conv_ext, y_ref = refs[ns + 16 + 2 * n_casts:]
    i = pl.program_id(0)
    tile = jnp.minimum(i, n_tiles - 1)
    tm = q_ref.shape[0]

    @pl.when(i == 0)
    def _():
        pool_ext[...] = jnp.zeros_like(pool_ext)
        conv_ext[0] = jnp.zeros(conv_ext.shape[1:], F32)

    u = _rms(_stream_tile(refs[:ns], layout, tile), g_ref[...]).astype(BF16)

    def proj(lo, hi, wait_for=()):
        lhs = u
        if wait_for:
            head = jnp.concatenate([u[:16, :LANES] + _zero_after(wait_for), u[:16, LANES:]], axis=1)
            lhs = jnp.concatenate([head, u[16:]], axis=0)
        return jnp.dot(lhs, w_ref[:, lo:hi], preferred_element_type=F32)

    def mix_pool():
        n_pool = POOL_HALO + tm
        level, window_sum = pool_ext, []
        for s in range(len(POOL_WINDOWS)):
            start, shift, lo = SUBLANES * (s + 1), 2 ** s, s * LANES
            both = level[start:n_pool, lo:] + level[start - shift:n_pool - shift, lo:]
            window_sum.append(both[POOL_HALO - start:, :LANES])
            if s + 1 < len(POOL_WINDOWS):
                pool_sums[s, start:n_pool, lo:] = both
                level = pool_sums.at[s]
        ti = lax.rem(i + tiles_per_batch - 1, tiles_per_batch)
        pos = ti * tm + lax.broadcasted_iota(jnp.int32, (tm, 1), 0)
        mixed = []
        sq = jnp.zeros((tm, 1), F32)
        for gi, window in enumerate(POOL_WINDOWS):
            lanes = slice(gi * LANES, (gi + 1) * LANES)
            count = jnp.minimum(pos + 1, window).astype(F32)
            pooled = window_sum[gi] / count - pool_ext[POOL_HALO:, lanes]
            m = jnp.dot(pooled.astype(BF16), w_pool_ref[gi], preferred_element_type=F32) * pool_scale_ref[:, lanes]
            sq = sq + jnp.sum(m * m, axis=-1, keepdims=True)
            mixed.append(m)
        inv = lax.rsqrt(sq / d_pool + EPS)
        out = [(m * inv * g_pool_ref[:, gi * LANES:(gi + 1) * LANES]).astype(BF16) for gi, m in enumerate(mixed)]
        for gi, o in enumerate(out):
            o_pool_ref[:, gi * LANES:(gi + 1) * LANES] = o
        return [_digest(o) for o in out]

    n_ext = CONV_HALO + tm

    def conv_copies():
        digests = []
        for r in range(1, SUBLANES):
            shifted = conv_ext[0, r:r + n_ext - SUBLANES, :]
            conv_ext[r, :n_ext - SUBLANES, :] = shifted
            digests.append(_digest(shifted))
        return digests

    first_tap = CONV_HALO - (CONV_WIDTH - 1)

    def conv_rows(base):
        acc = jnp.broadcast_to(b_dw_ref[...], (CONV_CHUNK, d_conv))
        for tap in range(CONV_WIDTH):
            shift = (first_tap + tap) % SUBLANES
            lo = base + first_tap + tap - shift
            acc = acc + w_dw_ref[tap:tap + 1, :] * conv_ext[shift, lo:lo + CONV_CHUNK, :]
        y_ref[base:base + CONV_CHUNK, :] = acc
        return _digest(acc)

    def conv_finish():
        y = y_ref[...]
        mu = jnp.mean(y, axis=-1, keepdims=True)
        yc = y - mu
        var = jnp.mean(yc * yc, axis=-1, keepdims=True)
        y = yc * lax.rsqrt(var + EPS) * ln_g_ref[...] + ln_b_ref[...]
        y = y * jax.nn.sigmoid(y)
        o = _rms(jnp.dot(y.astype(BF16), w_pw_ref[...], preferred_element_type=F32), g_conv_ref[...]).astype(BF16)
        o_conv_ref[...] = o
        return [_digest(o)]

    half = d_attn // 2
    chunks = list(range(0, tm, CONV_CHUNK))
    per_group = -(-len(chunks) // 4)
    conv_group = lambda g: [conv_rows(base) for base in chunks[g * per_group:(g + 1) * per_group]]
    q_ref[:, :half] = (proj(0, half) * q_scale).astype(BF16)
    q_ref[:, half:] = (proj(half, d_attn) * q_scale).astype(BF16)
    done = mix_pool()
    k_ref[:, :half] = proj(d_attn, d_attn + half, done).astype(BF16)
    done = conv_copies()
    k_ref[:, half:] = proj(d_attn + half, 2 * d_attn, done).astype(BF16)
    done = conv_group(0)
    v_ref[:, :half] = proj(2 * d_attn, 2 * d_attn + half, done).astype(BF16)
    done = conv_group(1)
    v_ref[:, half:] = proj(2 * d_attn + half, 3 * d_attn, done).astype(BF16)
    done = conv_group(2)
    o = 3 * d_attn
    new_pool = proj(o, o + d_pool, done)
    done = conv_group(3)
    o += d_pool
    conv_value = proj(o, o + d_conv, done)
    done = conv_finish()
    new_conv = conv_value * jax.nn.sigmoid(proj(o + d_conv, o + 2 * d_conv, done))
    _cast_blocks(cast_src, cast_dst)

    starts_batch = lax.rem(tile, tiles_per_batch) == 0
    pool_ext[:POOL_HALO, :] = jnp.where(starts_batch, 0.0, pool_ext[tm:, :])
    pool_ext[POOL_HALO:, :] = new_pool
    conv_ext[0, :CONV_HALO, :] = jnp.where(starts_batch, 0.0, conv_ext[0, tm:, :])
    conv_ext[0, CONV_HALO:, :] = new_conv


def _proj_mix(layer, h, g, w, w_pool, pool_scale, g_pool, w_dw, b_dw, ln_g, ln_b, w_pw, g_conv, d_attn,
              rows_per_batch, casts):
    rows, d = h.shape
    d_pool, d_conv = pool_scale.shape[-1], w_pw.shape[-1]
    tm = PROJ_ROWS
    n = rows // tm
    head_dim = d_attn // N_HEADS_SB
    tile_of = lambda i: jnp.minimum(i, n - 1)
    cur = lambda width: pl.BlockSpec((tm, width), lambda i: (tile_of(i), 0))
    prev = lambda width: pl.BlockSpec((tm, width), lambda i: (jnp.maximum(i - 1, 0), 0))
    lr = functools.partial(_layer_resident, layer)
    stream, stream_specs, layout = _stream_inputs(h, tm, tile_of)
    cast_specs = [_cast_specs(w_, l, axis, PROJ_CAST_BLOCK, lambda i: i, n + 1) for w_, l, axis in casts]
    return pl.pallas_call(
        functools.partial(_proj_mix_kernel, n_stream=len(stream), layout=layout, n_casts=len(casts), n_tiles=n,
                          tiles_per_batch=rows_per_batch // tm, d_attn=d_attn, d_pool=d_pool, d_conv=d_conv,
                          q_scale=LOG2_E * head_dim ** -0.5),
        grid=(n + 1,),
        in_specs=stream_specs + [lr((1, d)), _resident(w.shape), lr(w_pool.shape[1:]), lr((1, d_pool)),
                                 lr((1, d_pool)), lr(w_dw.shape[1:]), lr((1, d_conv)), lr((1, d_conv)),
                                 lr((1, d_conv)), lr(w_pw.shape[1:]), lr((1, d_conv))] + [c[0] for c in cast_specs],
        out_specs=[cur(d_attn), cur(d_attn), cur(d_attn), prev(d_pool), prev(d_conv)] + [c[1] for c in cast_specs],
        out_shape=[jax.ShapeDtypeStruct((rows, d_attn), BF16)] * 3
        + [jax.ShapeDtypeStruct((rows, d_pool), BF16), jax.ShapeDtypeStruct((rows, d_conv), BF16)]
        + [c[2] for c in cast_specs],
        scratch_shapes=[pltpu.VMEM((POOL_HALO + tm, d_pool), F32),
                        pltpu.VMEM((len(POOL_WINDOWS) - 1, POOL_HALO + tm, d_pool), F32),
                        pltpu.VMEM((SUBLANES, CONV_HALO + tm, d_conv), F32), pltpu.VMEM((tm, d_conv), F32)],
        compiler_params=_params(1),
        name="in_proj_mixers",
    )(*stream, g, w, w_pool, pool_scale, g_pool, w_dw, b_dw, ln_g, ln_b, w_pw, g_conv, *[w_ for w_, _, _ in casts])


def _out_proj_kernel(*refs, n_stream, layout, n_casts):
    attn_ref, pool_ref, conv_ref, w_ref, g_ref = refs[n_stream:n_stream + 5]
    o_ref = refs[n_stream + 5 + n_casts]
    _cast_blocks(refs[n_stream + 5:n_stream + 5 + n_casts], refs[n_stream + 6 + n_casts:])
    merged = jnp.concatenate([attn_ref[...], pool_ref[...], conv_ref[...]], axis=1)
    y = jnp.dot(merged, w_ref[...], preferred_element_type=F32)
    o_ref[...] = _stream_tile(refs[:n_stream], layout) + _rms(y, g_ref[...])


def _out_proj(layer, attn, pool, conv, h, w, g, casts):
    rows, d = h.shape
    tm = PROJ_ROWS
    row_spec = lambda n: pl.BlockSpec((tm, n), lambda i: (i, 0))
    stream, stream_specs, layout = _stream_inputs(h, tm)
    cast_specs = [_cast_specs(w_, l, axis, PROJ_CAST_BLOCK, lambda i: i, rows // tm) for w_, l, axis in casts]
    return pl.pallas_call(
        functools.partial(_out_proj_kernel, n_stream=len(stream), layout=layout, n_casts=len(casts)),
        grid=(rows // tm,),
        in_specs=stream_specs + [row_spec(attn.shape[1]), row_spec(pool.shape[1]), row_spec(conv.shape[1]),
                                 _resident(w.shape), _layer_resident(layer, (1, d))] + [c[0] for c in cast_specs],
        out_specs=[row_spec(d)] + [c[1] for c in cast_specs],
        out_shape=[jax.ShapeDtypeStruct((rows, d), F32)] + [c[2] for c in cast_specs],
        compiler_params=_params(1),
        name="out_proj",
    )(*stream, attn, pool, conv, w, g, *[w_ for w_, _, _ in casts])


def _ffn_kernel(h_ref, g_pre_ref, w_gate_ref, w_up_ref, w_down_ref, g_post_ref, o_ref, u_ref, acc_ref):
    c = pl.program_id(1)

    @pl.when(c == 0)
    def _():
        u_ref[...] = _rms(h_ref[...], g_pre_ref[...]).astype(BF16)
        acc_ref[...] = jnp.zeros_like(acc_ref)

    u = u_ref[...]
    gate = jnp.dot(u, w_gate_ref[...], preferred_element_type=F32)
    up = jnp.dot(u, w_up_ref[...], preferred_element_type=F32)
    a = (gate * jax.nn.sigmoid(gate) * up).astype(BF16)
    tf = w_down_ref.shape[0]
    for n in range(0, acc_ref.shape[1], tf):
        acc_ref[:, n:n + tf] += jnp.dot(a, w_down_ref[:, n:n + tf], preferred_element_type=F32)

    @pl.when(c == pl.num_programs(1) - 1)
    def _():
        o_ref[...] = h_ref[...] + _rms(acc_ref[...], g_post_ref[...])


def _ffn(layer, h, g_pre, w_gate, w_up, w_down, g_post, keep=None):
    rows, d = h.shape
    d_ff = w_gate.shape[1]
    tf = FFN_COLS
    if keep is None:
        tm, n_tiles = FFN_ROWS, rows // FFN_ROWS
        h_spec = pl.BlockSpec((tm, d), lambda i, c: (i, 0))
    else:
        rows_per_batch, first, count = keep
        tm = LAST_FFN_ROWS
        per_batch = count // tm
        n_tiles = (rows // rows_per_batch) * per_batch
        h_spec = pl.BlockSpec((pl.Element(tm), pl.Element(d)),
                              lambda i, c: (pl.multiple_of(
                                  (i // per_batch) * rows_per_batch + first + (i % per_batch) * tm, SUBLANES), 0))
        assert rows_per_batch % SUBLANES == 0 and first % SUBLANES == 0
    return pl.pallas_call(
        _ffn_kernel,
        grid=(n_tiles, d_ff // tf),
        in_specs=[h_spec, _layer_resident(layer, (1, d)),
                  pl.BlockSpec((d, tf), lambda i, c: (0, c)), pl.BlockSpec((d, tf), lambda i, c: (0, c)),
                  pl.BlockSpec((tf, d), lambda i, c: (c, 0)), _layer_resident(layer, (1, d))],
        out_specs=pl.BlockSpec((tm, d), lambda i, c: (i, 0)),
        out_shape=jax.ShapeDtypeStruct((n_tiles * tm, d), F32),
        scratch_shapes=[pltpu.VMEM((tm, d), BF16), pltpu.VMEM((tm, d), F32)],
        compiler_params=_params(2),
        name="swiglu_ffn",
    )(h, g_pre, w_gate, w_up, w_down, g_post)


def kernel(x, meta_tokens, pre_mix_g, w_in, w_pool, pool_scale, w_dw, b_dw, conv_ln_g, conv_ln_b, w_pw, mix_out_g,
           w_out, post_mix_g, pre_ffn_g, w_gate, w_up, w_down, post_ffn_g):
    batch, seq, d = x.shape
    depth = w_in.shape[0]
    d_pool = pool_scale.shape[1]
    d_conv = w_pw.shape[1]
    d_attn = w_out.shape[1] - d_pool - d_conv
    assert d_pool == LANES * len(POOL_WINDOWS) and d_attn % (2 * LANES) == 0
    assert POOL_WINDOWS == tuple(2 ** (g + 1) for g in range(len(POOL_WINDOWS)))
    assert d_attn // N_HEADS_SB == LANES // 2

    length = N_META + seq
    rows_per_batch = -(-length // ROW_ALIGN) * ROW_ALIGN
    assert rows_per_batch % SEQ_TILE == 0 and (batch * rows_per_batch) % FFN_ROWS == 0 and seq % LAST_FFN_ROWS == 0
    assert meta_tokens.shape[0] == N_META
    h = _Tokens(x.reshape(batch * seq, d), meta_tokens.astype(x.dtype), seq, rows_per_batch)

    w_pool, w_pw = w_pool.astype(BF16), w_pw.astype(BF16)
    w_in_l = w_in[0].astype(BF16)
    vec = lambda a: a[:, None, :]
    pre_mix_g, pool_scale, b_dw, conv_ln_g, conv_ln_b, post_mix_g, pre_ffn_g, post_ffn_g = (
        vec(a) for a in (pre_mix_g, pool_scale, b_dw, conv_ln_g, conv_ln_b, post_mix_g, pre_ffn_g, post_ffn_g))
    g_attn, g_pool, g_conv = (vec(a) for a in jnp.split(mix_out_g, [d_attn, d_attn + d_pool], axis=1))

    for i in range(depth):
        q, k, v, o_pool, o_conv, w_gate_l, w_up_l = _proj_mix(
            i, h, pre_mix_g, w_in_l, w_pool, pool_scale, g_pool, w_dw, b_dw, conv_ln_g, conv_ln_b, w_pw, g_conv,
            d_attn, rows_per_batch, [(w_gate, i, 2), (w_up, i, 2)])
        o_attn, w_out_l, *w_in_next = _attention(i, q, k, v, g_attn, batch, rows_per_batch,
                                                 [(w_out, i, 2)] + ([(w_in, i + 1, 2)] if i + 1 < depth else []))
        if w_in_next:
            w_in_l = w_in_next[0]
        h, w_down_l = _out_proj(i, o_attn, o_pool, o_conv, h, w_out_l, post_mix_g, [(w_down, i, 1)])
        keep = (rows_per_batch, N_META, seq) if i == depth - 1 else None
        h = _ffn(i, h, pre_ffn_g, w_gate_l, w_up_l, w_down_l, post_ffn_g, keep)

    return h.reshape(batch, seq, d)
```

```python
import functools
from typing import NamedTuple

import jax
import jax.numpy as jnp
from jax import lax
from jax.experimental import pallas as pl
from jax.experimental.pallas import tpu as pltpu

N_META = 16
N_HEADS_SB = 16
POOL_WINDOWS = (2, 4, 8, 16)
CONV_WIDTH = 31
EPS = 1e-6

LANES = 128
SUBLANES = 8
VMEM_BYTES_V7X = 64 * 1024 * 1024
VMEM_LIMIT = VMEM_BYTES_V7X - 6 * 1024 * 1024

SEQ_TILE = 128
TOP_ROWS = 32
ROW_ALIGN = 384
PROJ_ROWS = 384
FFN_ROWS = 768
LAST_FFN_ROWS = 512
FFN_COLS = 512
CONV_HALO = 32
POOL_HALO = SUBLANES * len(POOL_WINDOWS)
CONV_CHUNK = 32
ATTN_CAST_BLOCK = 128
PROJ_CAST_BLOCK = 256

LOG2_E = 1.4426950408889634
LOG2_WEIGHT_FLOOR = -127.0

F32 = jnp.float32
BF16 = jnp.bfloat16


def _rms(x, g):
    return x * lax.rsqrt(jnp.mean(x * x, axis=-1, keepdims=True) + EPS) * g


def _resident(shape):
    return pl.BlockSpec(shape, lambda *_: (0,) * len(shape), pipeline_mode=pl.Buffered(1))


def _layer_resident(layer, shape, col=0):
    return pl.BlockSpec((None,) + tuple(shape), lambda *_: (layer,) + (0,) * (len(shape) - 1) + (col,),
                        pipeline_mode=pl.Buffered(1))


def _params(n_grid_axes):
    return pltpu.CompilerParams(dimension_semantics=("arbitrary",) * n_grid_axes, vmem_limit_bytes=VMEM_LIMIT)


class _Tokens(NamedTuple):
    x: jax.Array
    meta: jax.Array
    seq: int
    rows_per_batch: int

    @property
    def shape(self):
        return (self.x.shape[0] // self.seq * self.rows_per_batch, self.x.shape[1])


def _stream_inputs(h, tm, tile_of=lambda i: i):
    if not isinstance(h, _Tokens):
        return [h], [pl.BlockSpec((tm, h.shape[1]), lambda i, *_: (tile_of(i), 0))], None
    seq, d = h.seq, h.x.shape[1]
    n_meta = h.meta.shape[0]
    tiles_per_batch = h.rows_per_batch // tm
    n_pad = h.rows_per_batch - n_meta - seq
    assert h.rows_per_batch % tm == 0 and n_pad <= tm and n_meta <= tm
    assert all(n % SUBLANES == 0 for n in (n_meta, n_pad, seq, tm))

    def start(i):
        b, t = i // tiles_per_batch, i % tiles_per_batch
        return pl.multiple_of(b * seq + jnp.clip(t * tm - n_meta, 0, seq - tm), SUBLANES)

    spec = pl.BlockSpec((pl.Element(tm), pl.Element(d)), lambda i, *_: (start(tile_of(i)), 0))
    return [h.x, h.meta], [spec, _resident(h.meta.shape)], (tiles_per_batch, n_pad)


def _stream_tile(stream_refs, layout, tile=None):
    if layout is None:
        return stream_refs[0][...]
    x_ref, meta_ref = stream_refs
    tiles_per_batch, n_pad = layout
    blk = x_ref[...]
    tm, d = blk.shape
    t = lax.rem(pl.program_id(0) if tile is None else tile, tiles_per_batch)
    first = jnp.concatenate([meta_ref[...], blk[:tm - meta_ref.shape[0]]], axis=0)
    last = jnp.concatenate([blk[n_pad:], jnp.zeros((n_pad, d), blk.dtype)], axis=0)
    return jnp.where(t == 0, first, jnp.where(t == tiles_per_batch - 1, last, blk))


def _cast_specs(w, layer, axis, block_size, step_of, n_steps):
    k, n = w.shape[1:]
    n_blocks = w.shape[axis] // block_size
    assert w.shape[axis] % block_size == 0 and n_blocks <= n_steps
    block = (block_size, n) if axis == 1 else (k, block_size)

    def index(*grid_ids):
        j = jnp.minimum(step_of(*grid_ids), n_blocks - 1)
        return (j, 0) if axis == 1 else (0, j)

    return (pl.BlockSpec((None,) + block, lambda *g: (layer,) + index(*g)), pl.BlockSpec(block, index),
            jax.ShapeDtypeStruct((k, n), BF16))


def _cast_blocks(cast_src, cast_dst):
    for src, dst in zip(cast_src, cast_dst):
        dst[...] = src[...].astype(BF16)


def _attn_kernel(*refs, n_pairs, n_casts):
    q_ref, k_ref, v_ref, tri_ref, g_ref = refs[:5]
    o_ref = refs[5 + n_casts]
    q2_ref, acc_ref, c_ref = refs[6 + 2 * n_casts:]
    _cast_blocks(refs[5:5 + n_casts], refs[6 + n_casts:6 + 2 * n_casts])
    t = SEQ_TILE
    qi = pl.program_id(1)
    pairs = range(n_pairs)
    lanes = [slice(p * LANES, (p + 1) * LANES) for p in pairs]
    first_head = lax.broadcasted_iota(jnp.int32, (t, LANES), 1) < LANES // 2
    row = lax.broadcasted_iota(jnp.int32, (2 * t, t), 0)
    col = lax.broadcasted_iota(jnp.int32, (2 * t, t), 1)
    causal = col < (row & (t - 1))
    zero = jnp.zeros((), BF16)
    for p in pairs:
        qp = q_ref[:, lanes[p]]
        q2_ref[p] = jnp.concatenate([jnp.where(first_head, qp, zero), jnp.where(first_head, zero, qp)], axis=0)

    def head_rows(ref, p, r):
        return ref[p] if r == t else jnp.concatenate([ref[p, :r], ref[p, t:t + r]], axis=0)

    def key_tile(kb, r, diagonal=False):
        start = pl.multiple_of(kb * t, t)
        z = [lax.dot_general(head_rows(q2_ref, p, r), k_ref[pl.ds(start, t), lanes[p]], (((1,), (1,)), ((), ())),
                             preferred_element_type=F32) for p in pairs]
        v = [v_ref[pl.ds(start, t), lanes[p]] for p in pairs]
        c_old = None if diagonal else [head_rows(c_ref, p, r) for p in pairs]
        log_beta, later, c_new = [], [], []
        for p in pairs:
            z_pos = jnp.maximum(z[p], 0.0)
            z_neg = z[p] - z_pos
            log_term = jnp.log(1.0 + jnp.exp2(z_neg - z_pos)) * LOG2_E
            soft = z_pos + log_term
            if diagonal:
                soft = jnp.where(causal, soft, 0.0)
            later.append(jnp.dot(soft.astype(BF16), tri_ref[...], preferred_element_type=F32))
            log_beta.append(z_neg - log_term)
            c = jnp.sum(soft, axis=1, keepdims=True)
            c_new.append(jnp.broadcast_to(c, soft.shape) if diagonal else c_old[p] + c)
        o = []
        for p in pairs:
            x = log_beta[p] - later[p]
            if not diagonal:
                x = x - c_old[p]
            w = jnp.exp2(x)
            if diagonal:
                w = jnp.where(causal, w, 0.0)
            o.append(jnp.dot(w.astype(BF16), v[p], preferred_element_type=F32))
        c_min = None
        for p in pairs:
            first = lax.broadcasted_iota(jnp.int32, (r, LANES), 1) < LANES // 2
            o_pair = jnp.where(first, o[p][:r], o[p][r:])
            if diagonal:
                acc_ref[:, lanes[p]] = o_pair
            else:
                acc_ref[:r, lanes[p]] += o_pair
            if r == t:
                c_ref[p] = c_new[p]
            else:
                c_ref[p, :r] = c_new[p][:r]
                c_ref[p, t:t + r] = c_new[p][r:]
            c_min = c_new[p] if c_min is None else jnp.minimum(c_min, c_new[p])
        if r < t:
            return jnp.min(c_min)
        top = jnp.minimum(jnp.min(c_min[:TOP_ROWS]), jnp.min(c_min[t:t + TOP_ROWS]))
        rest = jnp.minimum(jnp.min(c_min[TOP_ROWS:t]), jnp.min(c_min[t + TOP_ROWS:]))
        return top, rest

    def unfinished(c_min):
        return c_min < -LOG2_WEIGHT_FLOOR

    def full_body(s):
        kb, _, _ = s
        return (kb - 1,) + key_tile(kb, t)

    def top_body(s):
        kb, _ = s
        return kb - 1, key_tile(kb, TOP_ROWS)

    kb, top, _ = lax.while_loop(lambda s: jnp.logical_and(s[0] >= 0, unfinished(s[2])), full_body,
                                (qi - 1,) + key_tile(qi, t, diagonal=True))
    lax.while_loop(lambda s: jnp.logical_and(s[0] >= 0, unfinished(s[1])), top_body, (kb, top))
    o_ref[...] = _rms(acc_ref[...], g_ref[...]).astype(BF16)


def _attention(layer, q, k, v, g, batch, rows_per_batch, casts):
    rows, d_attn = q.shape
    t = SEQ_TILE
    nq = rows_per_batch // t
    j = jnp.arange(t)
    tri = (j[:, None] > j[None, :]).astype(BF16)
    n_pairs = d_attn // LANES
    kv_spec = pl.BlockSpec((rows_per_batch, d_attn), lambda b, i: (b, 0), pipeline_mode=pl.Buffered(1))
    tile_spec = pl.BlockSpec((t, d_attn), lambda b, i: (b * nq + i, 0))
    cast_specs = [_cast_specs(w, l, axis, ATTN_CAST_BLOCK, lambda b, i: b * nq + i, batch * nq) for w, l, axis in casts]
    return pl.pallas_call(
        functools.partial(_attn_kernel, n_pairs=n_pairs, n_casts=len(casts)),
        grid=(batch, nq),
        in_specs=[tile_spec, kv_spec, kv_spec, _resident(tri.shape), _layer_resident(layer, (1, d_attn))]
        + [c[0] for c in cast_specs],
        out_specs=[tile_spec] + [c[1] for c in cast_specs],
        out_shape=[jax.ShapeDtypeStruct((rows, d_attn), BF16)] + [c[2] for c in cast_specs],
        scratch_shapes=[pltpu.VMEM((n_pairs, 2 * t, LANES), BF16), pltpu.VMEM((t, d_attn), F32),
                        pltpu.VMEM((n_pairs, 2 * t, t), F32)],
        compiler_params=_params(2),
        name="sb_attention",
    )(q, k, v, tri, g, *[w for w, _, _ in casts])


def _digest(v):
    v = v.astype(F32)
    rows, cols = v.shape
    top = jnp.max(v.reshape(rows // SUBLANES, SUBLANES, cols), axis=0)
    return functools.reduce(jnp.maximum, [top[:, c:c + LANES] for c in range(0, cols, LANES)])


def _zero_after(digests):
    bits = pltpu.bitcast(functools.reduce(jnp.maximum, digests), jnp.uint32)
    zero = ((bits >> 16) >> 16).astype(F32)
    return jnp.concatenate([zero, zero], axis=0).astype(BF16)


def _proj_mix_kernel(*refs, n_stream, layout, n_casts, n_tiles, tiles_per_batch, d_attn, d_pool, d_conv, q_scale):
    ns = n_stream
    (g_ref, w_ref, w_pool_ref, pool_scale_ref, g_pool_ref, w_dw_ref, b_dw_ref, ln_g_ref, ln_b_ref, w_pw_ref,
     g_conv_ref) = refs[ns:ns + 11]
    cast_src = refs[ns + 11:ns + 11 + n_casts]
    q_ref, k_ref, v_ref, o_pool_ref, o_conv_ref = refs[ns + 11 + n_casts:ns + 16 + n_casts]
    cast_dst = refs[ns + 16 + n_casts:ns + 16 + 2 * n_casts]
    pool_ext, pool_sums, conv_ext, y_ref = refs[ns + 16 + 2 * n_casts:]
    i = pl.program_id(0)
    tile = jnp.minimum(i, n_tiles - 1)
    tm = q_ref.shape[0]

    @pl.when(i == 0)
    def _():
        pool_ext[...] = jnp.zeros_like(pool_ext)
        conv_ext[0] = jnp.zeros(conv_ext.shape[1:], F32)

    u = _rms(_stream_tile(refs[:ns], layout, tile), g_ref[...]).astype(BF16)

    def proj(lo, hi, wait_for=()):
        lhs = u
        if wait_for:
            head = jnp.concatenate([u[:16, :LANES] + _zero_after(wait_for), u[:16, LANES:]], axis=1)
            lhs = jnp.concatenate([head, u[16:]], axis=0)
        return jnp.dot(lhs, w_ref[:, lo:hi], preferred_element_type=F32)

    def mix_pool():
        n_pool = POOL_HALO + tm
        level, window_sum = pool_ext, []
        for s in range(len(POOL_WINDOWS)):
            start, shift, lo = SUBLANES * (s + 1), 2 ** s, s * LANES
            both = level[start:n_pool, lo:] + level[start - shift:n_pool - shift, lo:]
            window_sum.append(both[POOL_HALO - start:, :LANES])
            if s + 1 < len(POOL_WINDOWS):
                pool_sums[s, start:n_pool, lo:] = both
                level = pool_sums.at[s]
        ti = lax.rem(i + tiles_per_batch - 1, tiles_per_batch)
        pos = ti * tm + lax.broadcasted_iota(jnp.int32, (tm, 1), 0)
        mixed = []
        sq = jnp.zeros((tm, 1), F32)
        for gi, window in enumerate(POOL_WINDOWS):
            lanes = slice(gi * LANES, (gi + 1) * LANES)
            count = jnp.minimum(pos + 1, window).astype(F32)
            pooled = window_sum[gi] / count - pool_ext[POOL_HALO:, lanes]
            m = jnp.dot(pooled.astype(BF16), w_pool_ref[gi].astype(BF16), preferred_element_type=F32)
            m = m * pool_scale_ref[:, lanes]
            sq = sq + jnp.sum(m * m, axis=-1, keepdims=True)
            mixed.append(m)
        inv = lax.rsqrt(sq / d_pool + EPS)
        out = [(m * inv * g_pool_ref[:, gi * LANES:(gi + 1) * LANES]).astype(BF16) for gi, m in enumerate(mixed)]
        for gi, o in enumerate(out):
            o_pool_ref[:, gi * LANES:(gi + 1) * LANES] = o
        return [_digest(o) for o in out]

    n_ext = CONV_HALO + tm

    def conv_copies():
        digests = []
        for r in range(1, SUBLANES):
            shifted = conv_ext[0, r:r + n_ext - SUBLANES, :]
            conv_ext[r, :n_ext - SUBLANES, :] = shifted
            digests.append(_digest(shifted))
        return digests

    first_tap = CONV_HALO - (CONV_WIDTH - 1)

    def conv_rows(base):
        acc = jnp.broadcast_to(b_dw_ref[...], (CONV_CHUNK, d_conv))
        for tap in range(CONV_WIDTH):
            shift = (first_tap + tap) % SUBLANES
            lo = base + first_tap + tap - shift
            acc = acc + w_dw_ref[tap:tap + 1, :] * conv_ext[shift, lo:lo + CONV_CHUNK, :]
        y_ref[base:base + CONV_CHUNK, :] = acc
        return _digest(acc)

    def conv_finish():
        y = y_ref[...]
        mu = jnp.mean(y, axis=-1, keepdims=True)
        yc = y - mu
        var = jnp.mean(yc * yc, axis=-1, keepdims=True)
        y = yc * lax.rsqrt(var + EPS) * ln_g_ref[...] + ln_b_ref[...]
        y = y * jax.nn.sigmoid(y)
        o = jnp.dot(y.astype(BF16), w_pw_ref[...].astype(BF16), preferred_element_type=F32)
        o = _rms(o, g_conv_ref[...]).astype(BF16)
        o_conv_ref[...] = o
        return [_digest(o)]

    half = d_attn // 2
    chunks = list(range(0, tm, CONV_CHUNK))
    per_group = -(-len(chunks) // 4)
    conv_group = lambda g: [conv_rows(base) for base in chunks[g * per_group:(g + 1) * per_group]]
    q_ref[:, :half] = (proj(0, half) * q_scale).astype(BF16)
    q_ref[:, half:] = (proj(half, d_attn) * q_scale).astype(BF16)
    done = mix_pool()
    k_ref[:, :half] = proj(d_attn, d_attn + half, done).astype(BF16)
    done = conv_copies()
    k_ref[:, half:] = proj(d_attn + half, 2 * d_attn, done).astype(BF16)
    done = conv_group(0)
    v_ref[:, :half] = proj(2 * d_attn, 2 * d_attn + half, done).astype(BF16)
    done = conv_group(1)
    v_ref[:, half:] = proj(2 * d_attn + half, 3 * d_attn, done).astype(BF16)
    done = conv_group(2)
    o = 3 * d_attn
    new_pool = proj(o, o + d_pool, done)
    done = conv_group(3)
    o += d_pool
    conv_value = proj(o, o + d_conv, done)
    done = conv_finish()
    new_conv = conv_value * jax.nn.sigmoid(proj(o + d_conv, o + 2 * d_conv, done))
    _cast_blocks(cast_src, cast_dst)

    starts_batch = lax.rem(tile, tiles_per_batch) == 0
    pool_ext[:POOL_HALO, :] = jnp.where(starts_batch, 0.0, pool_ext[tm:, :])
    pool_ext[POOL_HALO:, :] = new_pool
    conv_ext[0, :CONV_HALO, :] = jnp.where(starts_batch, 0.0, conv_ext[0, tm:, :])
    conv_ext[0, CONV_HALO:, :] = new_conv


def _proj_mix(layer, h, g, w, w_pool, pool_scale, g_mix, w_dw, b_dw, ln_g, ln_b, w_pw, d_attn, rows_per_batch, casts):
    rows, d = h.shape
    d_pool, d_conv = pool_scale.shape[-1], w_pw.shape[-1]
    tm = PROJ_ROWS
    n = rows // tm
    head_dim = d_attn // N_HEADS_SB
    tile_of = lambda i: jnp.minimum(i, n - 1)
    cur = lambda width: pl.BlockSpec((tm, width), lambda i: (tile_of(i), 0))
    prev = lambda width: pl.BlockSpec((tm, width), lambda i: (jnp.maximum(i - 1, 0), 0))
    lr = functools.partial(_layer_resident, layer)
    stream, stream_specs, layout = _stream_inputs(h, tm, tile_of)
    cast_specs = [_cast_specs(w_, l, axis, PROJ_CAST_BLOCK, lambda i: i, n + 1) for w_, l, axis in casts]
    return pl.pallas_call(
        functools.partial(_proj_mix_kernel, n_stream=len(stream), layout=layout, n_casts=len(casts), n_tiles=n,
                          tiles_per_batch=rows_per_batch // tm, d_attn=d_attn, d_pool=d_pool, d_conv=d_conv,
                          q_scale=LOG2_E * head_dim ** -0.5),
        grid=(n + 1,),
        in_specs=stream_specs + [lr((1, d)), _resident(w.shape), lr(w_pool.shape[1:]), lr((1, d_pool)),
                                 lr((1, d_pool), d_attn // d_pool), lr(w_dw.shape[1:]), lr((1, d_conv)),
                                 lr((1, d_conv)), lr((1, d_conv)), lr(w_pw.shape[1:]),
                                 lr((1, d_conv), (d_attn + d_pool) // d_conv)] + [c[0] for c in cast_specs],
        out_specs=[cur(d_attn), cur(d_attn), cur(d_attn), prev(d_pool), prev(d_conv)] + [c[1] for c in cast_specs],
        out_shape=[jax.ShapeDtypeStruct((rows, d_attn), BF16)] * 3
        + [jax.ShapeDtypeStruct((rows, d_pool), BF16), jax.ShapeDtypeStruct((rows, d_conv), BF16)]
        + [c[2] for c in cast_specs],
        scratch_shapes=[pltpu.VMEM((POOL_HALO + tm, d_pool), F32),
                        pltpu.VMEM((len(POOL_WINDOWS) - 1, POOL_HALO + tm, d_pool), F32),
                        pltpu.VMEM((SUBLANES, CONV_HALO + tm, d_conv), F32), pltpu.VMEM((tm, d_conv), F32)],
        compiler_params=_params(1),
        name="in_proj_mixers",
    )(*stream, g, w, w_pool, pool_scale, g_mix, w_dw, b_dw, ln_g, ln_b, w_pw, g_mix, *[w_ for w_, _, _ in casts])


def _out_proj_kernel(*refs, n_stream, layout, n_casts):
    attn_ref, pool_ref, conv_ref, w_ref, g_ref = refs[n_stream:n_stream + 5]
    o_ref = refs[n_stream + 5 + n_casts]
    _cast_blocks(refs[n_stream + 5:n_stream + 5 + n_casts], refs[n_stream + 6 + n_casts:])
    merged = jnp.concatenate([attn_ref[...], pool_ref[...], conv_ref[...]], axis=1)
    y = jnp.dot(merged, w_ref[...], preferred_element_type=F32)
    o_ref[...] = _stream_tile(refs[:n_stream], layout) + _rms(y, g_ref[...])


def _out_proj(layer, attn, pool, conv, h, w, g, casts):
    rows, d = h.shape
    tm = PROJ_ROWS
    row_spec = lambda n: pl.BlockSpec((tm, n), lambda i: (i, 0))
    stream, stream_specs, layout = _stream_inputs(h, tm)
    cast_specs = [_cast_specs(w_, l, axis, PROJ_CAST_BLOCK, lambda i: i, rows // tm) for w_, l, axis in casts]
    return pl.pallas_call(
        functools.partial(_out_proj_kernel, n_stream=len(stream), layout=layout, n_casts=len(casts)),
        grid=(rows // tm,),
        in_specs=stream_specs + [row_spec(attn.shape[1]), row_spec(pool.shape[1]), row_spec(conv.shape[1]),
                                 _resident(w.shape), _layer_resident(layer, (1, d))] + [c[0] for c in cast_specs],
        out_specs=[row_spec(d)] + [c[1] for c in cast_specs],
        out_shape=[jax.ShapeDtypeStruct((rows, d), F32)] + [c[2] for c in cast_specs],
        compiler_params=_params(1),
        name="out_proj",
    )(*stream, attn, pool, conv, w, g, *[w_ for w_, _, _ in casts])


def _ffn_kernel(h_ref, g_pre_ref, w_gate_ref, w_up_ref, w_down_ref, g_post_ref, o_ref, u_ref, acc_ref):
    c = pl.program_id(1)

    @pl.when(c == 0)
    def _():
        u_ref[...] = _rms(h_ref[...], g_pre_ref[...]).astype(BF16)
        acc_ref[...] = jnp.zeros_like(acc_ref)

    u = u_ref[...]
    gate = jnp.dot(u, w_gate_ref[...], preferred_element_type=F32)
    up = jnp.dot(u, w_up_ref[...], preferred_element_type=F32)
    a = (gate * jax.nn.sigmoid(gate) * up).astype(BF16)
    tf = w_down_ref.shape[0]
    for n in range(0, acc_ref.shape[1], tf):
        acc_ref[:, n:n + tf] += jnp.dot(a, w_down_ref[:, n:n + tf], preferred_element_type=F32)

    @pl.when(c == pl.num_programs(1) - 1)
    def _():
        o_ref[...] = h_ref[...] + _rms(acc_ref[...], g_post_ref[...])


def _ffn(layer, h, g_pre, w_gate, w_up, w_down, g_post, keep=None):
    rows, d = h.shape
    d_ff = w_gate.shape[1]
    tf = FFN_COLS
    if keep is None:
        tm, n_tiles = FFN_ROWS, rows // FFN_ROWS
        h_spec = pl.BlockSpec((tm, d), lambda i, c: (i, 0))
    else:
        rows_per_batch, first, count = keep
        tm = LAST_FFN_ROWS
        per_batch = count // tm
        n_tiles = (rows // rows_per_batch) * per_batch
        h_spec = pl.BlockSpec((pl.Element(tm), pl.Element(d)),
                              lambda i, c: (pl.multiple_of(
                                  (i // per_batch) * rows_per_batch + first + (i % per_batch) * tm, SUBLANES), 0))
        assert rows_per_batch % SUBLANES == 0 and first % SUBLANES == 0
    return pl.pallas_call(
        _ffn_kernel,
        grid=(n_tiles, d_ff // tf),
        in_specs=[h_spec, _layer_resident(layer, (1, d)),
                  pl.BlockSpec((d, tf), lambda i, c: (0, c)), pl.BlockSpec((d, tf), lambda i, c: (0, c)),
                  pl.BlockSpec((tf, d), lambda i, c: (c, 0)), _layer_resident(layer, (1, d))],
        out_specs=pl.BlockSpec((tm, d), lambda i, c: (i, 0)),
        out_shape=jax.ShapeDtypeStruct((n_tiles * tm, d), F32),
        scratch_shapes=[pltpu.VMEM((tm, d), BF16), pltpu.VMEM((tm, d), F32)],
        compiler_params=_params(2),
        name="swiglu_ffn",
    )(h, g_pre, w_gate, w_up, w_down, g_post)


def kernel(x, meta_tokens, pre_mix_g, w_in, w_pool, pool_scale, w_dw, b_dw, conv_ln_g, conv_ln_b, w_pw, mix_out_g,
           w_out, post_mix_g, pre_ffn_g, w_gate, w_up, w_down, post_ffn_g):
    batch, seq, d = x.shape
    depth = w_in.shape[0]
    d_pool = pool_scale.shape[1]
    d_conv = w_pw.shape[1]
    d_attn = w_out.shape[1] - d_pool - d_conv
    assert d_pool == LANES * len(POOL_WINDOWS) and d_attn % (2 * LANES) == 0
    assert POOL_WINDOWS == tuple(2 ** (g + 1) for g in range(len(POOL_WINDOWS)))
    assert d_attn // N_HEADS_SB == LANES // 2

    length = N_META + seq
    rows_per_batch = -(-length // ROW_ALIGN) * ROW_ALIGN
    assert rows_per_batch % SEQ_TILE == 0 and (batch * rows_per_batch) % FFN_ROWS == 0 and seq % LAST_FFN_ROWS == 0
    assert meta_tokens.shape[0] == N_META
    h = _Tokens(x.reshape(batch * seq, d), meta_tokens.astype(x.dtype), seq, rows_per_batch)

    w_in_l = w_in[0].astype(BF16)
    vec = lambda a: a.reshape(a.shape[0], 1, a.shape[1])
    pre_mix_g, pool_scale, b_dw, conv_ln_g, conv_ln_b, mix_out_g, post_mix_g, pre_ffn_g, post_ffn_g = (
        vec(a) for a in (pre_mix_g, pool_scale, b_dw, conv_ln_g, conv_ln_b, mix_out_g, post_mix_g, pre_ffn_g,
                         post_ffn_g))
    assert d_attn % d_pool == 0 and (d_attn + d_pool) % d_conv == 0

    for i in range(depth):
        q, k, v, o_pool, o_conv, w_gate_l, w_up_l = _proj_mix(
            i, h, pre_mix_g, w_in_l, w_pool, pool_scale, mix_out_g, w_dw, b_dw, conv_ln_g, conv_ln_b, w_pw, d_attn,
            rows_per_batch, [(w_gate, i, 2), (w_up, i, 2)])
        o_attn, w_out_l, *w_in_next = _attention(i, q, k, v, mix_out_g, batch, rows_per_batch,
                                                 [(w_out, i, 2)] + ([(w_in, i + 1, 2)] if i + 1 < depth else []))
        if w_in_next:
            w_in_l = w_in_next[0]
        h, w_down_l = _out_proj(i, o_attn, o_pool, o_conv, h, w_out_l, post_mix_g, [(w_down, i, 1)])
        keep = (rows_per_batch, N_META, seq) if i == depth - 1 else None
        h = _ffn(i, h, pre_ffn_g, w_gate_l, w_up_l, w_down_l, post_ffn_g, keep)

    return h.reshape(batch, seq, d)
```

```python
import functools
from typing import NamedTuple

import jax
import jax.numpy as jnp
from jax import lax
from jax.experimental import pallas as pl
from jax.experimental.pallas import tpu as pltpu

N_META = 16
N_HEADS_SB = 16
POOL_WINDOWS = (2, 4, 8, 16)
CONV_WIDTH = 31
EPS = 1e-6

LANES = 128
SUBLANES = 8
VMEM_BYTES_V7X = 64 * 1024 * 1024
VMEM_LIMIT = VMEM_BYTES_V7X - 6 * 1024 * 1024

SEQ_TILE = 128
TOP_ROWS = 32
ROW_ALIGN = 384
PROJ_ROWS = 384
FFN_ROWS = 768
LAST_FFN_ROWS = 512
FFN_COLS = 512
CONV_HALO = 32
POOL_HALO = SUBLANES * len(POOL_WINDOWS)
CONV_CHUNK = 32
ATTN_CAST_BLOCK = 128
PROJ_CAST_BLOCK = 256

LOG2_E = 1.4426950408889634
LOG2_WEIGHT_FLOOR = -127.0

F32 = jnp.float32
BF16 = jnp.bfloat16


def _rms(x, g):
    return x * lax.rsqrt(jnp.mean(x * x, axis=-1, keepdims=True) + EPS) * g


def _resident(shape):
    return pl.BlockSpec(shape, lambda *_: (0,) * len(shape), pipeline_mode=pl.Buffered(1))


def _layer_resident(layer, shape):
    return pl.BlockSpec((None,) + tuple(shape), lambda *_: (layer,) + (0,) * len(shape),
                        pipeline_mode=pl.Buffered(1))


def _layer_row(ref, layer, lo=0, hi=None):
    return ref.at[layer:layer + 1, lo:ref.shape[1] if hi is None else hi]


def _params(n_grid_axes):
    return pltpu.CompilerParams(dimension_semantics=("arbitrary",) * n_grid_axes, vmem_limit_bytes=VMEM_LIMIT)


class _Tokens(NamedTuple):
    x: jax.Array
    meta: jax.Array
    seq: int
    rows_per_batch: int

    @property
    def shape(self):
        return (self.x.shape[0] // self.seq * self.rows_per_batch, self.x.shape[1])


def _stream_inputs(h, tm, tile_of=lambda i: i):
    if not isinstance(h, _Tokens):
        return [h], [pl.BlockSpec((tm, h.shape[1]), lambda i, *_: (tile_of(i), 0))], None
    seq, d = h.seq, h.x.shape[1]
    n_meta = h.meta.shape[0]
    tiles_per_batch = h.rows_per_batch // tm
    n_pad = h.rows_per_batch - n_meta - seq
    assert h.rows_per_batch % tm == 0 and n_pad <= tm and n_meta <= tm
    assert all(n % SUBLANES == 0 for n in (n_meta, n_pad, seq, tm))

    def start(i):
        b, t = i // tiles_per_batch, i % tiles_per_batch
        return pl.multiple_of(b * seq + jnp.clip(t * tm - n_meta, 0, seq - tm), SUBLANES)

    spec = pl.BlockSpec((pl.Element(tm), pl.Element(d)), lambda i, *_: (start(tile_of(i)), 0))
    return [h.x, h.meta], [spec, _resident(h.meta.shape)], (tiles_per_batch, n_pad)


def _stream_tile(stream_refs, layout, tile=None):
    if layout is None:
        return stream_refs[0][...]
    x_ref, meta_ref = stream_refs
    tiles_per_batch, n_pad = layout
    blk = x_ref[...]
    tm, d = blk.shape
    t = lax.rem(pl.program_id(0) if tile is None else tile, tiles_per_batch)
    first = jnp.concatenate([meta_ref[...], blk[:tm - meta_ref.shape[0]]], axis=0)
    last = jnp.concatenate([blk[n_pad:], jnp.zeros((n_pad, d), blk.dtype)], axis=0)
    return jnp.where(t == 0, first, jnp.where(t == tiles_per_batch - 1, last, blk))


def _cast_specs(w, layer, axis, block_size, step_of, n_steps):
    k, n = w.shape[1:]
    n_blocks = w.shape[axis] // block_size
    assert w.shape[axis] % block_size == 0 and n_blocks <= n_steps
    block = (block_size, n) if axis == 1 else (k, block_size)

    def index(*grid_ids):
        j = jnp.minimum(step_of(*grid_ids), n_blocks - 1)
        return (j, 0) if axis == 1 else (0, j)

    return (pl.BlockSpec((None,) + block, lambda *g: (layer,) + index(*g)), pl.BlockSpec(block, index),
            jax.ShapeDtypeStruct((k, n), BF16))


def _cast_blocks(cast_src, cast_dst):
    for src, dst in zip(cast_src, cast_dst):
        dst[...] = src[...].astype(BF16)


def _attn_kernel(*refs, layer, n_pairs, n_casts):
    q_ref, k_ref, v_ref, tri_ref, g_ref = refs[:5]
    g_ref = _layer_row(g_ref, layer, 0, n_pairs * LANES)
    o_ref = refs[5 + n_casts]
    q2_ref, acc_ref, c_ref = refs[6 + 2 * n_casts:]
    _cast_blocks(refs[5:5 + n_casts], refs[6 + n_casts:6 + 2 * n_casts])
    t = SEQ_TILE
    qi = pl.program_id(1)
    pairs = range(n_pairs)
    lanes = [slice(p * LANES, (p + 1) * LANES) for p in pairs]
    first_head = lax.broadcasted_iota(jnp.int32, (t, LANES), 1) < LANES // 2
    row = lax.broadcasted_iota(jnp.int32, (2 * t, t), 0)
    col = lax.broadcasted_iota(jnp.int32, (2 * t, t), 1)
    causal = col < (row & (t - 1))
    zero = jnp.zeros((), BF16)
    for p in pairs:
        qp = q_ref[:, lanes[p]]
        q2_ref[p] = jnp.concatenate([jnp.where(first_head, qp, zero), jnp.where(first_head, zero, qp)], axis=0)

    def head_rows(ref, p, r):
        return ref[p] if r == t else jnp.concatenate([ref[p, :r], ref[p, t:t + r]], axis=0)

    def key_tile(kb, r, diagonal=False):
        start = pl.multiple_of(kb * t, t)
        z = [lax.dot_general(head_rows(q2_ref, p, r), k_ref[pl.ds(start, t), lanes[p]], (((1,), (1,)), ((), ())),
                             preferred_element_type=F32) for p in pairs]
        v = [v_ref[pl.ds(start, t), lanes[p]] for p in pairs]
        c_old = None if diagonal else [head_rows(c_ref, p, r) for p in pairs]
        log_beta, later, c_new = [], [], []
        for p in pairs:
            z_pos = jnp.maximum(z[p], 0.0)
            z_neg = z[p] - z_pos
            log_term = jnp.log(1.0 + jnp.exp2(z_neg - z_pos)) * LOG2_E
            soft = z_pos + log_term
            if diagonal:
                soft = jnp.where(causal, soft, 0.0)
            later.append(jnp.dot(soft.astype(BF16), tri_ref[...], preferred_element_type=F32))
            log_beta.append(z_neg - log_term)
            c = jnp.sum(soft, axis=1, keepdims=True)
            c_new.append(jnp.broadcast_to(c, soft.shape) if diagonal else c_old[p] + c)
        o = []
        for p in pairs:
            x = log_beta[p] - later[p]
            if not diagonal:
                x = x - c_old[p]
            w = jnp.exp2(x)
            if diagonal:
                w = jnp.where(causal, w, 0.0)
            o.append(jnp.dot(w.astype(BF16), v[p], preferred_element_type=F32))
        c_min = None
        for p in pairs:
            first = lax.broadcasted_iota(jnp.int32, (r, LANES), 1) < LANES // 2
            o_pair = jnp.where(first, o[p][:r], o[p][r:])
            if diagonal:
                acc_ref[:, lanes[p]] = o_pair
            else:
                acc_ref[:r, lanes[p]] += o_pair
            if r == t:
                c_ref[p] = c_new[p]
            else:
                c_ref[p, :r] = c_new[p][:r]
                c_ref[p, t:t + r] = c_new[p][r:]
            c_min = c_new[p] if c_min is None else jnp.minimum(c_min, c_new[p])
        if r < t:
            return jnp.min(c_min)
        top = jnp.minimum(jnp.min(c_min[:TOP_ROWS]), jnp.min(c_min[t:t + TOP_ROWS]))
        rest = jnp.minimum(jnp.min(c_min[TOP_ROWS:t]), jnp.min(c_min[t + TOP_ROWS:]))
        return top, rest

    def unfinished(c_min):
        return c_min < -LOG2_WEIGHT_FLOOR

    def full_body(s):
        kb, _, _ = s
        return (kb - 1,) + key_tile(kb, t)

    def top_body(s):
        kb, _ = s
        return kb - 1, key_tile(kb, TOP_ROWS)

    kb, top, _ = lax.while_loop(lambda s: jnp.logical_and(s[0] >= 0, unfinished(s[2])), full_body,
                                (qi - 1,) + key_tile(qi, t, diagonal=True))
    lax.while_loop(lambda s: jnp.logical_and(s[0] >= 0, unfinished(s[1])), top_body, (kb, top))
    o_ref[...] = _rms(acc_ref[...], g_ref[...]).astype(BF16)


def _attention(layer, q, k, v, g, batch, rows_per_batch, casts):
    rows, d_attn = q.shape
    t = SEQ_TILE
    nq = rows_per_batch // t
    j = jnp.arange(t)
    tri = (j[:, None] > j[None, :]).astype(BF16)
    n_pairs = d_attn // LANES
    kv_spec = pl.BlockSpec((rows_per_batch, d_attn), lambda b, i: (b, 0), pipeline_mode=pl.Buffered(1))
    tile_spec = pl.BlockSpec((t, d_attn), lambda b, i: (b * nq + i, 0))
    cast_specs = [_cast_specs(w, l, axis, ATTN_CAST_BLOCK, lambda b, i: b * nq + i, batch * nq) for w, l, axis in casts]
    return pl.pallas_call(
        functools.partial(_attn_kernel, layer=layer, n_pairs=n_pairs, n_casts=len(casts)),
        grid=(batch, nq),
        in_specs=[tile_spec, kv_spec, kv_spec, _resident(tri.shape), _resident(g.shape)]
        + [c[0] for c in cast_specs],
        out_specs=[tile_spec] + [c[1] for c in cast_specs],
        out_shape=[jax.ShapeDtypeStruct((rows, d_attn), BF16)] + [c[2] for c in cast_specs],
        scratch_shapes=[pltpu.VMEM((n_pairs, 2 * t, LANES), BF16), pltpu.VMEM((t, d_attn), F32),
                        pltpu.VMEM((n_pairs, 2 * t, t), F32)],
        compiler_params=_params(2),
        name="sb_attention",
    )(q, k, v, tri, g, *[w for w, _, _ in casts])


def _digest(v):
    v = v.astype(F32)
    rows, cols = v.shape
    top = jnp.max(v.reshape(rows // SUBLANES, SUBLANES, cols), axis=0)
    return functools.reduce(jnp.maximum, [top[:, c:c + LANES] for c in range(0, cols, LANES)])


def _zero_after(digests):
    bits = pltpu.bitcast(functools.reduce(jnp.maximum, digests), jnp.uint32)
    zero = ((bits >> 16) >> 16).astype(F32)
    return jnp.concatenate([zero, zero], axis=0).astype(BF16)


def _proj_mix_kernel(*refs, layer, n_stream, layout, n_casts, n_tiles, tiles_per_batch, d_attn, d_pool, d_conv,
                     q_scale):
    ns = n_stream
    g_ref, w_ref, w_pool_ref, pool_scale_ref, g_mix_ref, w_dw_ref, b_dw_ref, ln_g_ref, ln_b_ref, w_pw_ref = refs[ns:ns + 10]
    g_ref, pool_scale_ref, b_dw_ref, ln_g_ref, ln_b_ref = (
        _layer_row(r, layer) for r in (g_ref, pool_scale_ref, b_dw_ref, ln_g_ref, ln_b_ref))
    g_pool_ref = _layer_row(g_mix_ref, layer, d_attn, d_attn + d_pool)
    g_conv_ref = _layer_row(g_mix_ref, layer, d_attn + d_pool)
    cast_src = refs[ns + 10:ns + 10 + n_casts]
    q_ref, k_ref, v_ref, o_pool_ref, o_conv_ref = refs[ns + 10 + n_casts:ns + 15 + n_casts]
    cast_dst = refs[ns + 15 + n_casts:ns + 15 + 2 * n_casts]
    pool_ext, pool_sums, conv_ext, y_ref = refs[ns + 15 + 2 * n_casts:]
    i = pl.program_id(0)
    tile = jnp.minimum(i, n_tiles - 1)
    tm = q_ref.shape[0]

    @pl.when(i == 0)
    def _():
        pool_ext[...] = jnp.zeros_like(pool_ext)
        conv_ext[0] = jnp.zeros(conv_ext.shape[1:], F32)

    u = _rms(_stream_tile(refs[:ns], layout, tile), g_ref[...]).astype(BF16)

    def proj(lo, hi, wait_for=()):
        lhs = u
        if wait_for:
            head = jnp.concatenate([u[:16, :LANES] + _zero_after(wait_for), u[:16, LANES:]], axis=1)
            lhs = jnp.concatenate([head, u[16:]], axis=0)
        return jnp.dot(lhs, w_ref[:, lo:hi], preferred_element_type=F32)

    def mix_pool():
        n_pool = POOL_HALO + tm
        level, window_sum = pool_ext, []
        for s in range(len(POOL_WINDOWS)):
            start, shift, lo = SUBLANES * (s + 1), 2 ** s, s * LANES
            both = level[start:n_pool, lo:] + level[start - shift:n_pool - shift, lo:]
            window_sum.append(both[POOL_HALO - start:, :LANES])
            if s + 1 < len(POOL_WINDOWS):
                pool_sums[s, start:n_pool, lo:] = both
                level = pool_sums.at[s]
        ti = lax.rem(i + tiles_per_batch - 1, tiles_per_batch)
        pos = ti * tm + lax.broadcasted_iota(jnp.int32, (tm, 1), 0)
        mixed = []
        sq = jnp.zeros((tm, 1), F32)
        for gi, window in enumerate(POOL_WINDOWS):
            lanes = slice(gi * LANES, (gi + 1) * LANES)
            count = jnp.minimum(pos + 1, window).astype(F32)
            pooled = window_sum[gi] / count - pool_ext[POOL_HALO:, lanes]
            m = jnp.dot(pooled.astype(BF16), w_pool_ref[gi].astype(BF16), preferred_element_type=F32)
            m = m * pool_scale_ref[:, lanes]
            sq = sq + jnp.sum(m * m, axis=-1, keepdims=True)
            mixed.append(m)
        inv = lax.rsqrt(sq / d_pool + EPS)
        out = [(m * inv * g_pool_ref[:, gi * LANES:(gi + 1) * LANES]).astype(BF16) for gi, m in enumerate(mixed)]
        for gi, o in enumerate(out):
            o_pool_ref[:, gi * LANES:(gi + 1) * LANES] = o
        return [_digest(o) for o in out]

    n_ext = CONV_HALO + tm

    def conv_copies():
        digests = []
        for r in range(1, SUBLANES):
            shifted = conv_ext[0, r:r + n_ext - SUBLANES, :]
            conv_ext[r, :n_ext - SUBLANES, :] = shifted
            digests.append(_digest(shifted))
        return digests

    first_tap = CONV_HALO - (CONV_WIDTH - 1)

    def conv_rows(base):
        acc = jnp.broadcast_to(b_dw_ref[...], (CONV_CHUNK, d_conv))
        for tap in range(CONV_WIDTH):
            shift = (first_tap + tap) % SUBLANES
            lo = base + first_tap + tap - shift
            acc = acc + w_dw_ref[tap:tap + 1, :] * conv_ext[shift, lo:lo + CONV_CHUNK, :]
        y_ref[base:base + CONV_CHUNK, :] = acc
        return _digest(acc)

    def conv_finish():
        y = y_ref[...]
        mu = jnp.mean(y, axis=-1, keepdims=True)
        yc = y - mu
        var = jnp.mean(yc * yc, axis=-1, keepdims=True)
        y = yc * lax.rsqrt(var + EPS) * ln_g_ref[...] + ln_b_ref[...]
        y = y * jax.nn.sigmoid(y)
        o = jnp.dot(y.astype(BF16), w_pw_ref[...].astype(BF16), preferred_element_type=F32)
        o = _rms(o, g_conv_ref[...]).astype(BF16)
        o_conv_ref[...] = o
        return [_digest(o)]

    half = d_attn // 2
    chunks = list(range(0, tm, CONV_CHUNK))
    per_group = -(-len(chunks) // 4)
    conv_group = lambda g: [conv_rows(base) for base in chunks[g * per_group:(g + 1) * per_group]]
    q_ref[:, :half] = (proj(0, half) * q_scale).astype(BF16)
    q_ref[:, half:] = (proj(half, d_attn) * q_scale).astype(BF16)
    done = mix_pool()
    k_ref[:, :half] = proj(d_attn, d_attn + half, done).astype(BF16)
    done = conv_copies()
    k_ref[:, half:] = proj(d_attn + half, 2 * d_attn, done).astype(BF16)
    done = conv_group(0)
    v_ref[:, :half] = proj(2 * d_attn, 2 * d_attn + half, done).astype(BF16)
    done = conv_group(1)
    v_ref[:, half:] = proj(2 * d_attn + half, 3 * d_attn, done).astype(BF16)
    done = conv_group(2)
    o = 3 * d_attn
    new_pool = proj(o, o + d_pool, done)
    done = conv_group(3)
    o += d_pool
    conv_value = proj(o, o + d_conv, done)
    done = conv_finish()
    new_conv = conv_value * jax.nn.sigmoid(proj(o + d_conv, o + 2 * d_conv, done))
    _cast_blocks(cast_src, cast_dst)

    starts_batch = lax.rem(tile, tiles_per_batch) == 0
    pool_ext[:POOL_HALO, :] = jnp.where(starts_batch, 0.0, pool_ext[tm:, :])
    pool_ext[POOL_HALO:, :] = new_pool
    conv_ext[0, :CONV_HALO, :] = jnp.where(starts_batch, 0.0, conv_ext[0, tm:, :])
    conv_ext[0, CONV_HALO:, :] = new_conv


def _proj_mix(layer, h, g, w, w_pool, pool_scale, g_mix, w_dw, b_dw, ln_g, ln_b, w_pw, d_attn, rows_per_batch, casts):
    rows, d = h.shape
    d_pool, d_conv = pool_scale.shape[-1], w_pw.shape[-1]
    tm = PROJ_ROWS
    n = rows // tm
    head_dim = d_attn // N_HEADS_SB
    tile_of = lambda i: jnp.minimum(i, n - 1)
    cur = lambda width: pl.BlockSpec((tm, width), lambda i: (tile_of(i), 0))
    prev = lambda width: pl.BlockSpec((tm, width), lambda i: (jnp.maximum(i - 1, 0), 0))
    lr = functools.partial(_layer_resident, layer)
    stream, stream_specs, layout = _stream_inputs(h, tm, tile_of)
    cast_specs = [_cast_specs(w_, l, axis, PROJ_CAST_BLOCK, lambda i: i, n + 1) for w_, l, axis in casts]
    return pl.pallas_call(
        functools.partial(_proj_mix_kernel, layer=layer, n_stream=len(stream), layout=layout, n_casts=len(casts), n_tiles=n,
                          tiles_per_batch=rows_per_batch // tm, d_attn=d_attn, d_pool=d_pool, d_conv=d_conv,
                          q_scale=LOG2_E * head_dim ** -0.5),
        grid=(n + 1,),
        in_specs=stream_specs + [_resident(g.shape), _resident(w.shape), lr(w_pool.shape[1:]),
                                 _resident(pool_scale.shape), _resident(g_mix.shape), lr(w_dw.shape[1:]),
                                 _resident(b_dw.shape), _resident(ln_g.shape), _resident(ln_b.shape),
                                 lr(w_pw.shape[1:])] + [c[0] for c in cast_specs],
        out_specs=[cur(d_attn), cur(d_attn), cur(d_attn), prev(d_pool), prev(d_conv)] + [c[1] for c in cast_specs],
        out_shape=[jax.ShapeDtypeStruct((rows, d_attn), BF16)] * 3
        + [jax.ShapeDtypeStruct((rows, d_pool), BF16), jax.ShapeDtypeStruct((rows, d_conv), BF16)]
        + [c[2] for c in cast_specs],
        scratch_shapes=[pltpu.VMEM((POOL_HALO + tm, d_pool), F32),
                        pltpu.VMEM((len(POOL_WINDOWS) - 1, POOL_HALO + tm, d_pool), F32),
                        pltpu.VMEM((SUBLANES, CONV_HALO + tm, d_conv), F32), pltpu.VMEM((tm, d_conv), F32)],
        compiler_params=_params(1),
        name="in_proj_mixers",
    )(*stream, g, w, w_pool, pool_scale, g_mix, w_dw, b_dw, ln_g, ln_b, w_pw, *[w_ for w_, _, _ in casts])


def _out_proj_kernel(*refs, layer, n_stream, layout, n_casts):
    attn_ref, pool_ref, conv_ref, w_ref, g_ref = refs[n_stream:n_stream + 5]
    g_ref = _layer_row(g_ref, layer)
    o_ref = refs[n_stream + 5 + n_casts]
    _cast_blocks(refs[n_stream + 5:n_stream + 5 + n_casts], refs[n_stream + 6 + n_casts:])
    merged = jnp.concatenate([attn_ref[...], pool_ref[...], conv_ref[...]], axis=1)
    y = jnp.dot(merged, w_ref[...], preferred_element_type=F32)
    o_ref[...] = _stream_tile(refs[:n_stream], layout) + _rms(y, g_ref[...])


def _out_proj(layer, attn, pool, conv, h, w, g, casts):
    rows, d = h.shape
    tm = PROJ_ROWS
    row_spec = lambda n: pl.BlockSpec((tm, n), lambda i: (i, 0))
    stream, stream_specs, layout = _stream_inputs(h, tm)
    cast_specs = [_cast_specs(w_, l, axis, PROJ_CAST_BLOCK, lambda i: i, rows // tm) for w_, l, axis in casts]
    return pl.pallas_call(
        functools.partial(_out_proj_kernel, layer=layer, n_stream=len(stream), layout=layout, n_casts=len(casts)),
        grid=(rows // tm,),
        in_specs=stream_specs + [row_spec(attn.shape[1]), row_spec(pool.shape[1]), row_spec(conv.shape[1]),
                                 _resident(w.shape), _resident(g.shape)] + [c[0] for c in cast_specs],
        out_specs=[row_spec(d)] + [c[1] for c in cast_specs],
        out_shape=[jax.ShapeDtypeStruct((rows, d), F32)] + [c[2] for c in cast_specs],
        compiler_params=_params(1),
        name="out_proj",
    )(*stream, attn, pool, conv, w, g, *[w_ for w_, _, _ in casts])


def _ffn_kernel(h_ref, g_pre_ref, w_gate_ref, w_up_ref, w_down_ref, g_post_ref, o_ref, u_ref, acc_ref, *, layer):
    g_pre_ref, g_post_ref = _layer_row(g_pre_ref, layer), _layer_row(g_post_ref, layer)
    c = pl.program_id(1)

    @pl.when(c == 0)
    def _():
        u_ref[...] = _rms(h_ref[...], g_pre_ref[...]).astype(BF16)
        acc_ref[...] = jnp.zeros_like(acc_ref)

    u = u_ref[...]
    gate = jnp.dot(u, w_gate_ref[...], preferred_element_type=F32)
    up = jnp.dot(u, w_up_ref[...], preferred_element_type=F32)
    a = (gate * jax.nn.sigmoid(gate) * up).astype(BF16)
    tf = w_down_ref.shape[0]
    for n in range(0, acc_ref.shape[1], tf):
        acc_ref[:, n:n + tf] += jnp.dot(a, w_down_ref[:, n:n + tf], preferred_element_type=F32)

    @pl.when(c == pl.num_programs(1) - 1)
    def _():
        o_ref[...] = h_ref[...] + _rms(acc_ref[...], g_post_ref[...])


def _ffn(layer, h, g_pre, w_gate, w_up, w_down, g_post, keep=None):
    rows, d = h.shape
    d_ff = w_gate.shape[1]
    tf = FFN_COLS
    if keep is None:
        tm, n_tiles = FFN_ROWS, rows // FFN_ROWS
        h_spec = pl.BlockSpec((tm, d), lambda i, c: (i, 0))
    else:
        rows_per_batch, first, count = keep
        tm = LAST_FFN_ROWS
        per_batch = count // tm
        n_tiles = (rows // rows_per_batch) * per_batch
        h_spec = pl.BlockSpec((pl.Element(tm), pl.Element(d)),
                              lambda i, c: (pl.multiple_of(
                                  (i // per_batch) * rows_per_batch + first + (i % per_batch) * tm, SUBLANES), 0))
        assert rows_per_batch % SUBLANES == 0 and first % SUBLANES == 0
    return pl.pallas_call(
        functools.partial(_ffn_kernel, layer=layer),
        grid=(n_tiles, d_ff // tf),
        in_specs=[h_spec, _resident(g_pre.shape),
                  pl.BlockSpec((d, tf), lambda i, c: (0, c)), pl.BlockSpec((d, tf), lambda i, c: (0, c)),
                  pl.BlockSpec((tf, d), lambda i, c: (c, 0)), _resident(g_post.shape)],
        out_specs=pl.BlockSpec((tm, d), lambda i, c: (i, 0)),
        out_shape=jax.ShapeDtypeStruct((n_tiles * tm, d), F32),
        scratch_shapes=[pltpu.VMEM((tm, d), BF16), pltpu.VMEM((tm, d), F32)],
        compiler_params=_params(2),
        name="swiglu_ffn",
    )(h, g_pre, w_gate, w_up, w_down, g_post)


def kernel(x, meta_tokens, pre_mix_g, w_in, w_pool, pool_scale, w_dw, b_dw, conv_ln_g, conv_ln_b, w_pw, mix_out_g,
           w_out, post_mix_g, pre_ffn_g, w_gate, w_up, w_down, post_ffn_g):
    batch, seq, d = x.shape
    depth = w_in.shape[0]
    d_pool = pool_scale.shape[1]
    d_conv = w_pw.shape[1]
    d_attn = w_out.shape[1] - d_pool - d_conv
    assert d_pool == LANES * len(POOL_WINDOWS) and d_attn % (2 * LANES) == 0
    assert POOL_WINDOWS == tuple(2 ** (g + 1) for g in range(len(POOL_WINDOWS)))
    assert d_attn // N_HEADS_SB == LANES // 2

    length = N_META + seq
    rows_per_batch = -(-length // ROW_ALIGN) * ROW_ALIGN
    assert rows_per_batch % SEQ_TILE == 0 and (batch * rows_per_batch) % FFN_ROWS == 0 and seq % LAST_FFN_ROWS == 0
    assert meta_tokens.shape[0] == N_META
    h = _Tokens(x.reshape(batch * seq, d), meta_tokens.astype(x.dtype), seq, rows_per_batch)

    w_in_l = w_in[0].astype(BF16)

    for i in range(depth):
        q, k, v, o_pool, o_conv, w_gate_l, w_up_l = _proj_mix(
            i, h, pre_mix_g, w_in_l, w_pool, pool_scale, mix_out_g, w_dw, b_dw, conv_ln_g, conv_ln_b, w_pw, d_attn,
            rows_per_batch, [(w_gate, i, 2), (w_up, i, 2)])
        o_attn, w_out_l, *w_in_next = _attention(i, q, k, v, mix_out_g, batch, rows_per_batch,
                                                 [(w_out, i, 2)] + ([(w_in, i + 1, 2)] if i + 1 < depth else []))
        if w_in_next:
            w_in_l = w_in_next[0]
        h, w_down_l = _out_proj(i, o_attn, o_pool, o_conv, h, w_out_l, post_mix_g, [(w_down, i, 1)])
        keep = (rows_per_batch, N_META, seq) if i == depth - 1 else None
        h = _ffn(i, h, pre_ffn_g, w_gate_l, w_up_l, w_down_l, post_ffn_g, keep)

    return h.reshape(batch, seq, d)
```

```python
import functools
from typing import NamedTuple

import jax
import jax.numpy as jnp
from jax import lax
from jax.experimental import pallas as pl
from jax.experimental.pallas import tpu as pltpu

N_META = 16
N_HEADS_SB = 16
POOL_WINDOWS = (2, 4, 8, 16)
CONV_WIDTH = 31
EPS = 1e-6

LANES = 128
SUBLANES = 8
VMEM_BYTES_V7X = 64 * 1024 * 1024
VMEM_LIMIT = VMEM_BYTES_V7X - 6 * 1024 * 1024

SEQ_TILE = 128
TOP_ROWS = 32
PROJ_ROWS = 384
ROW_ALIGN = PROJ_ROWS
FFN_ROWS = 768
LAST_FFN_ROWS = 512
FFN_COLS = 512
CONV_HALO = 32
POOL_HALO = SUBLANES * len(POOL_WINDOWS)
CONV_CHUNK = 32
ATTN_CAST_BLOCK = 128
PROJ_CAST_BLOCK = 256

LOG2_E = 1.4426950408889634
LOG2_WEIGHT_FLOOR = -127.0

F32 = jnp.float32
BF16 = jnp.bfloat16


def _rms(x, g):
    return x * lax.rsqrt(jnp.mean(x * x, axis=-1, keepdims=True) + EPS) * g


def _resident(shape):
    return pl.BlockSpec(shape, lambda *_: (0,) * len(shape), pipeline_mode=pl.Buffered(1))


def _layer_resident(layer, shape):
    return pl.BlockSpec((None,) + tuple(shape), lambda *_: (layer,) + (0,) * len(shape),
                        pipeline_mode=pl.Buffered(1))


def _layer_row(ref, layer, lo=0, hi=None):
    return ref.at[layer:layer + 1, lo:ref.shape[1] if hi is None else hi]


def _params(n_grid_axes):
    return pltpu.CompilerParams(dimension_semantics=("arbitrary",) * n_grid_axes, vmem_limit_bytes=VMEM_LIMIT)


class _Tokens(NamedTuple):
    x: jax.Array
    meta: jax.Array
    seq: int
    rows_per_batch: int

    @property
    def shape(self):
        return (self.x.shape[0] // self.seq * self.rows_per_batch, self.x.shape[1])


def _stream_inputs(h, tm, tile_of=lambda i: i):
    if not isinstance(h, _Tokens):
        return [h], [pl.BlockSpec((tm, h.shape[1]), lambda i, *_: (tile_of(i), 0))], None
    seq, d = h.seq, h.x.shape[1]
    n_meta = h.meta.shape[0]
    tiles_per_batch = h.rows_per_batch // tm
    n_pad = h.rows_per_batch - n_meta - seq
    assert h.rows_per_batch % tm == 0 and n_pad <= tm and n_meta <= tm
    assert all(n % SUBLANES == 0 for n in (n_meta, n_pad, seq, tm))

    def start(i):
        b, t = i // tiles_per_batch, i % tiles_per_batch
        return pl.multiple_of(b * seq + jnp.clip(t * tm - n_meta, 0, seq - tm), SUBLANES)

    spec = pl.BlockSpec((pl.Element(tm), pl.Element(d)), lambda i, *_: (start(tile_of(i)), 0))
    return [h.x, h.meta], [spec, _resident(h.meta.shape)], (tiles_per_batch, n_pad)


def _stream_tile(stream_refs, layout, tile=None):
    if layout is None:
        return stream_refs[0][...]
    x_ref, meta_ref = stream_refs
    tiles_per_batch, n_pad = layout
    blk = x_ref[...]
    tm, d = blk.shape
    t = lax.rem(pl.program_id(0) if tile is None else tile, tiles_per_batch)
    first = jnp.concatenate([meta_ref[...], blk[:tm - meta_ref.shape[0]]], axis=0)
    last = jnp.concatenate([blk[n_pad:], jnp.zeros((n_pad, d), blk.dtype)], axis=0)
    return jnp.where(t == 0, first, jnp.where(t == tiles_per_batch - 1, last, blk))


def _cast_specs(w, layer, axis, block_size, step_of, n_steps):
    k, n = w.shape[1:]
    n_blocks = w.shape[axis] // block_size
    assert w.shape[axis] % block_size == 0 and n_blocks <= n_steps
    block = (block_size, n) if axis == 1 else (k, block_size)

    def index(*grid_ids):
        j = jnp.minimum(step_of(*grid_ids), n_blocks - 1)
        return (j, 0) if axis == 1 else (0, j)

    return (pl.BlockSpec((None,) + block, lambda *g: (layer,) + index(*g)), pl.BlockSpec(block, index),
            jax.ShapeDtypeStruct((k, n), BF16))


def _cast_blocks(cast_src, cast_dst):
    for src, dst in zip(cast_src, cast_dst):
        dst[...] = src[...].astype(BF16)


def _attn_kernel(*refs, layer, n_pairs, n_casts):
    q_ref, k_ref, v_ref, tri_ref, g_ref = refs[:5]
    g_ref = _layer_row(g_ref, layer, 0, n_pairs * LANES)
    o_ref = refs[5 + n_casts]
    q2_ref, acc_ref, c_ref = refs[6 + 2 * n_casts:]
    _cast_blocks(refs[5:5 + n_casts], refs[6 + n_casts:6 + 2 * n_casts])
    t = SEQ_TILE
    qi = pl.program_id(1)
    pairs = range(n_pairs)
    lanes = [slice(p * LANES, (p + 1) * LANES) for p in pairs]
    first_head = lax.broadcasted_iota(jnp.int32, (t, LANES), 1) < LANES // 2
    row = lax.broadcasted_iota(jnp.int32, (2 * t, t), 0)
    col = lax.broadcasted_iota(jnp.int32, (2 * t, t), 1)
    causal = col < (row & (t - 1))
    zero = jnp.zeros((), BF16)
    for p in pairs:
        qp = q_ref[:, lanes[p]]
        q2_ref[p] = jnp.concatenate([jnp.where(first_head, qp, zero), jnp.where(first_head, zero, qp)], axis=0)

    def head_rows(ref, p, r):
        return ref[p] if r == t else jnp.concatenate([ref[p, :r], ref[p, t:t + r]], axis=0)

    def key_tile(kb, r, diagonal=False):
        start = pl.multiple_of(kb * t, t)
        z = [lax.dot_general(head_rows(q2_ref, p, r), k_ref[pl.ds(start, t), lanes[p]], (((1,), (1,)), ((), ())),
                             preferred_element_type=F32) for p in pairs]
        v = [v_ref[pl.ds(start, t), lanes[p]] for p in pairs]
        c_old = None if diagonal else [head_rows(c_ref, p, r) for p in pairs]
        log_beta, later, c_new = [], [], []
        for p in pairs:
            z_pos = jnp.maximum(z[p], 0.0)
            z_neg = z[p] - z_pos
            log_term = jnp.log(1.0 + jnp.exp2(z_neg - z_pos)) * LOG2_E
            soft = z_pos + log_term
            if diagonal:
                soft = jnp.where(causal, soft, 0.0)
            later.append(jnp.dot(soft.astype(BF16), tri_ref[...], preferred_element_type=F32))
            log_beta.append(z_neg - log_term)
            c = jnp.sum(soft, axis=1, keepdims=True)
            c_new.append(jnp.broadcast_to(c, soft.shape) if diagonal else c_old[p] + c)
        o = []
        for p in pairs:
            x = log_beta[p] - later[p]
            if not diagonal:
                x = x - c_old[p]
            w = jnp.exp2(x)
            if diagonal:
                w = jnp.where(causal, w, 0.0)
            o.append(jnp.dot(w.astype(BF16), v[p], preferred_element_type=F32))
        c_min = None
        for p in pairs:
            first = lax.broadcasted_iota(jnp.int32, (r, LANES), 1) < LANES // 2
            o_pair = jnp.where(first, o[p][:r], o[p][r:])
            if diagonal:
                acc_ref[:, lanes[p]] = o_pair
            else:
                acc_ref[:r, lanes[p]] += o_pair
            if r == t:
                c_ref[p] = c_new[p]
            else:
                c_ref[p, :r] = c_new[p][:r]
                c_ref[p, t:t + r] = c_new[p][r:]
            c_min = c_new[p] if c_min is None else jnp.minimum(c_min, c_new[p])
        if r < t:
            return jnp.min(c_min)
        top = jnp.minimum(jnp.min(c_min[:TOP_ROWS]), jnp.min(c_min[t:t + TOP_ROWS]))
        rest = jnp.minimum(jnp.min(c_min[TOP_ROWS:t]), jnp.min(c_min[t + TOP_ROWS:]))
        return top, rest

    def unfinished(c_min):
        return c_min < -LOG2_WEIGHT_FLOOR

    def full_body(s):
        kb, _, _ = s
        return (kb - 1,) + key_tile(kb, t)

    def top_body(s):
        kb, _ = s
        return kb - 1, key_tile(kb, TOP_ROWS)

    kb, top, _ = lax.while_loop(lambda s: jnp.logical_and(s[0] >= 0, unfinished(s[2])), full_body,
                                (qi - 1,) + key_tile(qi, t, diagonal=True))
    lax.while_loop(lambda s: jnp.logical_and(s[0] >= 0, unfinished(s[1])), top_body, (kb, top))
    o_ref[...] = _rms(acc_ref[...], g_ref[...]).astype(BF16)


def _attention(layer, q, k, v, g, batch, rows_per_batch, casts):
    rows, d_attn = q.shape
    t = SEQ_TILE
    nq = rows_per_batch // t
    j = jnp.arange(t)
    tri = (j[:, None] > j[None, :]).astype(BF16)
    n_pairs = d_attn // LANES
    kv_spec = pl.BlockSpec((rows_per_batch, d_attn), lambda b, i: (b, 0), pipeline_mode=pl.Buffered(1))
    tile_spec = pl.BlockSpec((t, d_attn), lambda b, i: (b * nq + i, 0))
    cast_specs = [_cast_specs(w, l, axis, ATTN_CAST_BLOCK, lambda b, i: b * nq + i, batch * nq) for w, l, axis in casts]
    return pl.pallas_call(
        functools.partial(_attn_kernel, layer=layer, n_pairs=n_pairs, n_casts=len(casts)),
        grid=(batch, nq),
        in_specs=[tile_spec, kv_spec, kv_spec, _resident(tri.shape), _resident(g.shape)]
        + [c[0] for c in cast_specs],
        out_specs=[tile_spec] + [c[1] for c in cast_specs],
        out_shape=[jax.ShapeDtypeStruct((rows, d_attn), BF16)] + [c[2] for c in cast_specs],
        scratch_shapes=[pltpu.VMEM((n_pairs, 2 * t, LANES), BF16), pltpu.VMEM((t, d_attn), F32),
                        pltpu.VMEM((n_pairs, 2 * t, t), F32)],
        compiler_params=_params(2),
        name="sb_attention",
    )(q, k, v, tri, g, *[w for w, _, _ in casts])


def _digest(v):
    v = v.astype(F32)
    rows, cols = v.shape
    top = jnp.max(v.reshape(rows // SUBLANES, SUBLANES, cols), axis=0)
    return functools.reduce(jnp.maximum, [top[:, c:c + LANES] for c in range(0, cols, LANES)])


def _zero_after(digests):
    bits = pltpu.bitcast(functools.reduce(jnp.maximum, digests), jnp.uint32)
    zero = ((bits >> 16) >> 16).astype(F32)
    return jnp.concatenate([zero, zero], axis=0).astype(BF16)


def _proj_mix_kernel(*refs, layer, n_stream, layout, n_casts, n_tiles, tiles_per_batch, d_attn, d_pool, d_conv,
                     q_scale):
    ns = n_stream
    g_ref, w_ref, w_pool_ref, pool_scale_ref, g_mix_ref, w_dw_ref, b_dw_ref, ln_g_ref, ln_b_ref, w_pw_ref = refs[ns:ns + 10]
    g_ref, pool_scale_ref, b_dw_ref, ln_g_ref, ln_b_ref = (
        _layer_row(r, layer) for r in (g_ref, pool_scale_ref, b_dw_ref, ln_g_ref, ln_b_ref))
    g_pool_ref = _layer_row(g_mix_ref, layer, d_attn, d_attn + d_pool)
    g_conv_ref = _layer_row(g_mix_ref, layer, d_attn + d_pool)
    cast_src = refs[ns + 10:ns + 10 + n_casts]
    q_ref, k_ref, v_ref, o_pool_ref, o_conv_ref = refs[ns + 10 + n_casts:ns + 15 + n_casts]
    cast_dst = refs[ns + 15 + n_casts:ns + 15 + 2 * n_casts]
    pool_ext, pool_sums, conv_ext, y_ref = refs[ns + 15 + 2 * n_casts:]
    i = pl.program_id(0)
    tile = jnp.minimum(i, n_tiles - 1)
    tm = q_ref.shape[0]

    @pl.when(i == 0)
    def _():
        pool_ext[...] = jnp.zeros_like(pool_ext)
        conv_ext[0] = jnp.zeros(conv_ext.shape[1:], F32)

    u = _rms(_stream_tile(refs[:ns], layout, tile), g_ref[...]).astype(BF16)

    def proj(lo, hi, wait_for=()):
        lhs = u
        if wait_for:
            head = jnp.concatenate([u[:16, :LANES] + _zero_after(wait_for), u[:16, LANES:]], axis=1)
            lhs = jnp.concatenate([head, u[16:]], axis=0)
        return jnp.dot(lhs, w_ref[:, lo:hi], preferred_element_type=F32)

    def mix_pool():
        n_pool = POOL_HALO + tm
        level, window_sum = pool_ext, []
        for s in range(len(POOL_WINDOWS)):
            start, shift, lo = SUBLANES * (s + 1), 2 ** s, s * LANES
            both = level[start:n_pool, lo:] + level[start - shift:n_pool - shift, lo:]
            window_sum.append(both[POOL_HALO - start:, :LANES])
            if s + 1 < len(POOL_WINDOWS):
                pool_sums[s, start:n_pool, lo:] = both
                level = pool_sums.at[s]
        ti = lax.rem(i + tiles_per_batch - 1, tiles_per_batch)
        pos = ti * tm + lax.broadcasted_iota(jnp.int32, (tm, 1), 0)
        mixed = []
        sq = jnp.zeros((tm, 1), F32)
        for gi, window in enumerate(POOL_WINDOWS):
            lanes = slice(gi * LANES, (gi + 1) * LANES)
            count = jnp.minimum(pos + 1, window).astype(F32)
            pooled = window_sum[gi] / count - pool_ext[POOL_HALO:, lanes]
            m = jnp.dot(pooled.astype(BF16), w_pool_ref[gi].astype(BF16), preferred_element_type=F32)
            m = m * pool_scale_ref[:, lanes]
            sq = sq + jnp.sum(m * m, axis=-1, keepdims=True)
            mixed.append(m)
        inv = lax.rsqrt(sq / d_pool + EPS)
        out = [(m * inv * g_pool_ref[:, gi * LANES:(gi + 1) * LANES]).astype(BF16) for gi, m in enumerate(mixed)]
        for gi, o in enumerate(out):
            o_pool_ref[:, gi * LANES:(gi + 1) * LANES] = o
        return [_digest(o) for o in out]

    n_ext = CONV_HALO + tm

    def conv_copies():
        digests = []
        for r in range(1, SUBLANES):
            shifted = conv_ext[0, r:r + n_ext - SUBLANES, :]
            conv_ext[r, :n_ext - SUBLANES, :] = shifted
            digests.append(_digest(shifted))
        return digests

    first_tap = CONV_HALO - (CONV_WIDTH - 1)

    def conv_rows(base):
        acc = jnp.broadcast_to(b_dw_ref[...], (CONV_CHUNK, d_conv))
        for tap in range(CONV_WIDTH):
            shift = (first_tap + tap) % SUBLANES
            lo = base + first_tap + tap - shift
            acc = acc + w_dw_ref[tap:tap + 1, :] * conv_ext[shift, lo:lo + CONV_CHUNK, :]
        y_ref[base:base + CONV_CHUNK, :] = acc
        return _digest(acc)

    def conv_finish():
        y = y_ref[...]
        mu = jnp.mean(y, axis=-1, keepdims=True)
        yc = y - mu
        var = jnp.mean(yc * yc, axis=-1, keepdims=True)
        y = yc * lax.rsqrt(var + EPS) * ln_g_ref[...] + ln_b_ref[...]
        y = y * jax.nn.sigmoid(y)
        o = jnp.dot(y.astype(BF16), w_pw_ref[...].astype(BF16), preferred_element_type=F32)
        o = _rms(o, g_conv_ref[...]).astype(BF16)
        o_conv_ref[...] = o
        return [_digest(o)]

    half = d_attn // 2
    chunks = list(range(0, tm, CONV_CHUNK))
    per_group = -(-len(chunks) // 4)
    conv_group = lambda g: [conv_rows(base) for base in chunks[g * per_group:(g + 1) * per_group]]
    q_ref[:, :half] = (proj(0, half) * q_scale).astype(BF16)
    q_ref[:, half:] = (proj(half, d_attn) * q_scale).astype(BF16)
    done = mix_pool()
    k_ref[:, :half] = proj(d_attn, d_attn + half, done).astype(BF16)
    done = conv_copies()
    k_ref[:, half:] = proj(d_attn + half, 2 * d_attn, done).astype(BF16)
    done = conv_group(0)
    v_ref[:, :half] = proj(2 * d_attn, 2 * d_attn + half, done).astype(BF16)
    done = conv_group(1)
    v_ref[:, half:] = proj(2 * d_attn + half, 3 * d_attn, done).astype(BF16)
    done = conv_group(2)
    o = 3 * d_attn
    new_pool = proj(o, o + d_pool, done)
    done = conv_group(3)
    o += d_pool
    conv_value = proj(o, o + d_conv, done)
    done = conv_finish()
    new_conv = conv_value * jax.nn.sigmoid(proj(o + d_conv, o + 2 * d_conv, done))
    _cast_blocks(cast_src, cast_dst)

    starts_batch = lax.rem(tile, tiles_per_batch) == 0
    pool_ext[:POOL_HALO, :] = jnp.where(starts_batch, 0.0, pool_ext[tm:, :])
    pool_ext[POOL_HALO:, :] = new_pool
    conv_ext[0, :CONV_HALO, :] = jnp.where(starts_batch, 0.0, conv_ext[0, tm:, :])
    conv_ext[0, CONV_HALO:, :] = new_conv


def _proj_mix(layer, h, g, w, w_pool, pool_scale, g_mix, w_dw, b_dw, ln_g, ln_b, w_pw, d_attn, rows_per_batch, casts):
    rows, d = h.shape
    d_pool, d_conv = pool_scale.shape[-1], w_pw.shape[-1]
    tm = PROJ_ROWS
    n = rows // tm
    head_dim = d_attn // N_HEADS_SB
    tile_of = lambda i: jnp.minimum(i, n - 1)
    cur = lambda width: pl.BlockSpec((tm, width), lambda i: (tile_of(i), 0))
    prev = lambda width: pl.BlockSpec((tm, width), lambda i: (jnp.maximum(i - 1, 0), 0))
    lr = functools.partial(_layer_resident, layer)
    stream, stream_specs, layout = _stream_inputs(h, tm, tile_of)
    cast_specs = [_cast_specs(w_, l, axis, PROJ_CAST_BLOCK, lambda i: i, n + 1) for w_, l, axis in casts]
    return pl.pallas_call(
        functools.partial(_proj_mix_kernel, layer=layer, n_stream=len(stream), layout=layout, n_casts=len(casts), n_tiles=n,
                          tiles_per_batch=rows_per_batch // tm, d_attn=d_attn, d_pool=d_pool, d_conv=d_conv,
                          q_scale=LOG2_E * head_dim ** -0.5),
        grid=(n + 1,),
        in_specs=stream_specs + [_resident(g.shape), _resident(w.shape), lr(w_pool.shape[1:]),
                                 _resident(pool_scale.shape), _resident(g_mix.shape), lr(w_dw.shape[1:]),
                                 _resident(b_dw.shape), _resident(ln_g.shape), _resident(ln_b.shape),
                                 lr(w_pw.shape[1:])] + [c[0] for c in cast_specs],
        out_specs=[cur(d_attn), cur(d_attn), cur(d_attn), prev(d_pool), prev(d_conv)] + [c[1] for c in cast_specs],
        out_shape=[jax.ShapeDtypeStruct((rows, d_attn), BF16)] * 3
        + [jax.ShapeDtypeStruct((rows, d_pool), BF16), jax.ShapeDtypeStruct((rows, d_conv), BF16)]
        + [c[2] for c in cast_specs],
        scratch_shapes=[pltpu.VMEM((POOL_HALO + tm, d_pool), F32),
                        pltpu.VMEM((len(POOL_WINDOWS) - 1, POOL_HALO + tm, d_pool), F32),
                        pltpu.VMEM((SUBLANES, CONV_HALO + tm, d_conv), F32), pltpu.VMEM((tm, d_conv), F32)],
        compiler_params=_params(1),
        name="in_proj_mixers",
    )(*stream, g, w, w_pool, pool_scale, g_mix, w_dw, b_dw, ln_g, ln_b, w_pw, *[w_ for w_, _, _ in casts])


def _out_proj_kernel(*refs, layer, n_stream, layout, n_casts):
    attn_ref, pool_ref, conv_ref, w_ref, g_ref = refs[n_stream:n_stream + 5]
    g_ref = _layer_row(g_ref, layer)
    o_ref = refs[n_stream + 5 + n_casts]
    _cast_blocks(refs[n_stream + 5:n_stream + 5 + n_casts], refs[n_stream + 6 + n_casts:])
    merged = jnp.concatenate([attn_ref[...], pool_ref[...], conv_ref[...]], axis=1)
    y = jnp.dot(merged, w_ref[...], preferred_element_type=F32)
    o_ref[...] = _stream_tile(refs[:n_stream], layout) + _rms(y, g_ref[...])


def _out_proj(layer, attn, pool, conv, h, w, g, casts):
    rows, d = h.shape
    tm = PROJ_ROWS
    row_spec = lambda n: pl.BlockSpec((tm, n), lambda i: (i, 0))
    stream, stream_specs, layout = _stream_inputs(h, tm)
    cast_specs = [_cast_specs(w_, l, axis, PROJ_CAST_BLOCK, lambda i: i, rows // tm) for w_, l, axis in casts]
    return pl.pallas_call(
        functools.partial(_out_proj_kernel, layer=layer, n_stream=len(stream), layout=layout, n_casts=len(casts)),
        grid=(rows // tm,),
        in_specs=stream_specs + [row_spec(attn.shape[1]), row_spec(pool.shape[1]), row_spec(conv.shape[1]),
                                 _resident(w.shape), _resident(g.shape)] + [c[0] for c in cast_specs],
        out_specs=[row_spec(d)] + [c[1] for c in cast_specs],
        out_shape=[jax.ShapeDtypeStruct((rows, d), F32)] + [c[2] for c in cast_specs],
        compiler_params=_params(1),
        name="out_proj",
    )(*stream, attn, pool, conv, w, g, *[w_ for w_, _, _ in casts])


def _ffn_kernel(h_ref, g_pre_ref, w_gate_ref, w_up_ref, w_down_ref, g_post_ref, o_ref, u_ref, acc_ref, *, layer):
    g_pre_ref, g_post_ref = _layer_row(g_pre_ref, layer), _layer_row(g_post_ref, layer)
    c = pl.program_id(1)

    @pl.when(c == 0)
    def _():
        u_ref[...] = _rms(h_ref[...], g_pre_ref[...]).astype(BF16)
        acc_ref[...] = jnp.zeros_like(acc_ref)

    u = u_ref[...]
    gate = jnp.dot(u, w_gate_ref[...], preferred_element_type=F32)
    up = jnp.dot(u, w_up_ref[...], preferred_element_type=F32)
    a = (gate * jax.nn.sigmoid(gate) * up).astype(BF16)
    tf = w_down_ref.shape[0]
    for n in range(0, acc_ref.shape[1], tf):
        acc_ref[:, n:n + tf] += jnp.dot(a, w_down_ref[:, n:n + tf], preferred_element_type=F32)

    @pl.when(c == pl.num_programs(1) - 1)
    def _():
        o_ref[...] = h_ref[...] + _rms(acc_ref[...], g_post_ref[...])


def _ffn(layer, h, g_pre, w_gate, w_up, w_down, g_post, keep=None):
    rows, d = h.shape
    d_ff = w_gate.shape[1]
    tf = FFN_COLS
    if keep is None:
        tm, n_tiles = FFN_ROWS, rows // FFN_ROWS
        h_spec = pl.BlockSpec((tm, d), lambda i, c: (i, 0))
    else:
        rows_per_batch, first, count = keep
        tm = LAST_FFN_ROWS
        per_batch = count // tm
        n_tiles = (rows // rows_per_batch) * per_batch
        h_spec = pl.BlockSpec((pl.Element(tm), pl.Element(d)),
                              lambda i, c: (pl.multiple_of(
                                  (i // per_batch) * rows_per_batch + first + (i % per_batch) * tm, SUBLANES), 0))
        assert rows_per_batch % SUBLANES == 0 and first % SUBLANES == 0
    return pl.pallas_call(
        functools.partial(_ffn_kernel, layer=layer),
        grid=(n_tiles, d_ff // tf),
        in_specs=[h_spec, _resident(g_pre.shape),
                  pl.BlockSpec((d, tf), lambda i, c: (0, c)), pl.BlockSpec((d, tf), lambda i, c: (0, c)),
                  pl.BlockSpec((tf, d), lambda i, c: (c, 0)), _resident(g_post.shape)],
        out_specs=pl.BlockSpec((tm, d), lambda i, c: (i, 0)),
        out_shape=jax.ShapeDtypeStruct((n_tiles * tm, d), F32),
        scratch_shapes=[pltpu.VMEM((tm, d), BF16), pltpu.VMEM((tm, d), F32)],
        compiler_params=_params(2),
        name="swiglu_ffn",
    )(h, g_pre, w_gate, w_up, w_down, g_post)


def kernel(x, meta_tokens, pre_mix_g, w_in, w_pool, pool_scale, w_dw, b_dw, conv_ln_g, conv_ln_b, w_pw, mix_out_g,
           w_out, post_mix_g, pre_ffn_g, w_gate, w_up, w_down, post_ffn_g):
    batch, seq, d = x.shape
    depth = w_in.shape[0]
    d_pool = pool_scale.shape[1]
    d_conv = w_pw.shape[1]
    d_attn = w_out.shape[1] - d_pool - d_conv
    assert d_pool == LANES * len(POOL_WINDOWS) and d_attn % (2 * LANES) == 0
    assert POOL_WINDOWS == tuple(2 ** (g + 1) for g in range(len(POOL_WINDOWS)))
    assert d_attn // N_HEADS_SB == LANES // 2

    length = N_META + seq
    rows_per_batch = -(-length // ROW_ALIGN) * ROW_ALIGN
    assert rows_per_batch % SEQ_TILE == 0 and (batch * rows_per_batch) % FFN_ROWS == 0 and seq % LAST_FFN_ROWS == 0
    assert meta_tokens.shape[0] == N_META
    h = _Tokens(x.reshape(batch * seq, d), meta_tokens.astype(x.dtype), seq, rows_per_batch)

    w_in_l = w_in[0].astype(BF16)

    for i in range(depth):
        q, k, v, o_pool, o_conv, w_gate_l, w_up_l = _proj_mix(
            i, h, pre_mix_g, w_in_l, w_pool, pool_scale, mix_out_g, w_dw, b_dw, conv_ln_g, conv_ln_b, w_pw, d_attn,
            rows_per_batch, [(w_gate, i, 2), (w_up, i, 2)])
        o_attn, w_out_l, *w_in_next = _attention(i, q, k, v, mix_out_g, batch, rows_per_batch,
                                                 [(w_out, i, 2)] + ([(w_in, i + 1, 2)] if i + 1 < depth else []))
        if w_in_next:
            w_in_l = w_in_next[0]
        h, w_down_l = _out_proj(i, o_attn, o_pool, o_conv, h, w_out_l, post_mix_g, [(w_down, i, 1)])
        keep = (rows_per_batch, N_META, seq) if i == depth - 1 else None
        h = _ffn(i, h, pre_ffn_g, w_gate_l, w_up_l, w_down_l, post_ffn_g, keep)

    return h.reshape(batch, seq, d)
```

```python
import functools
from typing import NamedTuple

import jax
import jax.numpy as jnp
from jax import lax
from jax.experimental import pallas as pl
from jax.experimental.pallas import tpu as pltpu

N_META = 16
N_HEADS_SB = 16
POOL_WINDOWS = (2, 4, 8, 16)
CONV_WIDTH = 31
EPS = 1e-6

LANES = 128
SUBLANES = 8
VMEM_BYTES_V7X = 64 * 1024 * 1024
VMEM_LIMIT = VMEM_BYTES_V7X - 6 * 1024 * 1024

SEQ_TILE = 128
TOP_ROWS = 32
PROJ_ROWS = 384
ROW_ALIGN = PROJ_ROWS
FFN_ROWS = 768
LAST_FFN_ROWS = 512
FFN_COLS = 512
CONV_HALO = 32
POOL_HALO = SUBLANES * len(POOL_WINDOWS)
CONV_CHUNK = 32
ATTN_CAST_BLOCK = 128
PROJ_CAST_BLOCK = 256

LOG2_E = 1.4426950408889634
LOG2_WEIGHT_FLOOR = -127.0

F32 = jnp.float32
BF16 = jnp.bfloat16


def _rms(x, g):
    return x * lax.rsqrt(jnp.mean(x * x, axis=-1, keepdims=True) + EPS) * g


def _resident(shape):
    return pl.BlockSpec(shape, lambda *_: (0,) * len(shape), pipeline_mode=pl.Buffered(1))


def _layer_resident(layer, shape):
    return pl.BlockSpec((None,) + tuple(shape), lambda *_: (layer,) + (0,) * len(shape),
                        pipeline_mode=pl.Buffered(1))


def _layer_row(ref, layer, lo=0, hi=None):
    return ref.at[layer:layer + 1, lo:ref.shape[1] if hi is None else hi]


def _params(n_grid_axes):
    return pltpu.CompilerParams(dimension_semantics=("arbitrary",) * n_grid_axes, vmem_limit_bytes=VMEM_LIMIT)


class _Tokens(NamedTuple):
    x: jax.Array
    meta: jax.Array
    seq: int
    rows_per_batch: int

    @property
    def shape(self):
        return (self.x.shape[0] // self.seq * self.rows_per_batch, self.x.shape[1])


def _stream_inputs(h, tm, tile_of=lambda i: i):
    if not isinstance(h, _Tokens):
        return [h], [pl.BlockSpec((tm, h.shape[1]), lambda i, *_: (tile_of(i), 0))], None
    seq, d = h.seq, h.x.shape[1]
    n_meta = h.meta.shape[0]
    tiles_per_batch = h.rows_per_batch // tm
    n_pad = h.rows_per_batch - n_meta - seq
    assert h.rows_per_batch % tm == 0 and n_pad <= tm and n_meta <= tm
    assert all(n % SUBLANES == 0 for n in (n_meta, n_pad, seq, tm))

    def start(i):
        b, t = i // tiles_per_batch, i % tiles_per_batch
        return pl.multiple_of(b * seq + jnp.clip(t * tm - n_meta, 0, seq - tm), SUBLANES)

    spec = pl.BlockSpec((pl.Element(tm), pl.Element(d)), lambda i, *_: (start(tile_of(i)), 0))
    return [h.x, h.meta], [spec, _resident(h.meta.shape)], (tiles_per_batch, n_pad)


def _stream_tile(stream_refs, layout, tile=None):
    if layout is None:
        return stream_refs[0][...]
    x_ref, meta_ref = stream_refs
    tiles_per_batch, n_pad = layout
    blk = x_ref[...]
    tm, d = blk.shape
    t = lax.rem(pl.program_id(0) if tile is None else tile, tiles_per_batch)
    first = jnp.concatenate([meta_ref[...], blk[:tm - meta_ref.shape[0]]], axis=0)
    last = jnp.concatenate([blk[n_pad:], jnp.zeros((n_pad, d), blk.dtype)], axis=0)
    return jnp.where(t == 0, first, jnp.where(t == tiles_per_batch - 1, last, blk))


def _cast_specs(w, layer, axis, block_size, step_of, n_steps):
    k, n = w.shape[1:]
    n_blocks = w.shape[axis] // block_size
    assert w.shape[axis] % block_size == 0 and n_blocks <= n_steps
    block = (block_size, n) if axis == 1 else (k, block_size)

    def index(*grid_ids):
        j = jnp.minimum(step_of(*grid_ids), n_blocks - 1)
        return (j, 0) if axis == 1 else (0, j)

    return (pl.BlockSpec((None,) + block, lambda *g: (layer,) + index(*g)), pl.BlockSpec(block, index),
            jax.ShapeDtypeStruct((k, n), BF16))


def _cast_blocks(cast_src, cast_dst):
    for src, dst in zip(cast_src, cast_dst):
        dst[...] = src[...].astype(BF16)


def _attn_kernel(*refs, layer, n_pairs, n_casts):
    q_ref, k_ref, v_ref, tri_ref, g_ref = refs[:5]
    g_ref = _layer_row(g_ref, layer, 0, n_pairs * LANES)
    o_ref = refs[5 + n_casts]
    q2_ref, acc_ref, c_ref = refs[6 + 2 * n_casts:]
    _cast_blocks(refs[5:5 + n_casts], refs[6 + n_casts:6 + 2 * n_casts])
    t = SEQ_TILE
    qi = pl.program_id(1)
    pairs = range(n_pairs)
    lanes = [slice(p * LANES, (p + 1) * LANES) for p in pairs]
    first_head = lax.broadcasted_iota(jnp.int32, (t, LANES), 1) < LANES // 2
    row = lax.broadcasted_iota(jnp.int32, (2 * t, t), 0)
    col = lax.broadcasted_iota(jnp.int32, (2 * t, t), 1)
    causal = col < (row & (t - 1))
    zero = jnp.zeros((), BF16)
    for p in pairs:
        qp = q_ref[:, lanes[p]]
        q2_ref[p] = jnp.concatenate([jnp.where(first_head, qp, zero), jnp.where(first_head, zero, qp)], axis=0)

    def head_rows(ref, p, r):
        return ref[p] if r == t else jnp.concatenate([ref[p, :r], ref[p, t:t + r]], axis=0)

    def key_tile(kb, r, diagonal=False):
        start = pl.multiple_of(kb * t, t)
        z = [lax.dot_general(head_rows(q2_ref, p, r), k_ref[pl.ds(start, t), lanes[p]], (((1,), (1,)), ((), ())),
                             preferred_element_type=F32) for p in pairs]
        v = [v_ref[pl.ds(start, t), lanes[p]] for p in pairs]
        c_old = None if diagonal else [head_rows(c_ref, p, r) for p in pairs]
        log_beta, later, c_new = [], [], []
        for p in pairs:
            z_pos = jnp.maximum(z[p], 0.0)
            z_neg = z[p] - z_pos
            log_term = jnp.log(1.0 + jnp.exp2(z_neg - z_pos)) * LOG2_E
            soft = z_pos + log_term
            if diagonal:
                soft = jnp.where(causal, soft, 0.0)
            later.append(jnp.dot(soft.astype(BF16), tri_ref[...], preferred_element_type=F32))
            log_beta.append(z_neg - log_term)
            c = jnp.sum(soft, axis=1, keepdims=True)
            c_new.append(jnp.broadcast_to(c, soft.shape) if diagonal else c_old[p] + c)
        o = []
        for p in pairs:
            x = log_beta[p] - later[p]
            if not diagonal:
                x = x - c_old[p]
            w = jnp.exp2(x)
            if diagonal:
                w = jnp.where(causal, w, 0.0)
            o.append(jnp.dot(w.astype(BF16), v[p], preferred_element_type=F32))
        c_min = None
        for p in pairs:
            first = lax.broadcasted_iota(jnp.int32, (r, LANES), 1) < LANES // 2
            o_pair = jnp.where(first, o[p][:r], o[p][r:])
            if diagonal:
                acc_ref[:, lanes[p]] = o_pair
            else:
                acc_ref[:r, lanes[p]] += o_pair
            if r == t:
                c_ref[p] = c_new[p]
            else:
                c_ref[p, :r] = c_new[p][:r]
                c_ref[p, t:t + r] = c_new[p][r:]
            c_min = c_new[p] if c_min is None else jnp.minimum(c_min, c_new[p])
        if r < t:
            return jnp.min(c_min)
        top = jnp.minimum(jnp.min(c_min[:TOP_ROWS]), jnp.min(c_min[t:t + TOP_ROWS]))
        rest = jnp.minimum(jnp.min(c_min[TOP_ROWS:t]), jnp.min(c_min[t + TOP_ROWS:]))
        return top, rest

    def unfinished(c_min):
        return c_min < -LOG2_WEIGHT_FLOOR

    def full_body(s):
        kb, _, _ = s
        return (kb - 1,) + key_tile(kb, t)

    def top_body(s):
        kb, _ = s
        return kb - 1, key_tile(kb, TOP_ROWS)

    kb, top, _ = lax.while_loop(lambda s: jnp.logical_and(s[0] >= 0, unfinished(s[2])), full_body,
                                (qi - 1,) + key_tile(qi, t, diagonal=True))
    lax.while_loop(lambda s: jnp.logical_and(s[0] >= 0, unfinished(s[1])), top_body, (kb, top))
    o_ref[...] = _rms(acc_ref[...], g_ref[...]).astype(BF16)


def _attention(layer, q, k, v, g, batch, rows_per_batch, casts):
    rows, d_attn = q.shape
    t = SEQ_TILE
    nq = rows_per_batch // t
    j = jnp.arange(t)
    tri = (j[:, None] > j[None, :]).astype(BF16)
    n_pairs = d_attn // LANES
    kv_spec = pl.BlockSpec((rows_per_batch, d_attn), lambda b, i: (b, 0), pipeline_mode=pl.Buffered(1))
    tile_spec = pl.BlockSpec((t, d_attn), lambda b, i: (b * nq + i, 0))
    cast_specs = [_cast_specs(w, l, axis, ATTN_CAST_BLOCK, lambda b, i: b * nq + i, batch * nq) for w, l, axis in casts]
    return pl.pallas_call(
        functools.partial(_attn_kernel, layer=layer, n_pairs=n_pairs, n_casts=len(casts)),
        grid=(batch, nq),
        in_specs=[tile_spec, kv_spec, kv_spec, _resident(tri.shape), _resident(g.shape)]
        + [c[0] for c in cast_specs],
        out_specs=[tile_spec] + [c[1] for c in cast_specs],
        out_shape=[jax.ShapeDtypeStruct((rows, d_attn), BF16)] + [c[2] for c in cast_specs],
        scratch_shapes=[pltpu.VMEM((n_pairs, 2 * t, LANES), BF16), pltpu.VMEM((t, d_attn), F32),
                        pltpu.VMEM((n_pairs, 2 * t, t), F32)],
        compiler_params=_params(2),
        name="sb_attention",
    )(q, k, v, tri, g, *[w for w, _, _ in casts])


def _digest(v):
    v = v.astype(F32)
    rows, cols = v.shape
    top = jnp.max(v.reshape(rows // SUBLANES, SUBLANES, cols), axis=0)
    return functools.reduce(jnp.maximum, [top[:, c:c + LANES] for c in range(0, cols, LANES)])


def _zero_after(digests):
    bits = pltpu.bitcast(functools.reduce(jnp.maximum, digests), jnp.uint32)
    zero = ((bits >> 16) >> 16).astype(F32)
    return jnp.concatenate([zero, zero], axis=0).astype(BF16)


def _proj_mix_kernel(*refs, layer, n_stream, layout, n_casts, n_tiles, tiles_per_batch, d_attn, d_pool, d_conv,
                     q_scale):
    ns = n_stream
    g_ref, w_ref, w_pool_ref, pool_scale_ref, g_mix_ref, w_dw_ref, b_dw_ref, ln_g_ref, ln_b_ref, w_pw_ref = refs[ns:ns + 10]
    g_ref, pool_scale_ref, b_dw_ref, ln_g_ref, ln_b_ref = (
        _layer_row(r, layer) for r in (g_ref, pool_scale_ref, b_dw_ref, ln_g_ref, ln_b_ref))
    g_pool_ref = _layer_row(g_mix_ref, layer, d_attn, d_attn + d_pool)
    g_conv_ref = _layer_row(g_mix_ref, layer, d_attn + d_pool)
    cast_src = refs[ns + 10:ns + 10 + n_casts]
    q_ref, k_ref, v_ref, o_pool_ref, o_conv_ref = refs[ns + 10 + n_casts:ns + 15 + n_casts]
    cast_dst = refs[ns + 15 + n_casts:ns + 15 + 2 * n_casts]
    pool_ext, pool_sums, conv_ext, y_ref = refs[ns + 15 + 2 * n_casts:]
    i = pl.program_id(0)
    tile = jnp.minimum(i, n_tiles - 1)
    tm = q_ref.shape[0]

    @pl.when(i == 0)
    def _():
        pool_ext[...] = jnp.zeros_like(pool_ext)
        conv_ext[0] = jnp.zeros(conv_ext.shape[1:], F32)

    def mix_pool():
        n_pool = POOL_HALO + tm
        level, window_sum = pool_ext, []
        for s in range(len(POOL_WINDOWS)):
            start, shift, lo = SUBLANES * (s + 1), 2 ** s, s * LANES
            both = level[start:n_pool, lo:] + level[start - shift:n_pool - shift, lo:]
            window_sum.append(both[POOL_HALO - start:, :LANES])
            if s + 1 < len(POOL_WINDOWS):
                pool_sums[s, start:n_pool, lo:] = both
                level = pool_sums.at[s]
        ti = lax.rem(i + tiles_per_batch - 1, tiles_per_batch)
        pos = ti * tm + lax.broadcasted_iota(jnp.int32, (tm, 1), 0)
        mixed = []
        sq = jnp.zeros((tm, 1), F32)
        for gi, window in enumerate(POOL_WINDOWS):
            lanes = slice(gi * LANES, (gi + 1) * LANES)
            count = jnp.minimum(pos + 1, window).astype(F32)
            pooled = window_sum[gi] / count - pool_ext[POOL_HALO:, lanes]
            m = jnp.dot(pooled.astype(BF16), w_pool_ref[gi].astype(BF16), preferred_element_type=F32)
            m = m * pool_scale_ref[:, lanes]
            sq = sq + jnp.sum(m * m, axis=-1, keepdims=True)
            mixed.append(m)
        inv = lax.rsqrt(sq / d_pool + EPS)
        out = [(m * inv * g_pool_ref[:, gi * LANES:(gi + 1) * LANES]).astype(BF16) for gi, m in enumerate(mixed)]
        for gi, o in enumerate(out):
            o_pool_ref[:, gi * LANES:(gi + 1) * LANES] = o
        return [_digest(o) for o in out]

    n_ext = CONV_HALO + tm

    def conv_copies():
        digests = []
        for r in range(1, SUBLANES):
            shifted = conv_ext[0, r:r + n_ext - SUBLANES, :]
            conv_ext[r, :n_ext - SUBLANES, :] = shifted
            digests.append(_digest(shifted))
        return digests

    first_tap = CONV_HALO - (CONV_WIDTH - 1)

    def conv_rows(base):
        acc = jnp.broadcast_to(b_dw_ref[...], (CONV_CHUNK, d_conv))
        for tap in range(CONV_WIDTH):
            shift = (first_tap + tap) % SUBLANES
            lo = base + first_tap + tap - shift
            acc = acc + w_dw_ref[tap:tap + 1, :] * conv_ext[shift, lo:lo + CONV_CHUNK, :]
        y_ref[base:base + CONV_CHUNK, :] = acc
        return _digest(acc)

    def conv_finish():
        y = y_ref[...]
        mu = jnp.mean(y, axis=-1, keepdims=True)
        yc = y - mu
        var = jnp.mean(yc * yc, axis=-1, keepdims=True)
        y = yc * lax.rsqrt(var + EPS) * ln_g_ref[...] + ln_b_ref[...]
        y = y * jax.nn.sigmoid(y)
        o = jnp.dot(y.astype(BF16), w_pw_ref[...].astype(BF16), preferred_element_type=F32)
        o = _rms(o, g_conv_ref[...]).astype(BF16)
        o_conv_ref[...] = o
        return [_digest(o)]

    chunks = list(range(0, tm, CONV_CHUNK))
    per_group = -(-len(chunks) // 4)
    conv_group = lambda g: [conv_rows(base) for base in chunks[g * per_group:(g + 1) * per_group]]
    pieces = [mix_pool, conv_copies] + [functools.partial(conv_group, g) for g in range(4)] + [conv_finish]

    def project(do_mix):
        u = _rms(_stream_tile(refs[:ns], layout, tile), g_ref[...]).astype(BF16)

        def proj(lo, hi, wait_for=()):
            lhs = u
            if wait_for:
                head = jnp.concatenate([u[:16, :LANES] + _zero_after(wait_for), u[:16, LANES:]], axis=1)
                lhs = jnp.concatenate([head, u[16:]], axis=0)
            return jnp.dot(lhs, w_ref[:, lo:hi], preferred_element_type=F32)

        mix = (lambda k: pieces[k]()) if do_mix else (lambda k: [])
        half = d_attn // 2
        q_ref[:, :half] = (proj(0, half) * q_scale).astype(BF16)
        q_ref[:, half:] = (proj(half, d_attn) * q_scale).astype(BF16)
        done = mix(0)
        k_ref[:, :half] = proj(d_attn, d_attn + half, done).astype(BF16)
        done = mix(1)
        k_ref[:, half:] = proj(d_attn + half, 2 * d_attn, done).astype(BF16)
        done = mix(2)
        v_ref[:, :half] = proj(2 * d_attn, 2 * d_attn + half, done).astype(BF16)
        done = mix(3)
        v_ref[:, half:] = proj(2 * d_attn + half, 3 * d_attn, done).astype(BF16)
        done = mix(4)
        o = 3 * d_attn
        new_pool = proj(o, o + d_pool, done)
        done = mix(5)
        o += d_pool
        conv_value = proj(o, o + d_conv, done)
        done = mix(6)
        new_conv = conv_value * jax.nn.sigmoid(proj(o + d_conv, o + 2 * d_conv, done))
        _cast_blocks(cast_src, cast_dst)

        starts_batch = lax.rem(tile, tiles_per_batch) == 0
        pool_ext[:POOL_HALO, :] = jnp.where(starts_batch, 0.0, pool_ext[tm:, :])
        pool_ext[POOL_HALO:, :] = new_pool
        conv_ext[0, :CONV_HALO, :] = jnp.where(starts_batch, 0.0, conv_ext[0, tm:, :])
        conv_ext[0, CONV_HALO:, :] = new_conv

    def mix_only():
        for piece in pieces:
            piece()

    pl.when(i == 0)(functools.partial(project, False))
    pl.when(jnp.logical_and(i > 0, i < n_tiles))(functools.partial(project, True))
    pl.when(i == n_tiles)(mix_only)


def _proj_mix(layer, h, g, w, w_pool, pool_scale, g_mix, w_dw, b_dw, ln_g, ln_b, w_pw, d_attn, rows_per_batch, casts):
    rows, d = h.shape
    d_pool, d_conv = pool_scale.shape[-1], w_pw.shape[-1]
    tm = PROJ_ROWS
    n = rows // tm
    head_dim = d_attn // N_HEADS_SB
    tile_of = lambda i: jnp.minimum(i, n - 1)
    cur = lambda width: pl.BlockSpec((tm, width), lambda i: (tile_of(i), 0))
    prev = lambda width: pl.BlockSpec((tm, width), lambda i: (jnp.maximum(i - 1, 0), 0))
    lr = functools.partial(_layer_resident, layer)
    stream, stream_specs, layout = _stream_inputs(h, tm, tile_of)
    cast_specs = [_cast_specs(w_, l, axis, PROJ_CAST_BLOCK, lambda i: i, n + 1) for w_, l, axis in casts]
    return pl.pallas_call(
        functools.partial(_proj_mix_kernel, layer=layer, n_stream=len(stream), layout=layout, n_casts=len(casts), n_tiles=n,
                          tiles_per_batch=rows_per_batch // tm, d_attn=d_attn, d_pool=d_pool, d_conv=d_conv,
                          q_scale=LOG2_E * head_dim ** -0.5),
        grid=(n + 1,),
        in_specs=stream_specs + [_resident(g.shape), _resident(w.shape), lr(w_pool.shape[1:]),
                                 _resident(pool_scale.shape), _resident(g_mix.shape), lr(w_dw.shape[1:]),
                                 _resident(b_dw.shape), _resident(ln_g.shape), _resident(ln_b.shape),
                                 lr(w_pw.shape[1:])] + [c[0] for c in cast_specs],
        out_specs=[cur(d_attn), cur(d_attn), cur(d_attn), prev(d_pool), prev(d_conv)] + [c[1] for c in cast_specs],
        out_shape=[jax.ShapeDtypeStruct((rows, d_attn), BF16)] * 3
        + [jax.ShapeDtypeStruct((rows, d_pool), BF16), jax.ShapeDtypeStruct((rows, d_conv), BF16)]
        + [c[2] for c in cast_specs],
        scratch_shapes=[pltpu.VMEM((POOL_HALO + tm, d_pool), F32),
                        pltpu.VMEM((len(POOL_WINDOWS) - 1, POOL_HALO + tm, d_pool), F32),
                        pltpu.VMEM((SUBLANES, CONV_HALO + tm, d_conv), F32), pltpu.VMEM((tm, d_conv), F32)],
        compiler_params=_params(1),
        name="in_proj_mixers",
    )(*stream, g, w, w_pool, pool_scale, g_mix, w_dw, b_dw, ln_g, ln_b, w_pw, *[w_ for w_, _, _ in casts])


def _out_proj_kernel(*refs, layer, n_stream, layout, n_casts):
    attn_ref, pool_ref, conv_ref, w_ref, g_ref = refs[n_stream:n_stream + 5]
    g_ref = _layer_row(g_ref, layer)
    o_ref = refs[n_stream + 5 + n_casts]
    _cast_blocks(refs[n_stream + 5:n_stream + 5 + n_casts], refs[n_stream + 6 + n_casts:])
    merged = jnp.concatenate([attn_ref[...], pool_ref[...], conv_ref[...]], axis=1)
    y = jnp.dot(merged, w_ref[...], preferred_element_type=F32)
    o_ref[...] = _stream_tile(refs[:n_stream], layout) + _rms(y, g_ref[...])


def _out_proj(layer, attn, pool, conv, h, w, g, casts):
    rows, d = h.shape
    tm = PROJ_ROWS
    row_spec = lambda n: pl.BlockSpec((tm, n), lambda i: (i, 0))
    stream, stream_specs, layout = _stream_inputs(h, tm)
    cast_specs = [_cast_specs(w_, l, axis, PROJ_CAST_BLOCK, lambda i: i, rows // tm) for w_, l, axis in casts]
    return pl.pallas_call(
        functools.partial(_out_proj_kernel, layer=layer, n_stream=len(stream), layout=layout, n_casts=len(casts)),
        grid=(rows // tm,),
        in_specs=stream_specs + [row_spec(attn.shape[1]), row_spec(pool.shape[1]), row_spec(conv.shape[1]),
                                 _resident(w.shape), _resident(g.shape)] + [c[0] for c in cast_specs],
        out_specs=[row_spec(d)] + [c[1] for c in cast_specs],
        out_shape=[jax.ShapeDtypeStruct((rows, d), F32)] + [c[2] for c in cast_specs],
        compiler_params=_params(1),
        name="out_proj",
    )(*stream, attn, pool, conv, w, g, *[w_ for w_, _, _ in casts])


def _ffn_kernel(h_ref, g_pre_ref, w_gate_ref, w_up_ref, w_down_ref, g_post_ref, o_ref, u_ref, acc_ref, *, layer):
    g_pre_ref, g_post_ref = _layer_row(g_pre_ref, layer), _layer_row(g_post_ref, layer)
    c = pl.program_id(1)

    @pl.when(c == 0)
    def _():
        u_ref[...] = _rms(h_ref[...], g_pre_ref[...]).astype(BF16)
        acc_ref[...] = jnp.zeros_like(acc_ref)

    u = u_ref[...]
    gate = jnp.dot(u, w_gate_ref[...], preferred_element_type=F32)
    up = jnp.dot(u, w_up_ref[...], preferred_element_type=F32)
    a = (gate * jax.nn.sigmoid(gate) * up).astype(BF16)
    tf = w_down_ref.shape[0]
    for n in range(0, acc_ref.shape[1], tf):
        acc_ref[:, n:n + tf] += jnp.dot(a, w_down_ref[:, n:n + tf], preferred_element_type=F32)

    @pl.when(c == pl.num_programs(1) - 1)
    def _():
        o_ref[...] = h_ref[...] + _rms(acc_ref[...], g_post_ref[...])


def _ffn(layer, h, g_pre, w_gate, w_up, w_down, g_post, keep=None):
    rows, d = h.shape
    d_ff = w_gate.shape[1]
    tf = FFN_COLS
    if keep is None:
        tm, n_tiles = FFN_ROWS, rows // FFN_ROWS
        h_spec = pl.BlockSpec((tm, d), lambda i, c: (i, 0))
    else:
        rows_per_batch, first, count = keep
        tm = LAST_FFN_ROWS
        per_batch = count // tm
        n_tiles = (rows // rows_per_batch) * per_batch
        h_spec = pl.BlockSpec((pl.Element(tm), pl.Element(d)),
                              lambda i, c: (pl.multiple_of(
                                  (i // per_batch) * rows_per_batch + first + (i % per_batch) * tm, SUBLANES), 0))
        assert rows_per_batch % SUBLANES == 0 and first % SUBLANES == 0
    return pl.pallas_call(
        functools.partial(_ffn_kernel, layer=layer),
        grid=(n_tiles, d_ff // tf),
        in_specs=[h_spec, _resident(g_pre.shape),
                  pl.BlockSpec((d, tf), lambda i, c: (0, c)), pl.BlockSpec((d, tf), lambda i, c: (0, c)),
                  pl.BlockSpec((tf, d), lambda i, c: (c, 0)), _resident(g_post.shape)],
        out_specs=pl.BlockSpec((tm, d), lambda i, c: (i, 0)),
        out_shape=jax.ShapeDtypeStruct((n_tiles * tm, d), F32),
        scratch_shapes=[pltpu.VMEM((tm, d), BF16), pltpu.VMEM((tm, d), F32)],
        compiler_params=_params(2),
        name="swiglu_ffn",
    )(h, g_pre, w_gate, w_up, w_down, g_post)


def kernel(x, meta_tokens, pre_mix_g, w_in, w_pool, pool_scale, w_dw, b_dw, conv_ln_g, conv_ln_b, w_pw, mix_out_g,
           w_out, post_mix_g, pre_ffn_g, w_gate, w_up, w_down, post_ffn_g):
    batch, seq, d = x.shape
    depth = w_in.shape[0]
    d_pool = pool_scale.shape[1]
    d_conv = w_pw.shape[1]
    d_attn = w_out.shape[1] - d_pool - d_conv
    assert d_pool == LANES * len(POOL_WINDOWS) and d_attn % (2 * LANES) == 0
    assert POOL_WINDOWS == tuple(2 ** (g + 1) for g in range(len(POOL_WINDOWS)))
    assert d_attn // N_HEADS_SB == LANES // 2

    length = N_META + seq
    rows_per_batch = -(-length // ROW_ALIGN) * ROW_ALIGN
    assert rows_per_batch % SEQ_TILE == 0 and (batch * rows_per_batch) % FFN_ROWS == 0 and seq % LAST_FFN_ROWS == 0
    assert meta_tokens.shape[0] == N_META
    h = _Tokens(x.reshape(batch * seq, d), meta_tokens.astype(x.dtype), seq, rows_per_batch)

    w_in_l = w_in[0].astype(BF16)

    for i in range(depth):
        q, k, v, o_pool, o_conv, w_gate_l, w_up_l = _proj_mix(
            i, h, pre_mix_g, w_in_l, w_pool, pool_scale, mix_out_g, w_dw, b_dw, conv_ln_g, conv_ln_b, w_pw, d_attn,
            rows_per_batch, [(w_gate, i, 2), (w_up, i, 2)])
        o_attn, w_out_l, *w_in_next = _attention(i, q, k, v, mix_out_g, batch, rows_per_batch,
                                                 [(w_out, i, 2)] + ([(w_in, i + 1, 2)] if i + 1 < depth else []))
        if w_in_next:
            w_in_l = w_in_next[0]
        h, w_down_l = _out_proj(i, o_attn, o_pool, o_conv, h, w_out_l, post_mix_g, [(w_down, i, 1)])
        keep = (rows_per_batch, N_META, seq) if i == depth - 1 else None
        h = _ffn(i, h, pre_ffn_g, w_gate_l, w_up_l, w_down_l, post_ffn_g, keep)

    return h.reshape(batch, seq, d)
```

```python
import functools
from typing import NamedTuple

import jax
import jax.numpy as jnp
from jax import lax
from jax.experimental import pallas as pl
from jax.experimental.pallas import tpu as pltpu

N_META = 16
N_HEADS_SB = 16
POOL_WINDOWS = (2, 4, 8, 16)
CONV_WIDTH = 31
EPS = 1e-6

LANES = 128
SUBLANES = 8
VMEM_BYTES_V7X = 64 * 1024 * 1024
VMEM_LIMIT = VMEM_BYTES_V7X - 6 * 1024 * 1024

SEQ_TILE = 128
TOP_ROWS = 32
PROJ_ROWS = 384
ROW_ALIGN = PROJ_ROWS
FFN_ROWS = 768
LAST_FFN_ROWS = 512
FFN_COLS = 512
CONV_HALO = 32
POOL_HALO = SUBLANES * len(POOL_WINDOWS)
CONV_CHUNK = 32
ATTN_CAST_BLOCK = 128
PROJ_CAST_BLOCK = 256

LOG2_E = 1.4426950408889634
LOG2_WEIGHT_FLOOR = -127.0

F32 = jnp.float32
BF16 = jnp.bfloat16


def _rms(x, g):
    return x * lax.rsqrt(jnp.mean(x * x, axis=-1, keepdims=True) + EPS) * g


def _resident(shape):
    return pl.BlockSpec(shape, lambda *_: (0,) * len(shape), pipeline_mode=pl.Buffered(1))


def _layer_resident(layer, shape):
    return pl.BlockSpec((None,) + tuple(shape), lambda *_: (layer,) + (0,) * len(shape),
                        pipeline_mode=pl.Buffered(1))


def _layer_row(ref, layer, lo=0, hi=None):
    return ref.at[layer:layer + 1, lo:ref.shape[1] if hi is None else hi]


def _params(n_grid_axes):
    return pltpu.CompilerParams(dimension_semantics=("arbitrary",) * n_grid_axes, vmem_limit_bytes=VMEM_LIMIT)


class _Tokens(NamedTuple):
    x: jax.Array
    meta: jax.Array
    seq: int
    rows_per_batch: int

    @property
    def shape(self):
        return (self.x.shape[0] // self.seq * self.rows_per_batch, self.x.shape[1])


def _stream_inputs(h, tm, tile_of=lambda i: i):
    if not isinstance(h, _Tokens):
        return [h], [pl.BlockSpec((tm, h.shape[1]), lambda i, *_: (tile_of(i), 0))], None
    seq, d = h.seq, h.x.shape[1]
    n_meta = h.meta.shape[0]
    tiles_per_batch = h.rows_per_batch // tm
    n_pad = h.rows_per_batch - n_meta - seq
    assert h.rows_per_batch % tm == 0 and n_pad <= tm and n_meta <= tm
    assert all(n % SUBLANES == 0 for n in (n_meta, n_pad, seq, tm))

    def start(i):
        b, t = i // tiles_per_batch, i % tiles_per_batch
        return pl.multiple_of(b * seq + jnp.clip(t * tm - n_meta, 0, seq - tm), SUBLANES)

    spec = pl.BlockSpec((pl.Element(tm), pl.Element(d)), lambda i, *_: (start(tile_of(i)), 0))
    return [h.x, h.meta], [spec, _resident(h.meta.shape)], (tiles_per_batch, n_pad)


def _stream_tile(stream_refs, layout, tile=None):
    if layout is None:
        return stream_refs[0][...]
    x_ref, meta_ref = stream_refs
    tiles_per_batch, n_pad = layout
    blk = x_ref[...]
    tm, d = blk.shape
    t = lax.rem(pl.program_id(0) if tile is None else tile, tiles_per_batch)
    first = jnp.concatenate([meta_ref[...], blk[:tm - meta_ref.shape[0]]], axis=0)
    last = jnp.concatenate([blk[n_pad:], jnp.zeros((n_pad, d), blk.dtype)], axis=0)
    return jnp.where(t == 0, first, jnp.where(t == tiles_per_batch - 1, last, blk))


def _cast_specs(w, layer, axis, block_size, step_of, n_steps):
    k, n = w.shape[1:]
    n_blocks = w.shape[axis] // block_size
    assert w.shape[axis] % block_size == 0 and n_blocks <= n_steps
    block = (block_size, n) if axis == 1 else (k, block_size)

    def index(*grid_ids):
        j = jnp.minimum(step_of(*grid_ids), n_blocks - 1)
        return (j, 0) if axis == 1 else (0, j)

    return (pl.BlockSpec((None,) + block, lambda *g: (layer,) + index(*g)), pl.BlockSpec(block, index),
            jax.ShapeDtypeStruct((k, n), BF16))


def _cast_blocks(cast_src, cast_dst):
    for src, dst in zip(cast_src, cast_dst):
        dst[...] = src[...].astype(BF16)


def _attn_kernel(*refs, layer, n_pairs, n_casts):
    q_ref, k_ref, v_ref, tri_ref, g_ref = refs[:5]
    g_ref = _layer_row(g_ref, layer, 0, n_pairs * LANES)
    o_ref = refs[5 + n_casts]
    q2_ref, acc_ref, c_ref = refs[6 + 2 * n_casts:]
    _cast_blocks(refs[5:5 + n_casts], refs[6 + n_casts:6 + 2 * n_casts])
    t = SEQ_TILE
    qi = pl.program_id(1)
    pairs = range(n_pairs)
    lanes = [slice(p * LANES, (p + 1) * LANES) for p in pairs]
    first_head = lax.broadcasted_iota(jnp.int32, (t, LANES), 1) < LANES // 2
    row = lax.broadcasted_iota(jnp.int32, (2 * t, t), 0)
    col = lax.broadcasted_iota(jnp.int32, (2 * t, t), 1)
    causal = col < (row & (t - 1))
    zero = jnp.zeros((), BF16)
    for p in pairs:
        qp = q_ref[:, lanes[p]]
        q2_ref[p] = jnp.concatenate([jnp.where(first_head, qp, zero), jnp.where(first_head, zero, qp)], axis=0)

    def head_rows(ref, p, r):
        return ref[p] if r == t else jnp.concatenate([ref[p, :r], ref[p, t:t + r]], axis=0)

    def key_tile(kb, r, diagonal=False):
        start = pl.multiple_of(kb * t, t)
        z = [lax.dot_general(head_rows(q2_ref, p, r), k_ref[pl.ds(start, t), lanes[p]], (((1,), (1,)), ((), ())),
                             preferred_element_type=F32) for p in pairs]
        v = [v_ref[pl.ds(start, t), lanes[p]] for p in pairs]
        c_old = None if diagonal else [head_rows(c_ref, p, r) for p in pairs]
        log_beta, later, c_new = [], [], []
        for p in pairs:
            z_pos = jnp.maximum(z[p], 0.0)
            z_neg = z[p] - z_pos
            log_term = jnp.log(1.0 + jnp.exp2(z_neg - z_pos)) * LOG2_E
            soft = z_pos + log_term
            if diagonal:
                soft = jnp.where(causal, soft, 0.0)
            later.append(jnp.dot(soft.astype(BF16), tri_ref[...], preferred_element_type=F32))
            log_beta.append(z_neg - log_term)
            c = jnp.sum(soft, axis=1, keepdims=True)
            c_new.append(jnp.broadcast_to(c, soft.shape) if diagonal else c_old[p] + c)
        o = []
        for p in pairs:
            x = log_beta[p] - later[p]
            if not diagonal:
                x = x - c_old[p]
            w = jnp.exp2(x)
            if diagonal:
                w = jnp.where(causal, w, 0.0)
            o.append(jnp.dot(w.astype(BF16), v[p], preferred_element_type=F32))
        c_min = None
        for p in pairs:
            first = lax.broadcasted_iota(jnp.int32, (r, LANES), 1) < LANES // 2
            o_pair = jnp.where(first, o[p][:r], o[p][r:])
            if diagonal:
                acc_ref[:, lanes[p]] = o_pair
            else:
                acc_ref[:r, lanes[p]] += o_pair
            if r == t:
                c_ref[p] = c_new[p]
            else:
                c_ref[p, :r] = c_new[p][:r]
                c_ref[p, t:t + r] = c_new[p][r:]
            c_min = c_new[p] if c_min is None else jnp.minimum(c_min, c_new[p])
        if r < t:
            return jnp.min(c_min)
        top = jnp.minimum(jnp.min(c_min[:TOP_ROWS]), jnp.min(c_min[t:t + TOP_ROWS]))
        rest = jnp.minimum(jnp.min(c_min[TOP_ROWS:t]), jnp.min(c_min[t + TOP_ROWS:]))
        return top, rest

    def unfinished(c_min):
        return c_min < -LOG2_WEIGHT_FLOOR

    def full_body(s):
        kb, _, _ = s
        return (kb - 1,) + key_tile(kb, t)

    def top_body(s):
        kb, _ = s
        return kb - 1, key_tile(kb, TOP_ROWS)

    kb, top, _ = lax.while_loop(lambda s: jnp.logical_and(s[0] >= 0, unfinished(s[2])), full_body,
                                (qi - 1,) + key_tile(qi, t, diagonal=True))
    lax.while_loop(lambda s: jnp.logical_and(s[0] >= 0, unfinished(s[1])), top_body, (kb, top))
    o_ref[...] = _rms(acc_ref[...], g_ref[...]).astype(BF16)


def _attention(layer, q, k, v, g, batch, rows_per_batch, casts):
    rows, d_attn = q.shape
    t = SEQ_TILE
    nq = rows_per_batch // t
    j = jnp.arange(t)
    tri = (j[:, None] > j[None, :]).astype(BF16)
    n_pairs = d_attn // LANES
    kv_spec = pl.BlockSpec((rows_per_batch, d_attn), lambda b, i: (b, 0), pipeline_mode=pl.Buffered(1))
    tile_spec = pl.BlockSpec((t, d_attn), lambda b, i: (b * nq + i, 0))
    cast_specs = [_cast_specs(w, l, axis, ATTN_CAST_BLOCK, lambda b, i: b * nq + i, batch * nq) for w, l, axis in casts]
    return pl.pallas_call(
        functools.partial(_attn_kernel, layer=layer, n_pairs=n_pairs, n_casts=len(casts)),
        grid=(batch, nq),
        in_specs=[tile_spec, kv_spec, kv_spec, _resident(tri.shape), _resident(g.shape)]
        + [c[0] for c in cast_specs],
        out_specs=[tile_spec] + [c[1] for c in cast_specs],
        out_shape=[jax.ShapeDtypeStruct((rows, d_attn), BF16)] + [c[2] for c in cast_specs],
        scratch_shapes=[pltpu.VMEM((n_pairs, 2 * t, LANES), BF16), pltpu.VMEM((t, d_attn), F32),
                        pltpu.VMEM((n_pairs, 2 * t, t), F32)],
        compiler_params=_params(2),
        name="sb_attention",
    )(q, k, v, tri, g, *[w for w, _, _ in casts])


def _digest(v):
    v = v.astype(F32)
    rows, cols = v.shape
    top = jnp.max(v.reshape(rows // SUBLANES, SUBLANES, cols), axis=0)
    return functools.reduce(jnp.maximum, [top[:, c:c + LANES] for c in range(0, cols, LANES)])


def _zero_after(digests):
    bits = pltpu.bitcast(functools.reduce(jnp.maximum, digests), jnp.uint32)
    zero = ((bits >> 16) >> 16).astype(F32)
    return jnp.concatenate([zero, zero], axis=0).astype(BF16)


def _proj_mix_kernel(*refs, layer, n_stream, layout, n_casts, n_tiles, tiles_per_batch, d_attn, d_pool, d_conv,
                     q_scale):
    ns = n_stream
    g_ref, w_ref, w_pool_ref, pool_scale_ref, g_mix_ref, w_dw_ref, b_dw_ref, ln_g_ref, ln_b_ref, w_pw_ref = refs[ns:ns + 10]
    g_ref, pool_scale_ref, b_dw_ref, ln_g_ref, ln_b_ref = (
        _layer_row(r, layer) for r in (g_ref, pool_scale_ref, b_dw_ref, ln_g_ref, ln_b_ref))
    g_pool_ref = _layer_row(g_mix_ref, layer, d_attn, d_attn + d_pool)
    g_conv_ref = _layer_row(g_mix_ref, layer, d_attn + d_pool)
    cast_src = refs[ns + 10:ns + 10 + n_casts]
    q_ref, k_ref, v_ref, o_pool_ref, o_conv_ref = refs[ns + 10 + n_casts:ns + 15 + n_casts]
    cast_dst = refs[ns + 15 + n_casts:ns + 15 + 2 * n_casts]
    pool_ext, pool_sums, conv_ext, y_ref = refs[ns + 15 + 2 * n_casts:]
    i = pl.program_id(0)
    tile = jnp.minimum(i, n_tiles - 1)
    tm = q_ref.shape[0]

    @pl.when(i == 0)
    def _():
        pool_ext[...] = jnp.zeros_like(pool_ext)
        conv_ext[0] = jnp.zeros(conv_ext.shape[1:], F32)

    def mix_pool():
        n_pool = POOL_HALO + tm
        level, window_sum = pool_ext, []
        for s in range(len(POOL_WINDOWS)):
            start, shift, lo = SUBLANES * (s + 1), 2 ** s, s * LANES
            both = level[start:n_pool, lo:] + level[start - shift:n_pool - shift, lo:]
            window_sum.append(both[POOL_HALO - start:, :LANES])
            if s + 1 < len(POOL_WINDOWS):
                pool_sums[s, start:n_pool, lo:] = both
                level = pool_sums.at[s]
        ti = lax.rem(i + tiles_per_batch - 1, tiles_per_batch)
        pos = ti * tm + lax.broadcasted_iota(jnp.int32, (tm, 1), 0)
        mixed = []
        sq = jnp.zeros((tm, 1), F32)
        for gi, window in enumerate(POOL_WINDOWS):
            lanes = slice(gi * LANES, (gi + 1) * LANES)
            count = jnp.minimum(pos + 1, window).astype(F32)
            pooled = window_sum[gi] / count - pool_ext[POOL_HALO:, lanes]
            m = jnp.dot(pooled.astype(BF16), w_pool_ref[gi].astype(BF16), preferred_element_type=F32)
            m = m * pool_scale_ref[:, lanes]
            sq = sq + jnp.sum(m * m, axis=-1, keepdims=True)
            mixed.append(m)
        inv = lax.rsqrt(sq / d_pool + EPS)
        out = [(m * inv * g_pool_ref[:, gi * LANES:(gi + 1) * LANES]).astype(BF16) for gi, m in enumerate(mixed)]
        for gi, o in enumerate(out):
            o_pool_ref[:, gi * LANES:(gi + 1) * LANES] = o
        return [_digest(o) for o in out]

    n_ext = CONV_HALO + tm

    def conv_copies():
        digests = []
        for r in range(1, SUBLANES):
            shifted = conv_ext[0, r:r + n_ext - SUBLANES, :]
            conv_ext[r, :n_ext - SUBLANES, :] = shifted
            digests.append(_digest(shifted))
        return digests

    first_tap = CONV_HALO - (CONV_WIDTH - 1)

    def conv_rows(base):
        acc = jnp.broadcast_to(b_dw_ref[...], (CONV_CHUNK, d_conv))
        for tap in range(CONV_WIDTH):
            shift = (first_tap + tap) % SUBLANES
            lo = base + first_tap + tap - shift
            acc = acc + w_dw_ref[tap:tap + 1, :] * conv_ext[shift, lo:lo + CONV_CHUNK, :]
        y_ref[base:base + CONV_CHUNK, :] = acc
        return _digest(acc)

    def conv_finish():
        y = y_ref[...]
        mu = jnp.mean(y, axis=-1, keepdims=True)
        yc = y - mu
        var = jnp.mean(yc * yc, axis=-1, keepdims=True)
        y = yc * lax.rsqrt(var + EPS) * ln_g_ref[...] + ln_b_ref[...]
        y = y * jax.nn.sigmoid(y)
        o = jnp.dot(y.astype(BF16), w_pw_ref[...].astype(BF16), preferred_element_type=F32)
        o = _rms(o, g_conv_ref[...]).astype(BF16)
        o_conv_ref[...] = o
        return [_digest(o)]

    chunks = list(range(0, tm, CONV_CHUNK))
    per_group = -(-len(chunks) // 4)
    conv_group = lambda g: [conv_rows(base) for base in chunks[g * per_group:(g + 1) * per_group]]
    pieces = [mix_pool, conv_copies] + [functools.partial(conv_group, g) for g in range(4)] + [conv_finish]

    def project(do_mix):
        u = _rms(_stream_tile(refs[:ns], layout, tile), g_ref[...]).astype(BF16)

        def proj(lo, hi, wait_for=()):
            lhs = u
            if wait_for:
                head = jnp.concatenate([u[:16, :LANES] + _zero_after(wait_for), u[:16, LANES:]], axis=1)
                lhs = jnp.concatenate([head, u[16:]], axis=0)
            return jnp.dot(lhs, w_ref[:, lo:hi], preferred_element_type=F32)

        mix = (lambda k: pieces[k]()) if do_mix else (lambda k: [])
        q_ref[...] = (proj(0, d_attn) * q_scale).astype(BF16)
        done = mix(0)
        k_ref[...] = proj(d_attn, 2 * d_attn, done).astype(BF16)
        done = mix(1) + mix(2)
        v_ref[...] = proj(2 * d_attn, 3 * d_attn, done).astype(BF16)
        done = mix(3) + mix(4)
        o = 3 * d_attn
        new_pool = proj(o, o + d_pool, done)
        done = mix(5)
        o += d_pool
        conv_value = proj(o, o + d_conv, done)
        done = mix(6)
        new_conv = conv_value * jax.nn.sigmoid(proj(o + d_conv, o + 2 * d_conv, done))
        _cast_blocks(cast_src, cast_dst)

        starts_batch = lax.rem(tile, tiles_per_batch) == 0
        pool_ext[:POOL_HALO, :] = jnp.where(starts_batch, 0.0, pool_ext[tm:, :])
        pool_ext[POOL_HALO:, :] = new_pool
        conv_ext[0, :CONV_HALO, :] = jnp.where(starts_batch, 0.0, conv_ext[0, tm:, :])
        conv_ext[0, CONV_HALO:, :] = new_conv

    def mix_only():
        for piece in pieces:
            piece()

    pl.when(i == 0)(functools.partial(project, False))
    pl.when(jnp.logical_and(i > 0, i < n_tiles))(functools.partial(project, True))
    pl.when(i == n_tiles)(mix_only)


def _proj_mix(layer, h, g, w, w_pool, pool_scale, g_mix, w_dw, b_dw, ln_g, ln_b, w_pw, d_attn, rows_per_batch, casts):
    rows, d = h.shape
    d_pool, d_conv = pool_scale.shape[-1], w_pw.shape[-1]
    tm = PROJ_ROWS
    n = rows // tm
    head_dim = d_attn // N_HEADS_SB
    tile_of = lambda i: jnp.minimum(i, n - 1)
    cur = lambda width: pl.BlockSpec((tm, width), lambda i: (tile_of(i), 0))
    prev = lambda width: pl.BlockSpec((tm, width), lambda i: (jnp.maximum(i - 1, 0), 0))
    lr = functools.partial(_layer_resident, layer)
    stream, stream_specs, layout = _stream_inputs(h, tm, tile_of)
    cast_specs = [_cast_specs(w_, l, axis, PROJ_CAST_BLOCK, lambda i: i, n + 1) for w_, l, axis in casts]
    return pl.pallas_call(
        functools.partial(_proj_mix_kernel, layer=layer, n_stream=len(stream), layout=layout, n_casts=len(casts), n_tiles=n,
                          tiles_per_batch=rows_per_batch // tm, d_attn=d_attn, d_pool=d_pool, d_conv=d_conv,
                          q_scale=LOG2_E * head_dim ** -0.5),
        grid=(n + 1,),
        in_specs=stream_specs + [_resident(g.shape), _resident(w.shape), lr(w_pool.shape[1:]),
                                 _resident(pool_scale.shape), _resident(g_mix.shape), lr(w_dw.shape[1:]),
                                 _resident(b_dw.shape), _resident(ln_g.shape), _resident(ln_b.shape),
                                 lr(w_pw.shape[1:])] + [c[0] for c in cast_specs],
        out_specs=[cur(d_attn), cur(d_attn), cur(d_attn), prev(d_pool), prev(d_conv)] + [c[1] for c in cast_specs],
        out_shape=[jax.ShapeDtypeStruct((rows, d_attn), BF16)] * 3
        + [jax.ShapeDtypeStruct((rows, d_pool), BF16), jax.ShapeDtypeStruct((rows, d_conv), BF16)]
        + [c[2] for c in cast_specs],
        scratch_shapes=[pltpu.VMEM((POOL_HALO + tm, d_pool), F32),
                        pltpu.VMEM((len(POOL_WINDOWS) - 1, POOL_HALO + tm, d_pool), F32),
                        pltpu.VMEM((SUBLANES, CONV_HALO + tm, d_conv), F32), pltpu.VMEM((tm, d_conv), F32)],
        compiler_params=_params(1),
        name="in_proj_mixers",
    )(*stream, g, w, w_pool, pool_scale, g_mix, w_dw, b_dw, ln_g, ln_b, w_pw, *[w_ for w_, _, _ in casts])


def _out_proj_kernel(*refs, layer, n_stream, layout, n_casts):
    attn_ref, pool_ref, conv_ref, w_ref, g_ref = refs[n_stream:n_stream + 5]
    g_ref = _layer_row(g_ref, layer)
    o_ref = refs[n_stream + 5 + n_casts]
    _cast_blocks(refs[n_stream + 5:n_stream + 5 + n_casts], refs[n_stream + 6 + n_casts:])
    merged = jnp.concatenate([attn_ref[...], pool_ref[...], conv_ref[...]], axis=1)
    y = jnp.dot(merged, w_ref[...], preferred_element_type=F32)
    o_ref[...] = _stream_tile(refs[:n_stream], layout) + _rms(y, g_ref[...])


def _out_proj(layer, attn, pool, conv, h, w, g, casts):
    rows, d = h.shape
    tm = PROJ_ROWS
    row_spec = lambda n: pl.BlockSpec((tm, n), lambda i: (i, 0))
    stream, stream_specs, layout = _stream_inputs(h, tm)
    cast_specs = [_cast_specs(w_, l, axis, PROJ_CAST_BLOCK, lambda i: i, rows // tm) for w_, l, axis in casts]
    return pl.pallas_call(
        functools.partial(_out_proj_kernel, layer=layer, n_stream=len(stream), layout=layout, n_casts=len(casts)),
        grid=(rows // tm,),
        in_specs=stream_specs + [row_spec(attn.shape[1]), row_spec(pool.shape[1]), row_spec(conv.shape[1]),
                                 _resident(w.shape), _resident(g.shape)] + [c[0] for c in cast_specs],
        out_specs=[row_spec(d)] + [c[1] for c in cast_specs],
        out_shape=[jax.ShapeDtypeStruct((rows, d), F32)] + [c[2] for c in cast_specs],
        compiler_params=_params(1),
        name="out_proj",
    )(*stream, attn, pool, conv, w, g, *[w_ for w_, _, _ in casts])


def _ffn_kernel(h_ref, g_pre_ref, w_gate_ref, w_up_ref, w_down_ref, g_post_ref, o_ref, u_ref, acc_ref, *, layer):
    g_pre_ref, g_post_ref = _layer_row(g_pre_ref, layer), _layer_row(g_post_ref, layer)
    c = pl.program_id(1)

    @pl.when(c == 0)
    def _():
        u_ref[...] = _rms(h_ref[...], g_pre_ref[...]).astype(BF16)
        acc_ref[...] = jnp.zeros_like(acc_ref)

    u = u_ref[...]
    gate = jnp.dot(u, w_gate_ref[...], preferred_element_type=F32)
    up = jnp.dot(u, w_up_ref[...], preferred_element_type=F32)
    a = (gate * jax.nn.sigmoid(gate) * up).astype(BF16)
    tf = w_down_ref.shape[0]
    for n in range(0, acc_ref.shape[1], tf):
        acc_ref[:, n:n + tf] += jnp.dot(a, w_down_ref[:, n:n + tf], preferred_element_type=F32)

    @pl.when(c == pl.num_programs(1) - 1)
    def _():
        o_ref[...] = h_ref[...] + _rms(acc_ref[...], g_post_ref[...])


def _ffn(layer, h, g_pre, w_gate, w_up, w_down, g_post, keep=None):
    rows, d = h.shape
    d_ff = w_gate.shape[1]
    tf = FFN_COLS
    if keep is None:
        tm, n_tiles = FFN_ROWS, rows // FFN_ROWS
        h_spec = pl.BlockSpec((tm, d), lambda i, c: (i, 0))
    else:
        rows_per_batch, first, count = keep
        tm = LAST_FFN_ROWS
        per_batch = count // tm
        n_tiles = (rows // rows_per_batch) * per_batch
        h_spec = pl.BlockSpec((pl.Element(tm), pl.Element(d)),
                              lambda i, c: (pl.multiple_of(
                                  (i // per_batch) * rows_per_batch + first + (i % per_batch) * tm, SUBLANES), 0))
        assert rows_per_batch % SUBLANES == 0 and first % SUBLANES == 0
    return pl.pallas_call(
        functools.partial(_ffn_kernel, layer=layer),
        grid=(n_tiles, d_ff // tf),
        in_specs=[h_spec, _resident(g_pre.shape),
                  pl.BlockSpec((d, tf), lambda i, c: (0, c)), pl.BlockSpec((d, tf), lambda i, c: (0, c)),
                  pl.BlockSpec((tf, d), lambda i, c: (c, 0)), _resident(g_post.shape)],
        out_specs=pl.BlockSpec((tm, d), lambda i, c: (i, 0)),
        out_shape=jax.ShapeDtypeStruct((n_tiles * tm, d), F32),
        scratch_shapes=[pltpu.VMEM((tm, d), BF16), pltpu.VMEM((tm, d), F32)],
        compiler_params=_params(2),
        name="swiglu_ffn",
    )(h, g_pre, w_gate, w_up, w_down, g_post)


def kernel(x, meta_tokens, pre_mix_g, w_in, w_pool, pool_scale, w_dw, b_dw, conv_ln_g, conv_ln_b, w_pw, mix_out_g,
           w_out, post_mix_g, pre_ffn_g, w_gate, w_up, w_down, post_ffn_g):
    batch, seq, d = x.shape
    depth = w_in.shape[0]
    d_pool = pool_scale.shape[1]
    d_conv = w_pw.shape[1]
    d_attn = w_out.shape[1] - d_pool - d_conv
    assert d_pool == LANES * len(POOL_WINDOWS) and d_attn % (2 * LANES) == 0
    assert POOL_WINDOWS == tuple(2 ** (g + 1) for g in range(len(POOL_WINDOWS)))
    assert d_attn // N_HEADS_SB == LANES // 2

    length = N_META + seq
    rows_per_batch = -(-length // ROW_ALIGN) * ROW_ALIGN
    assert rows_per_batch % SEQ_TILE == 0 and (batch * rows_per_batch) % FFN_ROWS == 0 and seq % LAST_FFN_ROWS == 0
    assert meta_tokens.shape[0] == N_META
    h = _Tokens(x.reshape(batch * seq, d), meta_tokens.astype(x.dtype), seq, rows_per_batch)

    w_in_l = w_in[0].astype(BF16)

    for i in range(depth):
        q, k, v, o_pool, o_conv, w_gate_l, w_up_l = _proj_mix(
            i, h, pre_mix_g, w_in_l, w_pool, pool_scale, mix_out_g, w_dw, b_dw, conv_ln_g, conv_ln_b, w_pw, d_attn,
            rows_per_batch, [(w_gate, i, 2), (w_up, i, 2)])
        o_attn, w_out_l, *w_in_next = _attention(i, q, k, v, mix_out_g, batch, rows_per_batch,
                                                 [(w_out, i, 2)] + ([(w_in, i + 1, 2)] if i + 1 < depth else []))
        if w_in_next:
            w_in_l = w_in_next[0]
        h, w_down_l = _out_proj(i, o_attn, o_pool, o_conv, h, w_out_l, post_mix_g, [(w_down, i, 1)])
        keep = (rows_per_batch, N_META, seq) if i == depth - 1 else None
        h = _ffn(i, h, pre_ffn_g, w_gate_l, w_up_l, w_down_l, post_ffn_g, keep)

    return h.reshape(batch, seq, d)
```

```python
import functools
from typing import NamedTuple

import jax
import jax.numpy as jnp
from jax import lax
from jax.experimental import pallas as pl
from jax.experimental.pallas import tpu as pltpu

N_META = 16
N_HEADS_SB = 16
POOL_WINDOWS = (2, 4, 8, 16)
CONV_WIDTH = 31
EPS = 1e-6

LANES = 128
SUBLANES = 8
VMEM_BYTES_V7X = 64 * 1024 * 1024
VMEM_LIMIT = VMEM_BYTES_V7X - 6 * 1024 * 1024

SEQ_TILE = 128
TOP_ROWS = 32
PROJ_ROWS = 384
ROW_ALIGN = PROJ_ROWS
FFN_ROWS = 768
LAST_FFN_ROWS = 512
FFN_COLS = 512
CONV_HALO = 32
POOL_HALO = SUBLANES * len(POOL_WINDOWS)
CONV_CHUNK = 32
ATTN_CAST_BLOCK = 128
PROJ_CAST_BLOCK = 256

LOG2_E = 1.4426950408889634
LOG2_WEIGHT_FLOOR = -127.0

F32 = jnp.float32
BF16 = jnp.bfloat16


def _rms(x, g):
    return x * lax.rsqrt(jnp.mean(x * x, axis=-1, keepdims=True) + EPS) * g


def _resident(shape):
    return pl.BlockSpec(shape, lambda *_: (0,) * len(shape), pipeline_mode=pl.Buffered(1))


def _layer_resident(layer, shape):
    return pl.BlockSpec((None,) + tuple(shape), lambda *_: (layer,) + (0,) * len(shape),
                        pipeline_mode=pl.Buffered(1))


def _layer_row(ref, layer, lo=0, hi=None):
    return ref.at[layer:layer + 1, lo:ref.shape[1] if hi is None else hi]


def _params(n_grid_axes):
    return pltpu.CompilerParams(dimension_semantics=("arbitrary",) * n_grid_axes, vmem_limit_bytes=VMEM_LIMIT)


class _Tokens(NamedTuple):
    x: jax.Array
    meta: jax.Array
    seq: int
    rows_per_batch: int

    @property
    def shape(self):
        return (self.x.shape[0] // self.seq * self.rows_per_batch, self.x.shape[1])


def _stream_inputs(h, tm, tile_of=lambda i: i):
    if not isinstance(h, _Tokens):
        return [h], [pl.BlockSpec((tm, h.shape[1]), lambda i, *_: (tile_of(i), 0))], None
    seq, d = h.seq, h.x.shape[1]
    n_meta = h.meta.shape[0]
    tiles_per_batch = h.rows_per_batch // tm
    n_pad = h.rows_per_batch - n_meta - seq
    assert h.rows_per_batch % tm == 0 and n_pad <= tm and n_meta <= tm
    assert all(n % SUBLANES == 0 for n in (n_meta, n_pad, seq, tm))

    def start(i):
        b, t = i // tiles_per_batch, i % tiles_per_batch
        return pl.multiple_of(b * seq + jnp.clip(t * tm - n_meta, 0, seq - tm), SUBLANES)

    spec = pl.BlockSpec((pl.Element(tm), pl.Element(d)), lambda i, *_: (start(tile_of(i)), 0))
    return [h.x, h.meta], [spec, _resident(h.meta.shape)], (tiles_per_batch, n_pad)


def _stream_tile(stream_refs, layout, tile=None):
    if layout is None:
        return stream_refs[0][...]
    x_ref, meta_ref = stream_refs
    tiles_per_batch, n_pad = layout
    blk = x_ref[...]
    tm, d = blk.shape
    t = lax.rem(pl.program_id(0) if tile is None else tile, tiles_per_batch)
    first = jnp.concatenate([meta_ref[...], blk[:tm - meta_ref.shape[0]]], axis=0)
    last = jnp.concatenate([blk[n_pad:], jnp.zeros((n_pad, d), blk.dtype)], axis=0)
    return jnp.where(t == 0, first, jnp.where(t == tiles_per_batch - 1, last, blk))


def _cast_specs(w, layer, axis, block_size, step_of, n_steps):
    k, n = w.shape[1:]
    n_blocks = w.shape[axis] // block_size
    assert w.shape[axis] % block_size == 0 and n_blocks <= n_steps
    block = (block_size, n) if axis == 1 else (k, block_size)

    def index(*grid_ids):
        j = jnp.minimum(step_of(*grid_ids), n_blocks - 1)
        return (j, 0) if axis == 1 else (0, j)

    return (pl.BlockSpec((None,) + block, lambda *g: (layer,) + index(*g)), pl.BlockSpec(block, index),
            jax.ShapeDtypeStruct((k, n), BF16))


def _cast_blocks(cast_src, cast_dst):
    for src, dst in zip(cast_src, cast_dst):
        dst[...] = src[...].astype(BF16)


def _attn_kernel(*refs, layer, n_pairs, n_casts):
    q_ref, k_ref, v_ref, tri_ref, g_ref = refs[:5]
    g_ref = _layer_row(g_ref, layer, 0, n_pairs * LANES)
    o_ref = refs[5 + n_casts]
    q2_ref, acc_ref, c_ref = refs[6 + 2 * n_casts:]
    _cast_blocks(refs[5:5 + n_casts], refs[6 + n_casts:6 + 2 * n_casts])
    t = SEQ_TILE
    qi = pl.program_id(1)
    pairs = range(n_pairs)
    lanes = [slice(p * LANES, (p + 1) * LANES) for p in pairs]
    first_head = lax.broadcasted_iota(jnp.int32, (t, LANES), 1) < LANES // 2
    row = lax.broadcasted_iota(jnp.int32, (2 * t, t), 0)
    col = lax.broadcasted_iota(jnp.int32, (2 * t, t), 1)
    causal = col < (row & (t - 1))
    zero = jnp.zeros((), BF16)
    for p in pairs:
        qp = q_ref[:, lanes[p]]
        q2_ref[p] = jnp.concatenate([jnp.where(first_head, qp, zero), jnp.where(first_head, zero, qp)], axis=0)

    def head_rows(ref, p, r):
        return ref[p] if r == t else jnp.concatenate([ref[p, :r], ref[p, t:t + r]], axis=0)

    def key_tiles(kbs, r, diagonal=False):
        tiles = range(len(kbs))
        starts = [pl.multiple_of(kb * t, t) for kb in kbs]
        masked = [diagonal and j == 0 for j in tiles]
        z = [[lax.dot_general(head_rows(q2_ref, p, r), k_ref[pl.ds(starts[j], t), lanes[p]],
                              (((1,), (1,)), ((), ())), preferred_element_type=F32) for p in pairs] for j in tiles]
        v = [[v_ref[pl.ds(starts[j], t), lanes[p]] for p in pairs] for j in tiles]
        c = [None if diagonal else head_rows(c_ref, p, r) for p in pairs]
        log_beta, later, c_before = [], [], []
        for j in tiles:
            c_before.append(list(c))
            log_beta.append([])
            later.append([])
            for p in pairs:
                z_pos = jnp.maximum(z[j][p], 0.0)
                z_neg = z[j][p] - z_pos
                log_term = jnp.log(1.0 + jnp.exp2(z_neg - z_pos)) * LOG2_E
                soft = z_pos + log_term
                if masked[j]:
                    soft = jnp.where(causal, soft, 0.0)
                later[j].append(jnp.dot(soft.astype(BF16), tri_ref[...], preferred_element_type=F32))
                log_beta[j].append(z_neg - log_term)
                row_sum = jnp.sum(soft, axis=1, keepdims=True)
                c[p] = jnp.broadcast_to(row_sum, soft.shape) if c[p] is None else c[p] + row_sum
        o = [None] * n_pairs
        for j in tiles:
            for p in pairs:
                x = log_beta[j][p] - later[j][p]
                if c_before[j][p] is not None:
                    x = x - c_before[j][p]
                w = jnp.exp2(x)
                if masked[j]:
                    w = jnp.where(causal, w, 0.0)
                o_tile = jnp.dot(w.astype(BF16), v[j][p], preferred_element_type=F32)
                o[p] = o_tile if o[p] is None else o[p] + o_tile
        c_min = None
        for p in pairs:
            first = lax.broadcasted_iota(jnp.int32, (r, LANES), 1) < LANES // 2
            o_pair = jnp.where(first, o[p][:r], o[p][r:])
            if diagonal:
                acc_ref[:, lanes[p]] = o_pair
            else:
                acc_ref[:r, lanes[p]] += o_pair
            if r == t:
                c_ref[p] = c[p]
            else:
                c_ref[p, :r] = c[p][:r]
                c_ref[p, t:t + r] = c[p][r:]
            c_min = c[p] if c_min is None else jnp.minimum(c_min, c[p])
        if r < t:
            return jnp.min(c_min)
        top = jnp.minimum(jnp.min(c_min[:TOP_ROWS]), jnp.min(c_min[t:t + TOP_ROWS]))
        rest = jnp.minimum(jnp.min(c_min[TOP_ROWS:t]), jnp.min(c_min[t + TOP_ROWS:]))
        return top, rest

    def unfinished(c_min):
        return c_min < -LOG2_WEIGHT_FLOOR

    def full_body(s):
        kb, _, _ = s
        return (kb - 1,) + key_tiles([kb], t)

    def top_body(s):
        kb, _ = s
        return kb - 1, key_tiles([kb], TOP_ROWS)

    state = lax.cond(qi >= 1, lambda: (qi - 2,) + key_tiles([qi, qi - 1], t, diagonal=True),
                     lambda: (qi - 1,) + key_tiles([qi], t, diagonal=True))
    kb, top, _ = lax.while_loop(lambda s: jnp.logical_and(s[0] >= 0, unfinished(s[2])), full_body, state)
    lax.while_loop(lambda s: jnp.logical_and(s[0] >= 0, unfinished(s[1])), top_body, (kb, top))
    o_ref[...] = _rms(acc_ref[...], g_ref[...]).astype(BF16)


def _attention(layer, q, k, v, g, batch, rows_per_batch, casts):
    rows, d_attn = q.shape
    t = SEQ_TILE
    nq = rows_per_batch // t
    j = jnp.arange(t)
    tri = (j[:, None] > j[None, :]).astype(BF16)
    n_pairs = d_attn // LANES
    kv_spec = pl.BlockSpec((rows_per_batch, d_attn), lambda b, i: (b, 0), pipeline_mode=pl.Buffered(1))
    tile_spec = pl.BlockSpec((t, d_attn), lambda b, i: (b * nq + i, 0))
    cast_specs = [_cast_specs(w, l, axis, ATTN_CAST_BLOCK, lambda b, i: b * nq + i, batch * nq) for w, l, axis in casts]
    return pl.pallas_call(
        functools.partial(_attn_kernel, layer=layer, n_pairs=n_pairs, n_casts=len(casts)),
        grid=(batch, nq),
        in_specs=[tile_spec, kv_spec, kv_spec, _resident(tri.shape), _resident(g.shape)]
        + [c[0] for c in cast_specs],
        out_specs=[tile_spec] + [c[1] for c in cast_specs],
        out_shape=[jax.ShapeDtypeStruct((rows, d_attn), BF16)] + [c[2] for c in cast_specs],
        scratch_shapes=[pltpu.VMEM((n_pairs, 2 * t, LANES), BF16), pltpu.VMEM((t, d_attn), F32),
                        pltpu.VMEM((n_pairs, 2 * t, t), F32)],
        compiler_params=_params(2),
        name="sb_attention",
    )(q, k, v, tri, g, *[w for w, _, _ in casts])


def _digest(v):
    v = v.astype(F32)
    rows, cols = v.shape
    top = jnp.max(v.reshape(rows // SUBLANES, SUBLANES, cols), axis=0)
    return functools.reduce(jnp.maximum, [top[:, c:c + LANES] for c in range(0, cols, LANES)])


def _zero_after(digests):
    bits = pltpu.bitcast(functools.reduce(jnp.maximum, digests), jnp.uint32)
    zero = ((bits >> 16) >> 16).astype(F32)
    return jnp.concatenate([zero, zero], axis=0).astype(BF16)


def _proj_mix_kernel(*refs, layer, n_stream, layout, n_casts, n_tiles, tiles_per_batch, d_attn, d_pool, d_conv,
                     q_scale):
    ns = n_stream
    g_ref, w_ref, w_pool_ref, pool_scale_ref, g_mix_ref, w_dw_ref, b_dw_ref, ln_g_ref, ln_b_ref, w_pw_ref = refs[ns:ns + 10]
    g_ref, pool_scale_ref, b_dw_ref, ln_g_ref, ln_b_ref = (
        _layer_row(r, layer) for r in (g_ref, pool_scale_ref, b_dw_ref, ln_g_ref, ln_b_ref))
    g_pool_ref = _layer_row(g_mix_ref, layer, d_attn, d_attn + d_pool)
    g_conv_ref = _layer_row(g_mix_ref, layer, d_attn + d_pool)
    cast_src = refs[ns + 10:ns + 10 + n_casts]
    q_ref, k_ref, v_ref, o_pool_ref, o_conv_ref = refs[ns + 10 + n_casts:ns + 15 + n_casts]
    cast_dst = refs[ns + 15 + n_casts:ns + 15 + 2 * n_casts]
    pool_ext, pool_sums, conv_ext, y_ref = refs[ns + 15 + 2 * n_casts:]
    i = pl.program_id(0)
    tile = jnp.minimum(i, n_tiles - 1)
    tm = q_ref.shape[0]

    @pl.when(i == 0)
    def _():
        pool_ext[...] = jnp.zeros_like(pool_ext)
        conv_ext[0] = jnp.zeros(conv_ext.shape[1:], F32)

    def mix_pool():
        n_pool = POOL_HALO + tm
        level, window_sum = pool_ext, []
        for s in range(len(POOL_WINDOWS)):
            start, shift, lo = SUBLANES * (s + 1), 2 ** s, s * LANES
            both = level[start:n_pool, lo:] + level[start - shift:n_pool - shift, lo:]
            window_sum.append(both[POOL_HALO - start:, :LANES])
            if s + 1 < len(POOL_WINDOWS):
                pool_sums[s, start:n_pool, lo:] = both
                level = pool_sums.at[s]
        ti = lax.rem(i + tiles_per_batch - 1, tiles_per_batch)
        pos = ti * tm + lax.broadcasted_iota(jnp.int32, (tm, 1), 0)
        mixed = []
        sq = jnp.zeros((tm, 1), F32)
        for gi, window in enumerate(POOL_WINDOWS):
            lanes = slice(gi * LANES, (gi + 1) * LANES)
            count = jnp.minimum(pos + 1, window).astype(F32)
            pooled = window_sum[gi] / count - pool_ext[POOL_HALO:, lanes]
            m = jnp.dot(pooled.astype(BF16), w_pool_ref[gi].astype(BF16), preferred_element_type=F32)
            m = m * pool_scale_ref[:, lanes]
            sq = sq + jnp.sum(m * m, axis=-1, keepdims=True)
            mixed.append(m)
        inv = lax.rsqrt(sq / d_pool + EPS)
        out = [(m * inv * g_pool_ref[:, gi * LANES:(gi + 1) * LANES]).astype(BF16) for gi, m in enumerate(mixed)]
        for gi, o in enumerate(out):
            o_pool_ref[:, gi * LANES:(gi + 1) * LANES] = o
        return [_digest(o) for o in out]

    n_ext = CONV_HALO + tm

    def conv_copies():
        digests = []
        for r in range(1, SUBLANES):
            shifted = conv_ext[0, r:r + n_ext - SUBLANES, :]
            conv_ext[r, :n_ext - SUBLANES, :] = shifted
            digests.append(_digest(shifted))
        return digests

    first_tap = CONV_HALO - (CONV_WIDTH - 1)

    def conv_rows(base):
        acc = jnp.broadcast_to(b_dw_ref[...], (CONV_CHUNK, d_conv))
        for tap in range(CONV_WIDTH):
            shift = (first_tap + tap) % SUBLANES
            lo = base + first_tap + tap - shift
            acc = acc + w_dw_ref[tap:tap + 1, :] * conv_ext[shift, lo:lo + CONV_CHUNK, :]
        y_ref[base:base + CONV_CHUNK, :] = acc
        return _digest(acc)

    def conv_finish():
        y = y_ref[...]
        mu = jnp.mean(y, axis=-1, keepdims=True)
        yc = y - mu
        var = jnp.mean(yc * yc, axis=-1, keepdims=True)
        y = yc * lax.rsqrt(var + EPS) * ln_g_ref[...] + ln_b_ref[...]
        y = y * jax.nn.sigmoid(y)
        o = jnp.dot(y.astype(BF16), w_pw_ref[...].astype(BF16), preferred_element_type=F32)
        o = _rms(o, g_conv_ref[...]).astype(BF16)
        o_conv_ref[...] = o
        return [_digest(o)]

    chunks = list(range(0, tm, CONV_CHUNK))
    per_group = -(-len(chunks) // 4)
    conv_group = lambda g: [conv_rows(base) for base in chunks[g * per_group:(g + 1) * per_group]]
    pieces = [mix_pool, conv_copies] + [functools.partial(conv_group, g) for g in range(4)] + [conv_finish]

    def project(do_mix):
        u = _rms(_stream_tile(refs[:ns], layout, tile), g_ref[...]).astype(BF16)

        def proj(lo, hi, wait_for=()):
            lhs = u
            if wait_for:
                head = jnp.concatenate([u[:16, :LANES] + _zero_after(wait_for), u[:16, LANES:]], axis=1)
                lhs = jnp.concatenate([head, u[16:]], axis=0)
            return jnp.dot(lhs, w_ref[:, lo:hi], preferred_element_type=F32)

        mix = (lambda k: pieces[k]()) if do_mix else (lambda k: [])
        q_ref[...] = (proj(0, d_attn) * q_scale).astype(BF16)
        done = mix(0)
        k_ref[...] = proj(d_attn, 2 * d_attn, done).astype(BF16)
        done = mix(1) + mix(2)
        v_ref[...] = proj(2 * d_attn, 3 * d_attn, done).astype(BF16)
        done = mix(3) + mix(4)
        o = 3 * d_attn
        new_pool = proj(o, o + d_pool, done)
        done = mix(5)
        o += d_pool
        conv_value = proj(o, o + d_conv, done)
        done = mix(6)
        new_conv = conv_value * jax.nn.sigmoid(proj(o + d_conv, o + 2 * d_conv, done))
        _cast_blocks(cast_src, cast_dst)

        starts_batch = lax.rem(tile, tiles_per_batch) == 0
        pool_ext[:POOL_HALO, :] = jnp.where(starts_batch, 0.0, pool_ext[tm:, :])
        pool_ext[POOL_HALO:, :] = new_pool
        conv_ext[0, :CONV_HALO, :] = jnp.where(starts_batch, 0.0, conv_ext[0, tm:, :])
        conv_ext[0, CONV_HALO:, :] = new_conv

    def mix_only():
        for piece in pieces:
            piece()

    pl.when(i == 0)(functools.partial(project, False))
    pl.when(jnp.logical_and(i > 0, i < n_tiles))(functools.partial(project, True))
    pl.when(i == n_tiles)(mix_only)


def _proj_mix(layer, h, g, w, w_pool, pool_scale, g_mix, w_dw, b_dw, ln_g, ln_b, w_pw, d_attn, rows_per_batch, casts):
    rows, d = h.shape
    d_pool, d_conv = pool_scale.shape[-1], w_pw.shape[-1]
    tm = PROJ_ROWS
    n = rows // tm
    head_dim = d_attn // N_HEADS_SB
    tile_of = lambda i: jnp.minimum(i, n - 1)
    cur = lambda width: pl.BlockSpec((tm, width), lambda i: (tile_of(i), 0))
    prev = lambda width: pl.BlockSpec((tm, width), lambda i: (jnp.maximum(i - 1, 0), 0))
    lr = functools.partial(_layer_resident, layer)
    stream, stream_specs, layout = _stream_inputs(h, tm, tile_of)
    cast_specs = [_cast_specs(w_, l, axis, PROJ_CAST_BLOCK, lambda i: i, n + 1) for w_, l, axis in casts]
    return pl.pallas_call(
        functools.partial(_proj_mix_kernel, layer=layer, n_stream=len(stream), layout=layout, n_casts=len(casts), n_tiles=n,
                          tiles_per_batch=rows_per_batch // tm, d_attn=d_attn, d_pool=d_pool, d_conv=d_conv,
                          q_scale=LOG2_E * head_dim ** -0.5),
        grid=(n + 1,),
        in_specs=stream_specs + [_resident(g.shape), _resident(w.shape), lr(w_pool.shape[1:]),
                                 _resident(pool_scale.shape), _resident(g_mix.shape), lr(w_dw.shape[1:]),
                                 _resident(b_dw.shape), _resident(ln_g.shape), _resident(ln_b.shape),
                                 lr(w_pw.shape[1:])] + [c[0] for c in cast_specs],
        out_specs=[cur(d_attn), cur(d_attn), cur(d_attn), prev(d_pool), prev(d_conv)] + [c[1] for c in cast_specs],
        out_shape=[jax.ShapeDtypeStruct((rows, d_attn), BF16)] * 3
        + [jax.ShapeDtypeStruct((rows, d_pool), BF16), jax.ShapeDtypeStruct((rows, d_conv), BF16)]
        + [c[2] for c in cast_specs],
        scratch_shapes=[pltpu.VMEM((POOL_HALO + tm, d_pool), F32),
                        pltpu.VMEM((len(POOL_WINDOWS) - 1, POOL_HALO + tm, d_pool), F32),
                        pltpu.VMEM((SUBLANES, CONV_HALO + tm, d_conv), F32), pltpu.VMEM((tm, d_conv), F32)],
        compiler_params=_params(1),
        name="in_proj_mixers",
    )(*stream, g, w, w_pool, pool_scale, g_mix, w_dw, b_dw, ln_g, ln_b, w_pw, *[w_ for w_, _, _ in casts])


def _out_proj_kernel(*refs, layer, n_stream, layout, n_casts):
    attn_ref, pool_ref, conv_ref, w_ref, g_ref = refs[n_stream:n_stream + 5]
    g_ref = _layer_row(g_ref, layer)
    o_ref = refs[n_stream + 5 + n_casts]
    _cast_blocks(refs[n_stream + 5:n_stream + 5 + n_casts], refs[n_stream + 6 + n_casts:])
    merged = jnp.concatenate([attn_ref[...], pool_ref[...], conv_ref[...]], axis=1)
    y = jnp.dot(merged, w_ref[...], preferred_element_type=F32)
    o_ref[...] = _stream_tile(refs[:n_stream], layout) + _rms(y, g_ref[...])


def _out_proj(layer, attn, pool, conv, h, w, g, casts):
    rows, d = h.shape
    tm = PROJ_ROWS
    row_spec = lambda n: pl.BlockSpec((tm, n), lambda i: (i, 0))
    stream, stream_specs, layout = _stream_inputs(h, tm)
    cast_specs = [_cast_specs(w_, l, axis, PROJ_CAST_BLOCK, lambda i: i, rows // tm) for w_, l, axis in casts]
    return pl.pallas_call(
        functools.partial(_out_proj_kernel, layer=layer, n_stream=len(stream), layout=layout, n_casts=len(casts)),
        grid=(rows // tm,),
        in_specs=stream_specs + [row_spec(attn.shape[1]), row_spec(pool.shape[1]), row_spec(conv.shape[1]),
                                 _resident(w.shape), _resident(g.shape)] + [c[0] for c in cast_specs],
        out_specs=[row_spec(d)] + [c[1] for c in cast_specs],
        out_shape=[jax.ShapeDtypeStruct((rows, d), F32)] + [c[2] for c in cast_specs],
        compiler_params=_params(1),
        name="out_proj",
    )(*stream, attn, pool, conv, w, g, *[w_ for w_, _, _ in casts])


def _ffn_kernel(h_ref, g_pre_ref, w_gate_ref, w_up_ref, w_down_ref, g_post_ref, o_ref, u_ref, acc_ref, *, layer):
    g_pre_ref, g_post_ref = _layer_row(g_pre_ref, layer), _layer_row(g_post_ref, layer)
    c = pl.program_id(1)

    @pl.when(c == 0)
    def _():
        u_ref[...] = _rms(h_ref[...], g_pre_ref[...]).astype(BF16)
        acc_ref[...] = jnp.zeros_like(acc_ref)

    u = u_ref[...]
    gate = jnp.dot(u, w_gate_ref[...], preferred_element_type=F32)
    up = jnp.dot(u, w_up_ref[...], preferred_element_type=F32)
    a = (gate * jax.nn.sigmoid(gate) * up).astype(BF16)
    tf = w_down_ref.shape[0]
    for n in range(0, acc_ref.shape[1], tf):
        acc_ref[:, n:n + tf] += jnp.dot(a, w_down_ref[:, n:n + tf], preferred_element_type=F32)

    @pl.when(c == pl.num_programs(1) - 1)
    def _():
        o_ref[...] = h_ref[...] + _rms(acc_ref[...], g_post_ref[...])


def _ffn(layer, h, g_pre, w_gate, w_up, w_down, g_post, keep=None):
    rows, d = h.shape
    d_ff = w_gate.shape[1]
    tf = FFN_COLS
    if keep is None:
        tm, n_tiles = FFN_ROWS, rows // FFN_ROWS
        h_spec = pl.BlockSpec((tm, d), lambda i, c: (i, 0))
    else:
        rows_per_batch, first, count = keep
        tm = LAST_FFN_ROWS
        per_batch = count // tm
        n_tiles = (rows // rows_per_batch) * per_batch
        h_spec = pl.BlockSpec((pl.Element(tm), pl.Element(d)),
                              lambda i, c: (pl.multiple_of(
                                  (i // per_batch) * rows_per_batch + first + (i % per_batch) * tm, SUBLANES), 0))
        assert rows_per_batch % SUBLANES == 0 and first % SUBLANES == 0
    return pl.pallas_call(
        functools.partial(_ffn_kernel, layer=layer),
        grid=(n_tiles, d_ff // tf),
        in_specs=[h_spec, _resident(g_pre.shape),
                  pl.BlockSpec((d, tf), lambda i, c: (0, c)), pl.BlockSpec((d, tf), lambda i, c: (0, c)),
                  pl.BlockSpec((tf, d), lambda i, c: (c, 0)), _resident(g_post.shape)],
        out_specs=pl.BlockSpec((tm, d), lambda i, c: (i, 0)),
        out_shape=jax.ShapeDtypeStruct((n_tiles * tm, d), F32),
        scratch_shapes=[pltpu.VMEM((tm, d), BF16), pltpu.VMEM((tm, d), F32)],
        compiler_params=_params(2),
        name="swiglu_ffn",
    )(h, g_pre, w_gate, w_up, w_down, g_post)


def kernel(x, meta_tokens, pre_mix_g, w_in, w_pool, pool_scale, w_dw, b_dw, conv_ln_g, conv_ln_b, w_pw, mix_out_g,
           w_out, post_mix_g, pre_ffn_g, w_gate, w_up, w_down, post_ffn_g):
    batch, seq, d = x.shape
    depth = w_in.shape[0]
    d_pool = pool_scale.shape[1]
    d_conv = w_pw.shape[1]
    d_attn = w_out.shape[1] - d_pool - d_conv
    assert d_pool == LANES * len(POOL_WINDOWS) and d_attn % (2 * LANES) == 0
    assert POOL_WINDOWS == tuple(2 ** (g + 1) for g in range(len(POOL_WINDOWS)))
    assert d_attn // N_HEADS_SB == LANES // 2

    length = N_META + seq
    rows_per_batch = -(-length // ROW_ALIGN) * ROW_ALIGN
    assert rows_per_batch % SEQ_TILE == 0 and (batch * rows_per_batch) % FFN_ROWS == 0 and seq % LAST_FFN_ROWS == 0
    assert meta_tokens.shape[0] == N_META
    h = _Tokens(x.reshape(batch * seq, d), meta_tokens.astype(x.dtype), seq, rows_per_batch)

    w_in_l = w_in[0].astype(BF16)

    for i in range(depth):
        q, k, v, o_pool, o_conv, w_gate_l, w_up_l = _proj_mix(
            i, h, pre_mix_g, w_in_l, w_pool, pool_scale, mix_out_g, w_dw, b_dw, conv_ln_g, conv_ln_b, w_pw, d_attn,
            rows_per_batch, [(w_gate, i, 2), (w_up, i, 2)])
        o_attn, w_out_l, *w_in_next = _attention(i, q, k, v, mix_out_g, batch, rows_per_batch,
                                                 [(w_out, i, 2)] + ([(w_in, i + 1, 2)] if i + 1 < depth else []))
        if w_in_next:
            w_in_l = w_in_next[0]
        h, w_down_l = _out_proj(i, o_attn, o_pool, o_conv, h, w_out_l, post_mix_g, [(w_down, i, 1)])
        keep = (rows_per_batch, N_META, seq) if i == depth - 1 else None
        h = _ffn(i, h, pre_ffn_g, w_gate_l, w_up_l, w_down_l, post_ffn_g, keep)

    return h.reshape(batch, seq, d)
```

```python
import functools
from typing import NamedTuple

import jax
import jax.numpy as jnp
from jax import lax
from jax.experimental import pallas as pl
from jax.experimental.pallas import tpu as pltpu

N_META = 16
N_HEADS_SB = 16
POOL_WINDOWS = (2, 4, 8, 16)
CONV_WIDTH = 31
EPS = 1e-6

LANES = 128
SUBLANES = 8
VMEM_BYTES_V7X = 64 * 1024 * 1024
VMEM_LIMIT = VMEM_BYTES_V7X - 6 * 1024 * 1024

SEQ_TILE = 128
TOP_ROWS = 32
PROJ_ROWS = 384
ROW_ALIGN = PROJ_ROWS
FFN_ROWS = 768
LAST_FFN_ROWS = 512
FFN_COLS = 512
CONV_HALO = 32
POOL_HALO = SUBLANES * len(POOL_WINDOWS)
CONV_CHUNK = 32
ATTN_CAST_BLOCK = 128
PROJ_CAST_BLOCK = 256

LOG2_E = 1.4426950408889634
LOG2_WEIGHT_FLOOR = -127.0

F32 = jnp.float32
BF16 = jnp.bfloat16


def _rms(x, g):
    return x * lax.rsqrt(jnp.mean(x * x, axis=-1, keepdims=True) + EPS) * g


def _resident(shape):
    return pl.BlockSpec(shape, lambda *_: (0,) * len(shape), pipeline_mode=pl.Buffered(1))


def _layer_resident(layer, shape):
    return pl.BlockSpec((None,) + tuple(shape), lambda *_: (layer,) + (0,) * len(shape),
                        pipeline_mode=pl.Buffered(1))


def _layer_row(ref, layer, lo=0, hi=None):
    return ref.at[layer:layer + 1, lo:ref.shape[1] if hi is None else hi]


def _params(n_grid_axes):
    return pltpu.CompilerParams(dimension_semantics=("arbitrary",) * n_grid_axes, vmem_limit_bytes=VMEM_LIMIT)


class _Tokens(NamedTuple):
    x: jax.Array
    meta: jax.Array
    seq: int
    rows_per_batch: int

    @property
    def shape(self):
        return (self.x.shape[0] // self.seq * self.rows_per_batch, self.x.shape[1])


def _stream_inputs(h, tm, tile_of=lambda i: i):
    if not isinstance(h, _Tokens):
        return [h], [pl.BlockSpec((tm, h.shape[1]), lambda i, *_: (tile_of(i), 0))], None
    seq, d = h.seq, h.x.shape[1]
    n_meta = h.meta.shape[0]
    tiles_per_batch = h.rows_per_batch // tm
    n_pad = h.rows_per_batch - n_meta - seq
    assert h.rows_per_batch % tm == 0 and n_pad <= tm and n_meta <= tm
    assert all(n % SUBLANES == 0 for n in (n_meta, n_pad, seq, tm))

    def start(i):
        b, t = i // tiles_per_batch, i % tiles_per_batch
        return pl.multiple_of(b * seq + jnp.clip(t * tm - n_meta, 0, seq - tm), SUBLANES)

    spec = pl.BlockSpec((pl.Element(tm), pl.Element(d)), lambda i, *_: (start(tile_of(i)), 0))
    return [h.x, h.meta], [spec, _resident(h.meta.shape)], (tiles_per_batch, n_pad)


def _stream_tile(stream_refs, layout, tile=None):
    if layout is None:
        return stream_refs[0][...]
    x_ref, meta_ref = stream_refs
    tiles_per_batch, n_pad = layout
    blk = x_ref[...]
    tm, d = blk.shape
    t = lax.rem(pl.program_id(0) if tile is None else tile, tiles_per_batch)
    first = jnp.concatenate([meta_ref[...], blk[:tm - meta_ref.shape[0]]], axis=0)
    last = jnp.concatenate([blk[n_pad:], jnp.zeros((n_pad, d), blk.dtype)], axis=0)
    return jnp.where(t == 0, first, jnp.where(t == tiles_per_batch - 1, last, blk))


def _cast_specs(w, layer, axis, block_size, step_of, n_steps):
    k, n = w.shape[1:]
    n_blocks = w.shape[axis] // block_size
    assert w.shape[axis] % block_size == 0 and n_blocks <= n_steps
    block = (block_size, n) if axis == 1 else (k, block_size)

    def index(*grid_ids):
        j = jnp.minimum(step_of(*grid_ids), n_blocks - 1)
        return (j, 0) if axis == 1 else (0, j)

    return (pl.BlockSpec((None,) + block, lambda *g: (layer,) + index(*g)), pl.BlockSpec(block, index),
            jax.ShapeDtypeStruct((k, n), BF16))


def _cast_blocks(cast_src, cast_dst):
    for src, dst in zip(cast_src, cast_dst):
        dst[...] = src[...].astype(BF16)


def _attn_kernel(*refs, layer, n_pairs, n_casts):
    q_ref, k_ref, v_ref, tri_ref, g_ref = refs[:5]
    g_ref = _layer_row(g_ref, layer, 0, n_pairs * LANES)
    o_ref = refs[5 + n_casts]
    q2_ref, acc_ref, c_ref = refs[6 + 2 * n_casts:]
    _cast_blocks(refs[5:5 + n_casts], refs[6 + n_casts:6 + 2 * n_casts])
    t = SEQ_TILE
    qi = pl.program_id(1)
    pairs = range(n_pairs)
    lanes = [slice(p * LANES, (p + 1) * LANES) for p in pairs]
    first_head = lax.broadcasted_iota(jnp.int32, (t, LANES), 1) < LANES // 2
    row = lax.broadcasted_iota(jnp.int32, (2 * t, t), 0)
    col = lax.broadcasted_iota(jnp.int32, (2 * t, t), 1)
    causal = col < (row & (t - 1))
    zero = jnp.zeros((), BF16)
    for p in pairs:
        qp = q_ref[:, lanes[p]]
        q2_ref[p] = jnp.concatenate([jnp.where(first_head, qp, zero), jnp.where(first_head, zero, qp)], axis=0)

    def head_rows(ref, p, r):
        return ref[p] if r == t else jnp.concatenate([ref[p, :r], ref[p, t:t + r]], axis=0)

    def key_tiles(kbs, r, diagonal=False):
        tiles = range(len(kbs))
        starts = [pl.multiple_of(kb * t, t) for kb in kbs]
        masked = [diagonal and j == 0 for j in tiles]
        z = [[lax.dot_general(head_rows(q2_ref, p, r), k_ref[pl.ds(starts[j], t), lanes[p]],
                              (((1,), (1,)), ((), ())), preferred_element_type=F32) for p in pairs] for j in tiles]
        v = [[v_ref[pl.ds(starts[j], t), lanes[p]] for p in pairs] for j in tiles]
        c = [None if diagonal else head_rows(c_ref, p, r) for p in pairs]
        log_beta, later, c_before = [], [], []
        for j in tiles:
            c_before.append(list(c))
            log_beta.append([])
            later.append([])
            for p in pairs:
                z_pos = jnp.maximum(z[j][p], 0.0)
                z_neg = z[j][p] - z_pos
                log_term = jnp.log(1.0 + jnp.exp2(z_neg - z_pos)) * LOG2_E
                soft = z_pos + log_term
                if masked[j]:
                    soft = jnp.where(causal, soft, 0.0)
                later[j].append(jnp.dot(soft.astype(BF16), tri_ref[...], preferred_element_type=F32))
                log_beta[j].append(z_neg - log_term)
                row_sum = jnp.sum(soft, axis=1, keepdims=True)
                c[p] = jnp.broadcast_to(row_sum, soft.shape) if c[p] is None else c[p] + row_sum
        o = [None] * n_pairs
        for j in tiles:
            for p in pairs:
                x = log_beta[j][p] - later[j][p]
                if c_before[j][p] is not None:
                    x = x - c_before[j][p]
                w = jnp.exp2(x)
                if masked[j]:
                    w = jnp.where(causal, w, 0.0)
                o_tile = jnp.dot(w.astype(BF16), v[j][p], preferred_element_type=F32)
                o[p] = o_tile if o[p] is None else o[p] + o_tile
        c_min = None
        for p in pairs:
            first = lax.broadcasted_iota(jnp.int32, (r, LANES), 1) < LANES // 2
            o_pair = jnp.where(first, o[p][:r], o[p][r:])
            if diagonal:
                acc_ref[:, lanes[p]] = o_pair
            else:
                acc_ref[:r, lanes[p]] += o_pair
            if r == t:
                c_ref[p] = c[p]
            else:
                c_ref[p, :r] = c[p][:r]
                c_ref[p, t:t + r] = c[p][r:]
            c_min = c[p] if c_min is None else jnp.minimum(c_min, c[p])
        if r < t:
            return jnp.min(c_min)
        top = jnp.minimum(jnp.min(c_min[:TOP_ROWS]), jnp.min(c_min[t:t + TOP_ROWS]))
        rest = jnp.minimum(jnp.min(c_min[TOP_ROWS:t]), jnp.min(c_min[t + TOP_ROWS:]))
        return top, rest

    def unfinished(c_min):
        return c_min < -LOG2_WEIGHT_FLOOR

    def full_body(s):
        kb, _, _ = s
        return (kb - 1,) + key_tiles([kb], t)

    def top_body(s):
        kb, _ = s
        return kb - 1, key_tiles([kb], TOP_ROWS)

    state = lax.cond(qi >= 1, lambda: (qi - 2,) + key_tiles([qi, qi - 1], t, diagonal=True),
                     lambda: (qi - 1,) + key_tiles([qi], t, diagonal=True))
    kb, top, _ = lax.while_loop(lambda s: jnp.logical_and(s[0] >= 0, unfinished(s[2])), full_body, state)
    lax.while_loop(lambda s: jnp.logical_and(s[0] >= 0, unfinished(s[1])), top_body, (kb, top))
    o_ref[...] = _rms(acc_ref[...], g_ref[...]).astype(BF16)


def _attention(layer, q, k, v, g, batch, rows_per_batch, casts):
    rows, d_attn = q.shape
    t = SEQ_TILE
    nq = rows_per_batch // t
    j = jnp.arange(t)
    tri = (j[:, None] > j[None, :]).astype(BF16)
    n_pairs = d_attn // LANES
    kv_spec = pl.BlockSpec((rows_per_batch, d_attn), lambda b, i: (b, 0), pipeline_mode=pl.Buffered(1))
    tile_spec = pl.BlockSpec((t, d_attn), lambda b, i: (b * nq + i, 0))
    cast_specs = [_cast_specs(w, l, axis, ATTN_CAST_BLOCK, lambda b, i: b * nq + i, batch * nq) for w, l, axis in casts]
    return pl.pallas_call(
        functools.partial(_attn_kernel, layer=layer, n_pairs=n_pairs, n_casts=len(casts)),
        grid=(batch, nq),
        in_specs=[tile_spec, kv_spec, kv_spec, _resident(tri.shape), _resident(g.shape)]
        + [c[0] for c in cast_specs],
        out_specs=[tile_spec] + [c[1] for c in cast_specs],
        out_shape=[jax.ShapeDtypeStruct((rows, d_attn), BF16)] + [c[2] for c in cast_specs],
        scratch_shapes=[pltpu.VMEM((n_pairs, 2 * t, LANES), BF16), pltpu.VMEM((t, d_attn), F32),
                        pltpu.VMEM((n_pairs, 2 * t, t), F32)],
        compiler_params=_params(2),
        name="sb_attention",
    )(q, k, v, tri, g, *[w for w, _, _ in casts])


def _digest(v):
    v = v.astype(F32)
    rows, cols = v.shape
    top = jnp.max(v.reshape(rows // SUBLANES, SUBLANES, cols), axis=0)
    return functools.reduce(jnp.maximum, [top[:, c:c + LANES] for c in range(0, cols, LANES)])


def _zero_after(digests):
    bits = pltpu.bitcast(functools.reduce(jnp.maximum, digests), jnp.uint32)
    zero = ((bits >> 16) >> 16).astype(F32)
    return jnp.concatenate([zero, zero], axis=0).astype(BF16)


def _proj_mix_kernel(*refs, layer, n_stream, layout, n_casts, n_tiles, tiles_per_batch, d_attn, d_pool, d_conv,
                     q_scale):
    ns = n_stream
    g_ref, w_ref, w_pool_ref, pool_scale_ref, g_mix_ref, w_dw_ref, b_dw_ref, ln_g_ref, ln_b_ref, w_pw_ref = refs[ns:ns + 10]
    g_ref, pool_scale_ref, b_dw_ref, ln_g_ref, ln_b_ref = (
        _layer_row(r, layer) for r in (g_ref, pool_scale_ref, b_dw_ref, ln_g_ref, ln_b_ref))
    g_pool_ref = _layer_row(g_mix_ref, layer, d_attn, d_attn + d_pool)
    g_conv_ref = _layer_row(g_mix_ref, layer, d_attn + d_pool)
    cast_src = refs[ns + 10:ns + 10 + n_casts]
    q_ref, k_ref, v_ref, o_pool_ref, o_conv_ref, cast_dst = refs[ns + 10 + n_casts:ns + 16 + n_casts]
    pool_ext, pool_sums, conv_ext, y_ref = refs[ns + 16 + n_casts:]
    i = pl.program_id(0)
    tile = jnp.minimum(i, n_tiles - 1)
    tm = q_ref.shape[0]

    @pl.when(i == 0)
    def _():
        pool_ext[...] = jnp.zeros_like(pool_ext)
        conv_ext[0] = jnp.zeros(conv_ext.shape[1:], F32)

    def mix_pool():
        n_pool = POOL_HALO + tm
        level, window_sum = pool_ext, []
        for s in range(len(POOL_WINDOWS)):
            start, shift, lo = SUBLANES * (s + 1), 2 ** s, s * LANES
            both = level[start:n_pool, lo:] + level[start - shift:n_pool - shift, lo:]
            window_sum.append(both[POOL_HALO - start:, :LANES])
            if s + 1 < len(POOL_WINDOWS):
                pool_sums[s, start:n_pool, lo:] = both
                level = pool_sums.at[s]
        ti = lax.rem(i + tiles_per_batch - 1, tiles_per_batch)
        pos = ti * tm + lax.broadcasted_iota(jnp.int32, (tm, 1), 0)
        mixed = []
        sq = jnp.zeros((tm, 1), F32)
        for gi, window in enumerate(POOL_WINDOWS):
            lanes = slice(gi * LANES, (gi + 1) * LANES)
            count = jnp.minimum(pos + 1, window).astype(F32)
            pooled = window_sum[gi] / count - pool_ext[POOL_HALO:, lanes]
            m = jnp.dot(pooled.astype(BF16), w_pool_ref[gi].astype(BF16), preferred_element_type=F32)
            m = m * pool_scale_ref[:, lanes]
            sq = sq + jnp.sum(m * m, axis=-1, keepdims=True)
            mixed.append(m)
        inv = lax.rsqrt(sq / d_pool + EPS)
        out = [(m * inv * g_pool_ref[:, gi * LANES:(gi + 1) * LANES]).astype(BF16) for gi, m in enumerate(mixed)]
        for gi, o in enumerate(out):
            o_pool_ref[:, gi * LANES:(gi + 1) * LANES] = o
        return [_digest(o) for o in out]

    n_ext = CONV_HALO + tm

    def conv_copies():
        digests = []
        for r in range(1, SUBLANES):
            shifted = conv_ext[0, r:r + n_ext - SUBLANES, :]
            conv_ext[r, :n_ext - SUBLANES, :] = shifted
            digests.append(_digest(shifted))
        return digests

    first_tap = CONV_HALO - (CONV_WIDTH - 1)

    def conv_rows(base):
        acc = jnp.broadcast_to(b_dw_ref[...], (CONV_CHUNK, d_conv))
        for tap in range(CONV_WIDTH):
            shift = (first_tap + tap) % SUBLANES
            lo = base + first_tap + tap - shift
            acc = acc + w_dw_ref[tap:tap + 1, :] * conv_ext[shift, lo:lo + CONV_CHUNK, :]
        y_ref[base:base + CONV_CHUNK, :] = acc
        return _digest(acc)

    def conv_finish():
        y = y_ref[...]
        mu = jnp.mean(y, axis=-1, keepdims=True)
        yc = y - mu
        var = jnp.mean(yc * yc, axis=-1, keepdims=True)
        y = yc * lax.rsqrt(var + EPS) * ln_g_ref[...] + ln_b_ref[...]
        y = y * jax.nn.sigmoid(y)
        o = jnp.dot(y.astype(BF16), w_pw_ref[...].astype(BF16), preferred_element_type=F32)
        o = _rms(o, g_conv_ref[...]).astype(BF16)
        o_conv_ref[...] = o
        return [_digest(o)]

    chunks = list(range(0, tm, CONV_CHUNK))
    per_group = -(-len(chunks) // 4)
    conv_group = lambda g: [conv_rows(base) for base in chunks[g * per_group:(g + 1) * per_group]]
    pieces = [mix_pool, conv_copies] + [functools.partial(conv_group, g) for g in range(4)] + [conv_finish]

    def project(do_mix):
        u = _rms(_stream_tile(refs[:ns], layout, tile), g_ref[...]).astype(BF16)

        def proj(lo, hi, wait_for=()):
            lhs = u
            if wait_for:
                head = jnp.concatenate([u[:16, :LANES] + _zero_after(wait_for), u[:16, LANES:]], axis=1)
                lhs = jnp.concatenate([head, u[16:]], axis=0)
            return jnp.dot(lhs, w_ref[:, lo:hi], preferred_element_type=F32)

        mix = (lambda k: pieces[k]()) if do_mix else (lambda k: [])
        q_ref[...] = (proj(0, d_attn) * q_scale).astype(BF16)
        done = mix(0)
        k_ref[...] = proj(d_attn, 2 * d_attn, done).astype(BF16)
        done = mix(1) + mix(2)
        v_ref[...] = proj(2 * d_attn, 3 * d_attn, done).astype(BF16)
        done = mix(3) + mix(4)
        o = 3 * d_attn
        new_pool = proj(o, o + d_pool, done)
        done = mix(5)
        o += d_pool
        conv_value = proj(o, o + d_conv, done)
        done = mix(6)
        new_conv = conv_value * jax.nn.sigmoid(proj(o + d_conv, o + 2 * d_conv, done))
        cast_dst[...] = jnp.concatenate([src[...].astype(BF16) for src in cast_src], axis=1)

        starts_batch = lax.rem(tile, tiles_per_batch) == 0
        pool_ext[:POOL_HALO, :] = jnp.where(starts_batch, 0.0, pool_ext[tm:, :])
        pool_ext[POOL_HALO:, :] = new_pool
        conv_ext[0, :CONV_HALO, :] = jnp.where(starts_batch, 0.0, conv_ext[0, tm:, :])
        conv_ext[0, CONV_HALO:, :] = new_conv

    def mix_only():
        for piece in pieces:
            piece()

    pl.when(i == 0)(functools.partial(project, False))
    pl.when(jnp.logical_and(i > 0, i < n_tiles))(functools.partial(project, True))
    pl.when(i == n_tiles)(mix_only)


def _proj_mix(layer, h, g, w, w_pool, pool_scale, g_mix, w_dw, b_dw, ln_g, ln_b, w_pw, d_attn, rows_per_batch, casts):
    rows, d = h.shape
    d_pool, d_conv = pool_scale.shape[-1], w_pw.shape[-1]
    tm = PROJ_ROWS
    n = rows // tm
    head_dim = d_attn // N_HEADS_SB
    tile_of = lambda i: jnp.minimum(i, n - 1)
    cur = lambda width: pl.BlockSpec((tm, width), lambda i: (tile_of(i), 0))
    prev = lambda width: pl.BlockSpec((tm, width), lambda i: (jnp.maximum(i - 1, 0), 0))
    lr = functools.partial(_layer_resident, layer)
    stream, stream_specs, layout = _stream_inputs(h, tm, tile_of)
    cast_specs = [_cast_specs(w_, l, 2, PROJ_CAST_BLOCK, lambda i: i, n + 1) for w_, l in casts]
    cast_k, cast_n = casts[0][0].shape[1:]
    assert all(w_.shape[1:] == (cast_k, cast_n) for w_, _ in casts)
    last_block = cast_n // PROJ_CAST_BLOCK - 1
    cast_out = pl.BlockSpec((cast_k, len(casts) * PROJ_CAST_BLOCK), lambda i: (0, jnp.minimum(i, last_block)))
    return pl.pallas_call(
        functools.partial(_proj_mix_kernel, layer=layer, n_stream=len(stream), layout=layout, n_casts=len(casts), n_tiles=n,
                          tiles_per_batch=rows_per_batch // tm, d_attn=d_attn, d_pool=d_pool, d_conv=d_conv,
                          q_scale=LOG2_E * head_dim ** -0.5),
        grid=(n + 1,),
        in_specs=stream_specs + [_resident(g.shape), _resident(w.shape), lr(w_pool.shape[1:]),
                                 _resident(pool_scale.shape), _resident(g_mix.shape), lr(w_dw.shape[1:]),
                                 _resident(b_dw.shape), _resident(ln_g.shape), _resident(ln_b.shape),
                                 lr(w_pw.shape[1:])] + [c[0] for c in cast_specs],
        out_specs=[cur(d_attn), cur(d_attn), cur(d_attn), prev(d_pool), prev(d_conv), cast_out],
        out_shape=[jax.ShapeDtypeStruct((rows, d_attn), BF16)] * 3
        + [jax.ShapeDtypeStruct((rows, d_pool), BF16), jax.ShapeDtypeStruct((rows, d_conv), BF16),
           jax.ShapeDtypeStruct((cast_k, len(casts) * cast_n), BF16)],
        scratch_shapes=[pltpu.VMEM((POOL_HALO + tm, d_pool), F32),
                        pltpu.VMEM((len(POOL_WINDOWS) - 1, POOL_HALO + tm, d_pool), F32),
                        pltpu.VMEM((SUBLANES, CONV_HALO + tm, d_conv), F32), pltpu.VMEM((tm, d_conv), F32)],
        compiler_params=_params(1),
        name="in_proj_mixers",
    )(*stream, g, w, w_pool, pool_scale, g_mix, w_dw, b_dw, ln_g, ln_b, w_pw, *[w_ for w_, _ in casts])


def _out_proj_kernel(*refs, layer, n_stream, layout, n_casts):
    attn_ref, pool_ref, conv_ref, w_ref, g_ref = refs[n_stream:n_stream + 5]
    g_ref = _layer_row(g_ref, layer)
    o_ref = refs[n_stream + 5 + n_casts]
    _cast_blocks(refs[n_stream + 5:n_stream + 5 + n_casts], refs[n_stream + 6 + n_casts:])
    merged = jnp.concatenate([attn_ref[...], pool_ref[...], conv_ref[...]], axis=1)
    y = jnp.dot(merged, w_ref[...], preferred_element_type=F32)
    o_ref[...] = _stream_tile(refs[:n_stream], layout) + _rms(y, g_ref[...])


def _out_proj(layer, attn, pool, conv, h, w, g, casts):
    rows, d = h.shape
    tm = PROJ_ROWS
    row_spec = lambda n: pl.BlockSpec((tm, n), lambda i: (i, 0))
    stream, stream_specs, layout = _stream_inputs(h, tm)
    cast_specs = [_cast_specs(w_, l, axis, PROJ_CAST_BLOCK, lambda i: i, rows // tm) for w_, l, axis in casts]
    return pl.pallas_call(
        functools.partial(_out_proj_kernel, layer=layer, n_stream=len(stream), layout=layout, n_casts=len(casts)),
        grid=(rows // tm,),
        in_specs=stream_specs + [row_spec(attn.shape[1]), row_spec(pool.shape[1]), row_spec(conv.shape[1]),
                                 _resident(w.shape), _resident(g.shape)] + [c[0] for c in cast_specs],
        out_specs=[row_spec(d)] + [c[1] for c in cast_specs],
        out_shape=[jax.ShapeDtypeStruct((rows, d), F32)] + [c[2] for c in cast_specs],
        compiler_params=_params(1),
        name="out_proj",
    )(*stream, attn, pool, conv, w, g, *[w_ for w_, _, _ in casts])


def _ffn_kernel(h_ref, g_pre_ref, w_gate_up_ref, w_down_ref, g_post_ref, o_ref, u_ref, acc_ref, *, layer):
    g_pre_ref, g_post_ref = _layer_row(g_pre_ref, layer), _layer_row(g_post_ref, layer)
    c = pl.program_id(1)

    @pl.when(c == 0)
    def _():
        u_ref[...] = _rms(h_ref[...], g_pre_ref[...]).astype(BF16)
        acc_ref[...] = jnp.zeros_like(acc_ref)

    u = u_ref[...]
    both = jnp.dot(u, w_gate_up_ref[...], preferred_element_type=F32)
    b = PROJ_CAST_BLOCK
    gate = jnp.concatenate([both[:, k:k + b] for k in range(0, both.shape[1], 2 * b)], axis=1)
    up = jnp.concatenate([both[:, k + b:k + 2 * b] for k in range(0, both.shape[1], 2 * b)], axis=1)
    a = (gate * jax.nn.sigmoid(gate) * up).astype(BF16)
    tf = w_down_ref.shape[0]
    for n in range(0, acc_ref.shape[1], tf):
        acc_ref[:, n:n + tf] += jnp.dot(a, w_down_ref[:, n:n + tf], preferred_element_type=F32)

    @pl.when(c == pl.num_programs(1) - 1)
    def _():
        o_ref[...] = h_ref[...] + _rms(acc_ref[...], g_post_ref[...])


def _ffn(layer, h, g_pre, w_gate_up, w_down, g_post, keep=None):
    rows, d = h.shape
    d_ff = w_down.shape[0]
    tf = FFN_COLS
    assert tf % PROJ_CAST_BLOCK == 0 and w_gate_up.shape[1] == 2 * d_ff
    if keep is None:
        tm, n_tiles = FFN_ROWS, rows // FFN_ROWS
        h_spec = pl.BlockSpec((tm, d), lambda i, c: (i, 0))
    else:
        rows_per_batch, first, count = keep
        tm = LAST_FFN_ROWS
        per_batch = count // tm
        n_tiles = (rows // rows_per_batch) * per_batch
        h_spec = pl.BlockSpec((pl.Element(tm), pl.Element(d)),
                              lambda i, c: (pl.multiple_of(
                                  (i // per_batch) * rows_per_batch + first + (i % per_batch) * tm, SUBLANES), 0))
        assert rows_per_batch % SUBLANES == 0 and first % SUBLANES == 0
    return pl.pallas_call(
        functools.partial(_ffn_kernel, layer=layer),
        grid=(n_tiles, d_ff // tf),
        in_specs=[h_spec, _resident(g_pre.shape),
                  pl.BlockSpec((d, 2 * tf), lambda i, c: (0, c)), pl.BlockSpec((tf, d), lambda i, c: (c, 0)),
                  _resident(g_post.shape)],
        out_specs=pl.BlockSpec((tm, d), lambda i, c: (i, 0)),
        out_shape=jax.ShapeDtypeStruct((n_tiles * tm, d), F32),
        scratch_shapes=[pltpu.VMEM((tm, d), BF16), pltpu.VMEM((tm, d), F32)],
        compiler_params=_params(2),
        name="swiglu_ffn",
    )(h, g_pre, w_gate_up, w_down, g_post)


def kernel(x, meta_tokens, pre_mix_g, w_in, w_pool, pool_scale, w_dw, b_dw, conv_ln_g, conv_ln_b, w_pw, mix_out_g,
           w_out, post_mix_g, pre_ffn_g, w_gate, w_up, w_down, post_ffn_g):
    batch, seq, d = x.shape
    depth = w_in.shape[0]
    d_pool = pool_scale.shape[1]
    d_conv = w_pw.shape[1]
    d_attn = w_out.shape[1] - d_pool - d_conv
    assert d_pool == LANES * len(POOL_WINDOWS) and d_attn % (2 * LANES) == 0
    assert POOL_WINDOWS == tuple(2 ** (g + 1) for g in range(len(POOL_WINDOWS)))
    assert d_attn // N_HEADS_SB == LANES // 2

    length = N_META + seq
    rows_per_batch = -(-length // ROW_ALIGN) * ROW_ALIGN
    assert rows_per_batch % SEQ_TILE == 0 and (batch * rows_per_batch) % FFN_ROWS == 0 and seq % LAST_FFN_ROWS == 0
    assert meta_tokens.shape[0] == N_META
    h = _Tokens(x.reshape(batch * seq, d), meta_tokens.astype(x.dtype), seq, rows_per_batch)

    w_in_l = w_in[0].astype(BF16)

    for i in range(depth):
        q, k, v, o_pool, o_conv, w_gate_up_l = _proj_mix(
            i, h, pre_mix_g, w_in_l, w_pool, pool_scale, mix_out_g, w_dw, b_dw, conv_ln_g, conv_ln_b, w_pw, d_attn,
            rows_per_batch, [(w_gate, i), (w_up, i)])
        o_attn, w_out_l, *w_in_next = _attention(i, q, k, v, mix_out_g, batch, rows_per_batch,
                                                 [(w_out, i, 2)] + ([(w_in, i + 1, 2)] if i + 1 < depth else []))
        if w_in_next:
            w_in_l = w_in_next[0]
        h, w_down_l = _out_proj(i, o_attn, o_pool, o_conv, h, w_out_l, post_mix_g, [(w_down, i, 1)])
        keep = (rows_per_batch, N_META, seq) if i == depth - 1 else None
        h = _ffn(i, h, pre_ffn_g, w_gate_up_l, w_down_l, post_ffn_g, keep)

    return h.reshape(batch, seq, d)
```
